```python
import math
import jax, jax.numpy as jnp
from jax import lax
import numpy as np

D_MODEL = 1024
BATCH = 8
SEQ = 8192
DEPTH = 1

GRID_W = 64
CTX_LEN = 256
DA_HEADS = 8
DA_DH = 64
DA_DV = 2 * DA_DH
Q_BLOCK = 128
GLA_HEADS = 4
GLA_DK = D_MODEL // (2 * GLA_HEADS)
GLA_DV = D_MODEL // GLA_HEADS
GLA_GATE_RANK = 16
GLA_GATE_NORM = 16.0
GLA_CHUNK = 64
D_FF = 4 * D_MODEL
ROPE_THETA = 10000.0
ROPE_AXIS_DIM = DA_DH // 2
N_BRANCH = 2
N_MOD = 6
EPS = 1e-6
IN_WIDTHS = (DA_HEADS * 2 * DA_DH,
             DA_HEADS * 2 * DA_DH,
             DA_HEADS * DA_DV,
             GLA_HEADS * GLA_DK,
             GLA_HEADS * GLA_DK,
             GLA_HEADS * GLA_DV,
             2 * GLA_GATE_RANK,
             GLA_HEADS * GLA_DV,
             N_BRANCH * D_MODEL)
D_IN = sum(IN_WIDTHS)

kernel_name = 'hybrid_diffattn_gla_dit_block'


def rms_norm(x, gain):
    xf = x.astype(jnp.float32)
    y = xf * lax.rsqrt(jnp.mean(xf * xf, axis=-1, keepdims=True) + EPS)
    return (y * gain.astype(jnp.float32)).astype(x.dtype)


def modulate(x, shift, scale):
    return x * (1.0 + scale) + shift


def axial_angles(T):
    rows = T // GRID_W
    r, col = jnp.meshgrid(jnp.arange(rows), jnp.arange(GRID_W), indexing='ij')
    inv = ROPE_THETA ** (-jnp.arange(0, ROPE_AXIS_DIM, 2, dtype=jnp.float32) / ROPE_AXIS_DIM)
    ang_r = r.reshape(-1, 1).astype(jnp.float32) * inv
    ang_c = col.reshape(-1, 1).astype(jnp.float32) * inv
    return ang_r, ang_c


def rotate_half_pairs(x, ang):
    cos = jnp.cos(ang)[:, None, None, :].astype(x.dtype)
    sin = jnp.sin(ang)[:, None, None, :].astype(x.dtype)
    x1, x2 = jnp.split(x, 2, axis=-1)
    return jnp.concatenate([x1 * cos - x2 * sin, x2 * cos + x1 * sin], axis=-1)


def axial_rope(x, ang_r, ang_c):
    return jnp.concatenate([rotate_half_pairs(x[..., :ROPE_AXIS_DIM], ang_r),
                            rotate_half_pairs(x[..., ROPE_AXIS_DIM:], ang_c)], axis=-1)


def diff_attend(q, k, v, lam):
    B, Tq = q.shape[:2]
    nblk = Tq // Q_BLOCK
    qb = q.reshape(B, nblk, Q_BLOCK, DA_HEADS, 2, DA_DH).swapaxes(0, 1)
    scale = DA_DH ** -0.5

    def block(qi):
        s = jnp.einsum('bqhcd,bkhcd->bhcqk', qi, k).astype(jnp.float32) * scale
        p = jax.nn.softmax(s, axis=-1)
        p = p[:, :, 0] - lam * p[:, :, 1]
        return jnp.einsum('bhqk,bkhe->bqhe', p.astype(v.dtype), v)

    o = lax.map(block, qb)
    return o.swapaxes(0, 1).reshape(B, Tq, DA_HEADS, DA_DV)


def gla_chunk_scan(q, k, v, g, state):
    B, H, T, _ = q.shape
    dv = v.shape[-1]
    n = T // GLA_CHUNK

    def chunks(a):
        return jnp.moveaxis(a.astype(jnp.float32).reshape(B, H, n, GLA_CHUNK, a.shape[-1]), 2, 0)

    tri = jnp.tril(jnp.ones((GLA_CHUNK, GLA_CHUNK), dtype=bool))[:, :, None]

    def step(S, inp):
        qc, kc, vc, gc = inp
        b = jnp.cumsum(gc, axis=2)
        o_inter = jnp.einsum('bhik,bhkv->bhiv', qc * jnp.exp(b), S)
        diff = b[:, :, :, None, :] - b[:, :, None, :, :]
        decay = jnp.where(tri, jnp.exp(jnp.where(tri, diff, 0.0)), 0.0)
        A = jnp.einsum('bhik,bhjk,bhijk->bhij', qc, kc, decay)
        o_intra = jnp.einsum('bhij,bhjv->bhiv', A, vc)
        b_last = b[:, :, -1:]
        S = jnp.exp(b_last)[:, :, 0, :, None] * S + jnp.einsum(
            'bhjk,bhjv->bhkv', kc * jnp.exp(b_last - b), vc)
        return S, o_inter + o_intra

    S, o = lax.scan(step, state.astype(jnp.float32), (chunks(q), chunks(k), chunks(v), chunks(g)))
    return jnp.moveaxis(o, 0, 2).reshape(B, H, T, dv), S


def gla_final_state(k, v, g):
    b = jnp.cumsum(g.astype(jnp.float32), axis=2)
    w = jnp.exp(b[:, :, -1:] - b)
    return jnp.einsum('bhtk,bhtv->bhkv', k.astype(jnp.float32) * w, v.astype(jnp.float32))


def flip_t(a):
    return jnp.flip(a, axis=2)


def gla_bidir(q, k, v, g_f, g_b, S_f, S_b):
    o_f, _ = gla_chunk_scan(q, k, v, g_f, S_f)
    o_b, _ = gla_chunk_scan(flip_t(q), flip_t(k), flip_t(v), flip_t(g_b), S_b)
    return o_f + flip_t(o_b)


def project(h, w_in, w_gate_up, b_gate_up):
    B, T = h.shape[:2]
    parts = jnp.split(h @ w_in, np.cumsum(IN_WIDTHS)[:-1].tolist(), axis=-1)
    da_q, da_k, da_v, gq, gk, gv, g_low, g_out, merge = parts
    da_q = da_q.reshape(B, T, DA_HEADS, 2, DA_DH)
    da_k = da_k.reshape(B, T, DA_HEADS, 2, DA_DH)
    da_v = da_v.reshape(B, T, DA_HEADS, DA_DV)

    def heads(a, d):
        return a.reshape(B, T, GLA_HEADS, d).transpose(0, 2, 1, 3)

    gq = heads(gq, GLA_DK) * (GLA_DK ** -0.5)
    gk = heads(gk, GLA_DK)
    gv = heads(gv, GLA_DV)
    gate = jnp.einsum('btzr,zrk->btzk', g_low.reshape(B, T, 2, GLA_GATE_RANK), w_gate_up) + b_gate_up
    gate = jax.nn.log_sigmoid(gate.astype(jnp.float32)) / GLA_GATE_NORM
    g_f = heads(gate[:, :, 0], GLA_DK)
    g_b = heads(gate[:, :, 1], GLA_DK)
    return da_q, da_k, da_v, gq, gk, gv, g_f, g_b, g_out, merge


def finish_and_merge(o_da, o_gla, g_out, merge, p, lam_init):
    B, T = o_da.shape[:2]
    y_da = (rms_norm(o_da, p['da_head_norm']) * (1.0 - lam_init)).reshape(B, T, DA_HEADS * DA_DV)
    o_gla = jnp.swapaxes(o_gla, 1, 2).astype(g_out.dtype)
    y_gla = (rms_norm(o_gla, p['gla_head_norm'])
             * jax.nn.silu(g_out.reshape(B, T, GLA_HEADS, GLA_DV))).reshape(B, T, GLA_HEADS * GLA_DV)
    gates = jax.nn.sigmoid(merge).reshape(B, T, N_BRANCH, D_MODEL)
    y = gates[:, :, 0] * (y_da @ p['w_branch_da']) + gates[:, :, 1] * (y_gla @ p['w_branch_gla'])
    return y @ p['w_out']


def sq_relu_mlp(h, w_ff1, w_ff2):
    return jnp.square(jax.nn.relu(h @ w_ff1)) @ w_ff2


def trunk_layer(x, ctx, mod_lat, mod_ctx, p, layer_idx, update_ctx):
    T = x.shape[1]
    lam_init = 0.8 - 0.6 * math.exp(-0.3 * layer_idx)
    f32 = jnp.float32
    lam = (jnp.exp(jnp.sum(p['lambda_q1'].astype(f32) * p['lambda_k1'].astype(f32)))
           - jnp.exp(jnp.sum(p['lambda_q2'].astype(f32) * p['lambda_k2'].astype(f32))) + lam_init)
    sh1, sc1, gt1, sh2, sc2, gt2 = jnp.split(mod_lat, N_MOD, axis=-1)
    csh1, csc1, cgt1, csh2, csc2, cgt2 = jnp.split(mod_ctx, N_MOD, axis=-1)

    h = modulate(rms_norm(x, p['pre_norm1']), sh1, sc1)
    hc = modulate(rms_norm(ctx, p['pre_norm1']), csh1, csc1)
    da_q, da_k, da_v, gq, gk, gv, g_f, g_b, g_out, merge = project(h, p['w_in'], p['w_gate_up'], p['b_gate_up'])
    cda_q, cda_k, cda_v, cgq, cgk, cgv, cg_f, cg_b, cg_out, cmerge = project(hc, p['w_in'], p['w_gate_up'], p['b_gate_up'])

    ang_r, ang_c = axial_angles(T)
    q_lat = axial_rope(da_q, ang_r, ang_c)
    k_lat = axial_rope(da_k, ang_r, ang_c)
    k_all = jnp.concatenate([k_lat, cda_k], axis=1)
    v_all = jnp.concatenate([da_v, cda_v], axis=1)
    o_da = diff_attend(q_lat, k_all, v_all, lam)

    S_f = gla_final_state(cgk, cgv, cg_f)
    S_b = gla_final_state(flip_t(cgk), flip_t(cgv), flip_t(cg_b))
    o_gla = gla_bidir(gq, gk, gv, g_f, g_b, S_f, S_b)

    y = finish_and_merge(o_da, o_gla, g_out, merge, p, lam_init)
    x_new = x + gt1 * rms_norm(y, p['post_norm1'])

    h2 = modulate(rms_norm(x_new, p['pre_norm2']), sh2, sc2)
    x_new = x_new + gt2 * rms_norm(sq_relu_mlp(h2, p['w_ff1'], p['w_ff2']), p['post_norm2'])

    if update_ctx:
        o_da_c = diff_attend(cda_q, cda_k, cda_v, lam)
        zero_state = jnp.zeros_like(S_f)
        o_gla_c = gla_bidir(cgq, cgk, cgv, cg_f, cg_b, zero_state, zero_state)
        yc = finish_and_merge(o_da_c, o_gla_c, cg_out, cmerge, p, lam_init)
        ctx = ctx + cgt1 * rms_norm(yc, p['post_norm1'])
        hc2 = modulate(rms_norm(ctx, p['pre_norm2']), csh2, csc2)
        ctx = ctx + cgt2 * rms_norm(sq_relu_mlp(hc2, p['w_ff1'], p['w_ff2']), p['post_norm2'])
    return x_new, ctx


def setup_inputs(seed: int = 0) -> dict:
    key = jax.random.key(seed)
    ks = jax.random.split(key, 26)
    D = D_MODEL
    nrm = jax.random.normal
    f32 = jnp.float32

    def gain(k, d):
        return 1.0 + 0.05 * nrm(k, (DEPTH, d), f32)

    return {
        'x': nrm(ks[0], (BATCH, SEQ, D), f32),
        'c': nrm(ks[1], (BATCH, D), f32),
        'ctx': nrm(ks[2], (BATCH, CTX_LEN, D), f32),
        'c_ctx': nrm(ks[3], (D,), f32),
        'w_mod': nrm(ks[4], (DEPTH, D, N_MOD * D), f32) * (0.5 * D ** -0.5),
        'b_mod': 0.01 * nrm(ks[5], (DEPTH, N_MOD * D), f32),
        'pre_norm1': gain(ks[6], D),
        'w_in': nrm(ks[7], (DEPTH, D, D_IN), f32) * D ** -0.5,
        'w_gate_up': nrm(ks[8], (DEPTH, 2, GLA_GATE_RANK, GLA_HEADS * GLA_DK), f32) * GLA_GATE_RANK ** -0.5,
        'b_gate_up': 0.1 * nrm(ks[9], (DEPTH, 2, GLA_HEADS * GLA_DK), f32),
        'lambda_q1': 0.1 * nrm(ks[10], (DEPTH, DA_DH), f32),
        'lambda_k1': 0.1 * nrm(ks[11], (DEPTH, DA_DH), f32),
        'lambda_q2': 0.1 * nrm(ks[12], (DEPTH, DA_DH), f32),
        'lambda_k2': 0.1 * nrm(ks[13], (DEPTH, DA_DH), f32),
        'da_head_norm': gain(ks[14], DA_DV),
        'gla_head_norm': gain(ks[15], GLA_DV),
        'w_branch_da': nrm(ks[16], (DEPTH, DA_HEADS * DA_DV, D), f32) * (DA_HEADS * DA_DV) ** -0.5,
        'w_branch_gla': nrm(ks[17], (DEPTH, GLA_HEADS * GLA_DV, D), f32) * (GLA_HEADS * GLA_DV) ** -0.5,
        'w_out': nrm(ks[18], (DEPTH, D, D), f32) * D ** -0.5,
        'post_norm1': gain(ks[19], D),
        'pre_norm2': gain(ks[20], D),
        'w_ff1': nrm(ks[21], (DEPTH, D, D_FF), f32) * D ** -0.5,
        'w_ff2': nrm(ks[22], (DEPTH, D_FF, D), f32) * D_FF ** -0.5,
        'post_norm2': gain(ks[23], D),
    }


def reference(x, c, ctx, c_ctx, w_mod, b_mod, pre_norm1, w_in, w_gate_up, b_gate_up,
              lambda_q1, lambda_k1, lambda_q2, lambda_k2, da_head_norm, gla_head_norm,
              w_branch_da, w_branch_gla, w_out, post_norm1, pre_norm2, w_ff1, w_ff2, post_norm2):
    for layer in range(DEPTH):
        mod_lat = (jax.nn.silu(c) @ w_mod[layer] + b_mod[layer])[:, None, :]
        mod_ctx = (jax.nn.silu(c_ctx) @ w_mod[layer] + b_mod[layer])[None, None, :]
        p = {
            'pre_norm1': pre_norm1[layer], 'w_in': w_in[layer],
            'w_gate_up': w_gate_up[layer], 'b_gate_up': b_gate_up[layer],
            'lambda_q1': lambda_q1[layer], 'lambda_k1': lambda_k1[layer],
            'lambda_q2': lambda_q2[layer], 'lambda_k2': lambda_k2[layer],
            'da_head_norm': da_head_norm[layer], 'gla_head_norm': gla_head_norm[layer],
            'w_branch_da': w_branch_da[layer], 'w_branch_gla': w_branch_gla[layer],
            'w_out': w_out[layer], 'post_norm1': post_norm1[layer],
            'pre_norm2': pre_norm2[layer], 'w_ff1': w_ff1[layer], 'w_ff2': w_ff2[layer],
            'post_norm2': post_norm2[layer],
        }
        x, ctx = trunk_layer(x, ctx, mod_lat, mod_ctx, p, layer, layer + 1 < DEPTH)
    return x
```

```python
import functools
import math

import jax
import jax.numpy as jnp
from jax import lax
from jax.experimental import pallas as pl
from jax.experimental.pallas import tpu as pltpu

F32 = jnp.float32
BF16 = jnp.bfloat16

GRID_W = 64
DA_HEADS = 8
DA_DH = 64
DA_DV = 2 * DA_DH
GLA_HEADS = 4
GLA_DK = 128
GLA_DV = 256
GLA_GATE_RANK = 16
GLA_GATE_NORM = 16.0
ROPE_THETA = 10000.0
ROPE_AXIS_DIM = DA_DH // 2
ROPE_HALF = ROPE_AXIS_DIM // 2
N_MOD = 6
EPS = 1e-6
NEG_BIG = -1e30

LANES = 128
SUBLANES = 8
VMEM_LIMIT = 56 * 1024 * 1024

GLA_CHUNK = 128
LOW_PAD = LANES


def _nt(a, b):
    return lax.dot_general(a, b, (((1,), (1,)), ((), ())), preferred_element_type=F32)


def _tn(a, b):
    return lax.dot_general(a, b, (((0,), (0,)), ((), ())), preferred_element_type=F32)


def _mm(a, b):
    return jnp.dot(a, b, preferred_element_type=F32)


def _rms(x, gain):
    return x * lax.rsqrt(jnp.mean(x * x, axis=-1, keepdims=True) + EPS) * gain


def _params(*sem):
    return pltpu.CompilerParams(dimension_semantics=sem, vmem_limit_bytes=VMEM_LIMIT)


def _resident(shape):
    nd = len(shape)
    return pl.BlockSpec(shape, lambda *_: (0,) * nd, pipeline_mode=pl.Buffered(1))


def _largest_divisor(n, candidates):
    for cand in candidates:
        if n % cand == 0:
            return cand
    raise ValueError(f"no tile in {candidates} divides {n}")


def _mod_kernel(c_ref, w_ref, b_ref, o_ref):
    c = c_ref[...]
    s = (c * jax.nn.sigmoid(c)).astype(BF16)
    o_ref[...] = _mm(s, w_ref[...].astype(BF16)) + b_ref[...]


def _mod_call(cc, w_mod, b_mod):
    rows, d = cc.shape
    n = w_mod.shape[1]
    bn = _largest_divisor(n, (1024, 512, 256, 128))
    return pl.pallas_call(
        _mod_kernel,
        out_shape=jax.ShapeDtypeStruct((rows, n), F32),
        grid=(n // bn,),
        in_specs=[pl.BlockSpec((rows, d), lambda j: (0, 0)),
                  pl.BlockSpec((d, bn), lambda j: (0, j)),
                  pl.BlockSpec((1, bn), lambda j: (0, j))],
        out_specs=pl.BlockSpec((rows, bn), lambda j: (0, j)),
        compiler_params=_params("arbitrary"),
        name="mod",
    )(cc, w_mod, b_mod.reshape(1, n))


_OFF_Q = 0
_OFF_K = _OFF_Q + DA_HEADS * 2 * DA_DH
_OFF_V = _OFF_K + DA_HEADS * 2 * DA_DH
_OFF_GQ = _OFF_V + DA_HEADS * DA_DV
_OFF_GK = _OFF_GQ + GLA_HEADS * GLA_DK
_OFF_GV = _OFF_GK + GLA_HEADS * GLA_DK
_OFF_LOW = _OFF_GV + GLA_HEADS * GLA_DV
_OFF_GOUT = _OFF_LOW + LOW_PAD
_OFF_MERGE = _OFF_GOUT + GLA_HEADS * GLA_DV
_N_GATE = 2 * GLA_HEADS * GLA_DK


def _proj_kernel(x_ref, sh_ref, sc_ref, gain_ref, w_ref, wup_ref, bup_ref, cos_ref, sa_ref, sb_ref,
                 q_ref, k_ref, v_ref, gq_ref, gk_ref, gv_ref, gate_ref, gout_ref, mrg_ref):
    d = x_ref.shape[-1]
    h = (_rms(x_ref[...], gain_ref[...]) * (1.0 + sc_ref[...]) + sh_ref[...]).astype(BF16)
    cos, sa, sb = cos_ref[...], sa_ref[...], sb_ref[...]

    def roped(off, out_ref, scale):
        acc = _mm(h, w_ref[:, off:off + d])
        for s in range(d // LANES):
            a = acc[:, s * LANES:(s + 1) * LANES]
            r = (a * cos + pltpu.roll(a, ROPE_HALF, 1) * sa
                 + pltpu.roll(a, LANES - ROPE_HALF, 1) * sb)
            out_ref[:, s * LANES:(s + 1) * LANES] = (r * scale).astype(out_ref.dtype)

    roped(_OFF_Q, q_ref, DA_DH ** -0.5)
    roped(_OFF_K, k_ref, 1.0)
    v_ref[...] = _mm(h, w_ref[:, _OFF_V:_OFF_GQ]).astype(v_ref.dtype)
    gq_ref[...] = (_mm(h, w_ref[:, _OFF_GQ:_OFF_GK]) * (GLA_DK ** -0.5)).astype(gq_ref.dtype)
    gk_ref[...] = _mm(h, w_ref[:, _OFF_GK:_OFF_GV]).astype(gk_ref.dtype)
    gv_ref[...] = _mm(h, w_ref[:, _OFF_GV:_OFF_LOW]).astype(gv_ref.dtype)
    low = _mm(h, w_ref[:, _OFF_LOW:_OFF_GOUT]).astype(BF16)
    z = _mm(low, wup_ref[...]) + bup_ref[...]
    log_sig = jnp.minimum(z, 0.0) - jnp.log1p(jnp.exp(-jnp.abs(z)))
    gate_ref[...] = log_sig / GLA_GATE_NORM
    go = _mm(h, w_ref[:, _OFF_GOUT:_OFF_MERGE])
    gout_ref[...] = (go * jax.nn.sigmoid(go)).astype(gout_ref.dtype)
    mg = _mm(h, w_ref[:, _OFF_MERGE:_OFF_MERGE + 2 * d])
    mrg_ref[...] = jax.nn.sigmoid(mg).astype(mrg_ref.dtype)


def _proj_call(x, shift, scale, gain, w_packed, w_up, b_up, cos, sa, sb):
    b, t, d = x.shape
    tm = _largest_divisor(t, (256, 128))
    row = lambda width: pl.BlockSpec((None, tm, width), lambda bi, i: (bi, i, 0))
    vec = pl.BlockSpec((None, 1, d), lambda bi, i: (bi, 0, 0))
    tab = pl.BlockSpec((tm, LANES), lambda bi, i: (i, 0))
    widths = (d, d, d, GLA_HEADS * GLA_DK, GLA_HEADS * GLA_DK, GLA_HEADS * GLA_DV,
              _N_GATE, GLA_HEADS * GLA_DV, 2 * d)
    dtypes = (BF16, BF16, BF16, BF16, BF16, BF16, F32, BF16, BF16)
    return pl.pallas_call(
        _proj_kernel,
        out_shape=[jax.ShapeDtypeStruct((b, t, w), dt) for w, dt in zip(widths, dtypes)],
        grid=(b, t // tm),
        in_specs=[row(d), vec, vec, _resident((1, d)), _resident(w_packed.shape),
                  _resident(w_up.shape), _resident(b_up.shape), tab, tab, tab],
        out_specs=[row(w) for w in widths],
        compiler_params=_params("parallel", "parallel"),
        name="proj",
    )(x, shift, scale, gain, w_packed, w_up, b_up, cos, sa, sb)


def _rope_tables(t):
    pos = jnp.arange(t)
    inv = ROPE_THETA ** (-jnp.arange(0, ROPE_AXIS_DIM, 2, dtype=F32) / ROPE_AXIS_DIM)
    ang_r = (pos // GRID_W).astype(F32)[:, None] * inv
    ang_c = (pos % GRID_W).astype(F32)[:, None] * inv
    zero = jnp.zeros_like(ang_r)
    cos = jnp.concatenate([jnp.cos(ang_r)] * 2 + [jnp.cos(ang_c)] * 2, axis=-1)
    sa = jnp.concatenate([zero, jnp.sin(ang_r), zero, jnp.sin(ang_c)], axis=-1)
    sb = jnp.concatenate([-jnp.sin(ang_r), zero, -jnp.sin(ang_c), zero], axis=-1)
    rep = LANES // DA_DH
    return tuple(jnp.tile(a, (1, rep)) for a in (cos, sa, sb))


def _identity_tables(t):
    return jnp.ones((t, LANES), F32), jnp.zeros((t, LANES), F32), jnp.zeros((t, LANES), F32)


def _level_ref(b, s, row, reverse):
    c = b.shape[0]
    if 2 * s >= SUBLANES:
        pieces = []
        for blk in range(c // (2 * s)):
            r = blk * 2 * s + s - 1 + (1 if reverse else 0)
            pieces.append(jnp.broadcast_to(b[r:r + 1, :], (2 * s, b.shape[1])))
        return jnp.concatenate(pieces, axis=0)
    pos = row % (2 * s)
    mid = s if reverse else s - 1
    ref = b
    for p in range(2 * s):
        if p != mid:
            ref = jnp.where(pos == p, pltpu.roll(b, (p - mid) % c, 0), ref)
    return ref


def _gla_direction(q, k, v, g, st_ref, reverse):
    c, dk = g.shape
    row = lax.broadcasted_iota(jnp.int32, (c, dk), 0)
    b = g
    step = 1
    while step < c:
        if reverse:
            b = b + jnp.where(row < c - step, pltpu.roll(b, c - step, 0), 0.0)
        else:
            b = b + jnp.where(row >= step, pltpu.roll(b, step, 0), 0.0)
        step *= 2
    total = b[0:1, :] if reverse else b[c - 1:c, :]
    qf = q.astype(F32)
    kf = k.astype(F32)

    st = st_ref[...]
    out = _nt((qf * jnp.exp(b)).astype(BF16), st.astype(BF16))

    ri = lax.broadcasted_iota(jnp.int32, (c, c), 0)
    ci = lax.broadcasted_iota(jnp.int32, (c, c), 1)
    a = jnp.where(ri == ci, _nt(q, k), 0.0)
    s = 1
    while s < c:
        ref = _level_ref(b, s, row, reverse)
        upper = (row % (2 * s)) >= s
        q_side = jnp.logical_not(upper) if reverse else upper
        eq = jnp.exp(jnp.where(q_side, b - ref, 0.0))
        ek = jnp.exp(jnp.where(q_side, 0.0, ref - b))
        ql = jnp.where(q_side, qf * eq, 0.0).astype(BF16)
        kl = jnp.where(q_side, 0.0, kf * ek).astype(BF16)
        a = a + jnp.where((ri ^ ci) < 2 * s, _nt(ql, kl), 0.0)
        s *= 2
    out = out + _mm(a.astype(BF16), v)

    kd = (kf * jnp.exp(total - b)).astype(BF16)
    st_ref[...] = st * jnp.exp(total) + _tn(v, kd)
    return out


def _gla_kernel(qf_ref, kf_ref, vf_ref, gf_ref, qb_ref, kb_ref, vb_ref, gb_ref, sf0_ref, sb0_ref,
                of_ref, ob_ref, sf1_ref, sb1_ref, sf_ref, sb_ref):
    i = pl.program_id(2)

    @pl.when(i == 0)
    def _():
        sf_ref[...] = sf0_ref[...]
        sb_ref[...] = sb0_ref[...]

    of_ref[...] = _gla_direction(qf_ref[...], kf_ref[...], vf_ref[...], gf_ref[...], sf_ref, False)
    ob_ref[...] = _gla_direction(qb_ref[...], kb_ref[...], vb_ref[...], gb_ref[...], sb_ref, True)
    sf1_ref[...] = sf_ref[...]
    sb1_ref[...] = sb_ref[...]


def _gla_call(gq, gk, gv, gate, sf0, sb0):
    b, t, _ = gq.shape
    c = _largest_divisor(t, (GLA_CHUNK,))
    n = t // c
    fwd = lambda w, off=0: pl.BlockSpec((None, c, w), lambda bi, h, i: (bi, i, h + off))
    bwd = lambda w, off=0: pl.BlockSpec((None, c, w), lambda bi, h, i: (bi, n - 1 - i, h + off))
    state = pl.BlockSpec((None, None, GLA_DV, GLA_DK), lambda bi, h, i: (bi, h, 0, 0))
    o_shape = jax.ShapeDtypeStruct((b, t, GLA_HEADS * GLA_DV), F32)
    s_shape = jax.ShapeDtypeStruct((b, GLA_HEADS, GLA_DV, GLA_DK), F32)
    return pl.pallas_call(
        _gla_kernel,
        out_shape=[o_shape, o_shape, s_shape, s_shape],
        grid=(b, GLA_HEADS, n),
        in_specs=[fwd(GLA_DK), fwd(GLA_DK), fwd(GLA_DV), fwd(GLA_DK),
                  bwd(GLA_DK), bwd(GLA_DK), bwd(GLA_DV), bwd(GLA_DK, GLA_HEADS),
                  state, state],
        out_specs=[fwd(GLA_DV), bwd(GLA_DV), state, state],
        scratch_shapes=[pltpu.VMEM((GLA_DV, GLA_DK), F32), pltpu.VMEM((GLA_DV, GLA_DK), F32)],
        compiler_params=_params("parallel", "parallel", "arbitrary"),
        name="gla",
    )(gq, gk, gv, gate, gq, gk, gv, gate, sf0, sb0)


def _da_kernel(lq1_ref, lk1_ref, lq2_ref, lk2_ref, gain_ref, qt_ref, k_ref, vt_ref, y_ref,
               m_ref, l_ref, acc_ref, *, lam_init):
    tq = qt_ref.shape[-1]
    nk = k_ref.shape[0]
    qt = qt_ref[...]
    sub = lax.broadcasted_iota(jnp.int32, qt.shape, 0)
    zero = jnp.zeros_like(qt)
    qm = jnp.concatenate([jnp.where(sub < DA_DH, qt, zero), jnp.where(sub < DA_DH, zero, qt)], axis=1)

    m_ref[...] = jnp.full(m_ref.shape, NEG_BIG, F32)
    l_ref[...] = jnp.zeros(l_ref.shape, F32)
    acc_ref[...] = jnp.zeros(acc_ref.shape, F32)

    def body(j, carry):
        s = _mm(k_ref[j], qm)
        m_prev = m_ref[...]
        m_new = jnp.maximum(m_prev, jnp.max(s, axis=0, keepdims=True))
        alpha = jnp.exp(m_prev - m_new)
        p = jnp.exp(s - m_new)
        l_ref[...] = alpha * l_ref[...] + jnp.sum(p, axis=0, keepdims=True)
        acc_ref[...] = alpha * acc_ref[...] + _mm(vt_ref[j], p.astype(BF16))
        m_ref[...] = m_new
        return carry

    lax.fori_loop(0, nk, body, 0)

    lam = (jnp.exp(jnp.sum(lq1_ref[...] * lk1_ref[...], keepdims=True))
           - jnp.exp(jnp.sum(lq2_ref[...] * lk2_ref[...], keepdims=True)) + lam_init)
    o = acc_ref[...] / l_ref[...]
    ot = o[:, :tq] - lam * o[:, tq:]
    y = _rms(ot.T, gain_ref[...]) * (1.0 - lam_init)
    y_ref[...] = y.astype(y_ref.dtype)


def _da_call(lams, gain, qt, k5, vt5, lam_init):
    b, h, dq, t = qt.shape
    nk, tk = k5.shape[2], k5.shape[3]
    tq = _largest_divisor(t, (256, 128))
    small = pl.BlockSpec((1, DA_DH), lambda bi, hi, i: (0, 0))
    return pl.pallas_call(
        functools.partial(_da_kernel, lam_init=lam_init),
        out_shape=jax.ShapeDtypeStruct((b, t, h * DA_DV), BF16),
        grid=(b, h, t // tq),
        in_specs=[small, small, small, small,
                  pl.BlockSpec((1, DA_DV), lambda bi, hi, i: (0, 0)),
                  pl.BlockSpec((None, None, dq, tq), lambda bi, hi, i: (bi, hi, 0, i)),
                  pl.BlockSpec((None, None, nk, tk, dq), lambda bi, hi, i: (bi, hi, 0, 0, 0)),
                  pl.BlockSpec((None, None, nk, DA_DV, tk), lambda bi, hi, i: (bi, hi, 0, 0, 0))],
        out_specs=pl.BlockSpec((None, tq, DA_DV), lambda bi, hi, i: (bi, i, hi)),
        scratch_shapes=[pltpu.VMEM((1, 2 * tq), F32), pltpu.VMEM((1, 2 * tq), F32),
                        pltpu.VMEM((DA_DV, 2 * tq), F32)],
        compiler_params=_params("parallel", "parallel", "arbitrary"),
        name="da",
    )(*lams, gain, qt, k5, vt5)


def _merge_kernel(x_ref, gt_ref, yda_ref, of_ref, ob_ref, gout_ref, mrg_ref, ggain_ref, pgain_ref,
                  wda_ref, wgla_ref, wout_ref, o_ref):
    d = x_ref.shape[-1]
    o_gla = of_ref[...] + ob_ref[...]
    ggain = ggain_ref[...]
    heads = [_rms(o_gla[:, h * GLA_DV:(h + 1) * GLA_DV], ggain) for h in range(GLA_HEADS)]
    y_gla = (jnp.concatenate(heads, axis=-1) * gout_ref[...].astype(F32)).astype(BF16)
    gates = mrg_ref[...].astype(F32)
    y = (gates[:, :d] * _mm(yda_ref[...], wda_ref[...])
         + gates[:, d:] * _mm(y_gla, wgla_ref[...]))
    z = _mm(y.astype(BF16), wout_ref[...])
    o_ref[...] = x_ref[...] + gt_ref[...] * _rms(z, pgain_ref[...])


def _merge_call(x, gt1, y_da, o_f, o_b, gout, mrg, ggain, pgain, w_da, w_gla, w_out):
    b, t, d = x.shape
    tm = _largest_divisor(t, (256, 128))
    row = lambda width: pl.BlockSpec((None, tm, width), lambda bi, i: (bi, i, 0))
    vec = pl.BlockSpec((None, 1, d), lambda bi, i: (bi, 0, 0))
    return pl.pallas_call(
        _merge_kernel,
        out_shape=jax.ShapeDtypeStruct((b, t, d), F32),
        grid=(b, t // tm),
        in_specs=[row(d), vec, row(d), row(d), row(d), row(d), row(2 * d),
                  _resident(ggain.shape), _resident(pgain.shape),
                  _resident(w_da.shape), _resident(w_gla.shape), _resident(w_out.shape)],
        out_specs=row(d),
        compiler_params=_params("parallel", "parallel"),
        name="merge",
    )(x, gt1, y_da, o_f, o_b, gout, mrg, ggain, pgain, w_da, w_gla, w_out)


def _mlp_kernel(x_ref, sh_ref, sc_ref, gt_ref, pre_ref, post_ref, w1_ref, w2_ref, o_ref):
    x = x_ref[...]
    h = (_rms(x, pre_ref[...]) * (1.0 + sc_ref[...]) + sh_ref[...]).astype(BF16)
    u = jnp.maximum(_mm(h, w1_ref[...]), 0.0)
    z = _mm((u * u).astype(BF16), w2_ref[...])
    o_ref[...] = x + gt_ref[...] * _rms(z, post_ref[...])


def _mlp_call(x, sh2, sc2, gt2, pre, post, w1, w2):
    b, t, d = x.shape
    tm = _largest_divisor(t, (256, 128))
    row = pl.BlockSpec((None, tm, d), lambda bi, i: (bi, i, 0))
    vec = pl.BlockSpec((None, 1, d), lambda bi, i: (bi, 0, 0))
    return pl.pallas_call(
        _mlp_kernel,
        out_shape=jax.ShapeDtypeStruct((b, t, d), F32),
        grid=(b, t // tm),
        in_specs=[row, vec, vec, vec, _resident(pre.shape), _resident(post.shape),
                  _resident(w1.shape), _resident(w2.shape)],
        out_specs=row,
        compiler_params=_params("parallel", "parallel"),
        name="mlp",
    )(x, sh2, sc2, gt2, pre, post, w1, w2)


def _pack_w_in(w):
    low0 = _OFF_LOW
    low1 = low0 + 2 * GLA_GATE_RANK
    pad = jnp.zeros((w.shape[0], LOW_PAD - 2 * GLA_GATE_RANK), w.dtype)
    return jnp.concatenate([w[:, :low0], w[:, low0:low1], pad, w[:, low1:]], axis=1).astype(BF16)


def _pack_gate_up(w_gate_up, b_gate_up):
    hk = GLA_HEADS * GLA_DK
    w = jnp.zeros((LOW_PAD, 2 * hk), F32)
    for z in range(2):
        w = w.at[z * GLA_GATE_RANK:(z + 1) * GLA_GATE_RANK, z * hk:(z + 1) * hk].set(w_gate_up[z])
    return w.astype(BF16), b_gate_up.reshape(1, 2 * hk).astype(F32)


def kernel(x, c, ctx, c_ctx, w_mod, b_mod, pre_norm1, w_in, w_gate_up, b_gate_up, lambda_q1, lambda_k1,
           lambda_q2, lambda_k2, da_head_norm, gla_head_norm, w_branch_da, w_branch_gla, w_out, post_norm1,
           pre_norm2, w_ff1, w_ff2, post_norm2):
    b, t, d = x.shape
    n_ctx = ctx.shape[1]
    assert w_in.shape[0] == 1, "single-layer block only"
    assert d == DA_HEADS * DA_DV == GLA_HEADS * GLA_DV and t % GRID_W == 0
    layer = 0
    lam_init = 0.8 - 0.6 * math.exp(-0.3 * layer)

    rows = -(-(b + 1) // SUBLANES) * SUBLANES
    cc = jnp.zeros((rows, d), F32).at[:b].set(c).at[b].set(c_ctx)
    mod = _mod_call(cc, w_mod[layer], b_mod[layer])
    lat = [mod[:b, i * d:(i + 1) * d].reshape(b, 1, d) for i in range(N_MOD)]
    cmod = [jnp.broadcast_to(mod[b, i * d:(i + 1) * d].reshape(1, 1, d), (b, 1, d)) for i in range(2)]
    sh1, sc1, gt1, sh2, sc2, gt2 = lat

    w_packed = _pack_w_in(w_in[layer])
    w_up, b_up = _pack_gate_up(w_gate_up[layer], b_gate_up[layer])
    gain1 = pre_norm1[layer].reshape(1, d)

    q, k, v, gq, gk, gv, gate, gout, mrg = _proj_call(
        x, sh1, sc1, gain1, w_packed, w_up, b_up, *_rope_tables(t))
    _, ck, cv, _, cgk, cgv, cgate, _, _ = _proj_call(
        ctx, cmod[0], cmod[1], gain1, w_packed, w_up, b_up, *_identity_tables(n_ctx))

    zero_state = jnp.zeros((b, GLA_HEADS, GLA_DV, GLA_DK), F32)
    _, _, sf0, sb0 = _gla_call(jnp.zeros_like(cgk), cgk, cgv, cgate, zero_state, zero_state)
    o_f, o_b, _, _ = _gla_call(gq, gk, gv, gate, sf0, sb0)

    tk_all = t + n_ctx
    tk = _largest_divisor(tk_all, (768, 512, 384, 256, 128))
    nk = tk_all // tk
    qt = q.reshape(b, t, DA_HEADS, 2 * DA_DH).transpose(0, 2, 3, 1)
    k5 = (jnp.concatenate([k, ck], axis=1).reshape(b, nk, tk, DA_HEADS, 2 * DA_DH)
          .transpose(0, 3, 1, 2, 4))
    vt5 = (jnp.concatenate([v, cv], axis=1).reshape(b, nk, tk, DA_HEADS, DA_DV)
           .transpose(0, 3, 1, 4, 2))
    lams = [a[layer].reshape(1, DA_DH) for a in (lambda_q1, lambda_k1, lambda_q2, lambda_k2)]
    y_da = _da_call(lams, da_head_norm[layer].reshape(1, DA_DV), qt, k5, vt5, lam_init)

    x1 = _merge_call(x, gt1, y_da, o_f, o_b, gout, mrg,
                     gla_head_norm[layer].reshape(1, GLA_DV), post_norm1[layer].reshape(1, d),
                     w_branch_da[layer].astype(BF16), w_branch_gla[layer].astype(BF16),
                     w_out[layer].astype(BF16))
    return _mlp_call(x1, sh2, sc2, gt2, pre_norm2[layer].reshape(1, d), post_norm2[layer].reshape(1, d),
                     w_ff1[layer].astype(BF16), w_ff2[layer].astype(BF16))
```

```python
import functools
import math

import jax
import jax.numpy as jnp
from jax import lax
from jax.experimental import pallas as pl
from jax.experimental.pallas import tpu as pltpu

F32 = jnp.float32
BF16 = jnp.bfloat16

GRID_W = 64
DA_HEADS = 8
DA_DH = 64
DA_DV = 2 * DA_DH
GLA_HEADS = 4
GLA_DK = 128
GLA_DV = 256
GLA_GATE_RANK = 16
GLA_GATE_NORM = 16.0
ROPE_THETA = 10000.0
ROPE_AXIS_DIM = DA_DH // 2
ROPE_HALF = ROPE_AXIS_DIM // 2
N_MOD = 6
EPS = 1e-6
NEG_BIG = -1e30

LANES = 128
SUBLANES = 8
BF16_ROWS = 16
LOG2_E = math.log2(math.e)
VMEM_LIMIT = 56 * 1024 * 1024

GLA_CHUNK = 128
DA_KEY_BLOCKS = (768, 512, 384, 256, 128)
DA_QUERY_BLOCKS = (256, 128)
LOW_PAD = LANES


def _nt(a, b):
    return lax.dot_general(a, b, (((1,), (1,)), ((), ())), preferred_element_type=F32)


def _tn(a, b):
    return lax.dot_general(a, b, (((0,), (0,)), ((), ())), preferred_element_type=F32)


def _mm(a, b):
    return jnp.dot(a, b, preferred_element_type=F32)


def _rms(x, gain):
    return x * lax.rsqrt(jnp.mean(x * x, axis=-1, keepdims=True) + EPS) * gain


def _params(*sem):
    return pltpu.CompilerParams(dimension_semantics=sem, vmem_limit_bytes=VMEM_LIMIT)


def _resident(shape):
    nd = len(shape)
    return pl.BlockSpec(shape, lambda *_: (0,) * nd, pipeline_mode=pl.Buffered(1))


def _largest_divisor(n, candidates):
    for cand in candidates:
        if n % cand == 0:
            return cand
    raise ValueError(f"no tile in {candidates} divides {n}")


def _mod_kernel(c_ref, w_ref, b_ref, o_ref):
    c = c_ref[...]
    s = (c * jax.nn.sigmoid(c)).astype(BF16)
    o_ref[...] = _mm(s, w_ref[...].astype(BF16)) + b_ref[...]


def _mod_call(cc, w_mod, b_mod):
    rows, d = cc.shape
    n = w_mod.shape[1]
    bn = _largest_divisor(n, (1024, 512, 256, 128))
    return pl.pallas_call(
        _mod_kernel,
        out_shape=jax.ShapeDtypeStruct((rows, n), F32),
        grid=(n // bn,),
        in_specs=[pl.BlockSpec((rows, d), lambda j: (0, 0)),
                  pl.BlockSpec((d, bn), lambda j: (0, j)),
                  pl.BlockSpec((1, bn), lambda j: (0, j))],
        out_specs=pl.BlockSpec((rows, bn), lambda j: (0, j)),
        compiler_params=_params("arbitrary"),
        name="mod",
    )(cc, w_mod, b_mod.reshape(1, n))


_OFF_Q = 0
_OFF_K = _OFF_Q + DA_HEADS * 2 * DA_DH
_OFF_V = _OFF_K + DA_HEADS * 2 * DA_DH
_OFF_GQ = _OFF_V + DA_HEADS * DA_DV
_OFF_GK = _OFF_GQ + GLA_HEADS * GLA_DK
_OFF_GV = _OFF_GK + GLA_HEADS * GLA_DK
_OFF_LOW = _OFF_GV + GLA_HEADS * GLA_DV
_OFF_GOUT = _OFF_LOW + LOW_PAD
_OFF_MERGE = _OFF_GOUT + GLA_HEADS * GLA_DV
_N_GATE = 2 * GLA_HEADS * GLA_DK


def _proj_kernel(x_ref, sh_ref, sc_ref, gain_ref, w_ref, wup_ref, bup_ref, cos_ref, sa_ref, sb_ref,
                 q_ref, k_ref, v_ref, gq_ref, gk_ref, gv_ref, gate_ref, gout_ref, mrg_ref):
    d = x_ref.shape[-1]
    h = (_rms(x_ref[...], gain_ref[...]) * (1.0 + sc_ref[...]) + sh_ref[...]).astype(BF16)
    cos, sa, sb = cos_ref[...], sa_ref[...], sb_ref[...]

    def roped(off, out_ref, scale):
        acc = _mm(h, w_ref[:, off:off + d])
        for s in range(d // LANES):
            a = acc[:, s * LANES:(s + 1) * LANES]
            r = (a * cos + pltpu.roll(a, ROPE_HALF, 1) * sa
                 + pltpu.roll(a, LANES - ROPE_HALF, 1) * sb)
            out_ref[:, s * LANES:(s + 1) * LANES] = (r * scale).astype(out_ref.dtype)

    roped(_OFF_Q, q_ref, DA_DH ** -0.5 * LOG2_E)
    roped(_OFF_K, k_ref, 1.0)
    v_ref[...] = _mm(h, w_ref[:, _OFF_V:_OFF_GQ]).astype(v_ref.dtype)
    gq_ref[...] = (_mm(h, w_ref[:, _OFF_GQ:_OFF_GK]) * (GLA_DK ** -0.5)).astype(gq_ref.dtype)
    gk_ref[...] = _mm(h, w_ref[:, _OFF_GK:_OFF_GV]).astype(gk_ref.dtype)
    gv_ref[...] = _mm(h, w_ref[:, _OFF_GV:_OFF_LOW]).astype(gv_ref.dtype)
    low = _mm(h, w_ref[:, _OFF_LOW:_OFF_GOUT]).astype(BF16)
    z = _mm(low, wup_ref[...]) + bup_ref[...]
    log_sig = jnp.minimum(z, 0.0) - jnp.log1p(jnp.exp(-jnp.abs(z)))
    gate_ref[...] = log_sig / GLA_GATE_NORM
    go = _mm(h, w_ref[:, _OFF_GOUT:_OFF_MERGE])
    gout_ref[...] = (go * jax.nn.sigmoid(go)).astype(gout_ref.dtype)
    mg = _mm(h, w_ref[:, _OFF_MERGE:_OFF_MERGE + 2 * d])
    mrg_ref[...] = jax.nn.sigmoid(mg).astype(mrg_ref.dtype)


def _proj_call(x, shift, scale, gain, w_packed, w_up, b_up, cos, sa, sb):
    b, t, d = x.shape
    tm = _largest_divisor(t, (256, 128))
    row = lambda width: pl.BlockSpec((None, tm, width), lambda bi, i: (bi, i, 0))
    vec = pl.BlockSpec((None, 1, d), lambda bi, i: (bi, 0, 0))
    tab = pl.BlockSpec((tm, LANES), lambda bi, i: (i, 0))
    widths = (d, d, d, GLA_HEADS * GLA_DK, GLA_HEADS * GLA_DK, GLA_HEADS * GLA_DV,
              _N_GATE, GLA_HEADS * GLA_DV, 2 * d)
    dtypes = (BF16, BF16, BF16, BF16, BF16, BF16, F32, BF16, BF16)
    return pl.pallas_call(
        _proj_kernel,
        out_shape=[jax.ShapeDtypeStruct((b, t, w), dt) for w, dt in zip(widths, dtypes)],
        grid=(b, t // tm),
        in_specs=[row(d), vec, vec, _resident((1, d)), _resident(w_packed.shape),
                  _resident(w_up.shape), _resident(b_up.shape), tab, tab, tab],
        out_specs=[row(w) for w in widths],
        compiler_params=_params("parallel", "parallel"),
        name="proj",
    )(x, shift, scale, gain, w_packed, w_up, b_up, cos, sa, sb)


def _rope_tables(t):
    pos = jnp.arange(t)
    inv = ROPE_THETA ** (-jnp.arange(0, ROPE_AXIS_DIM, 2, dtype=F32) / ROPE_AXIS_DIM)
    ang_r = (pos // GRID_W).astype(F32)[:, None] * inv
    ang_c = (pos % GRID_W).astype(F32)[:, None] * inv
    zero = jnp.zeros_like(ang_r)
    cos = jnp.concatenate([jnp.cos(ang_r)] * 2 + [jnp.cos(ang_c)] * 2, axis=-1)
    sa = jnp.concatenate([zero, jnp.sin(ang_r), zero, jnp.sin(ang_c)], axis=-1)
    sb = jnp.concatenate([-jnp.sin(ang_r), zero, -jnp.sin(ang_c), zero], axis=-1)
    rep = LANES // DA_DH
    return tuple(jnp.tile(a, (1, rep)) for a in (cos, sa, sb))


def _identity_tables(t):
    return jnp.ones((t, LANES), F32), jnp.zeros((t, LANES), F32), jnp.zeros((t, LANES), F32)


def _level_ref(b, s, row, reverse):
    c = b.shape[0]
    if 2 * s >= SUBLANES:
        pieces = []
        for blk in range(c // (2 * s)):
            r = blk * 2 * s + s - 1 + (1 if reverse else 0)
            pieces.append(jnp.broadcast_to(b[r:r + 1, :], (2 * s, b.shape[1])))
        return jnp.concatenate(pieces, axis=0)
    pos = row % (2 * s)
    mid = s if reverse else s - 1
    ref = b
    for p in range(2 * s):
        if p != mid:
            ref = jnp.where(pos == p, pltpu.roll(b, (p - mid) % c, 0), ref)
    return ref


def _gla_direction(q, k, v, g, st_ref, reverse):
    c, dk = g.shape
    row = lax.broadcasted_iota(jnp.int32, (c, dk), 0)
    b = g
    step = 1
    while step < c:
        if reverse:
            b = b + jnp.where(row < c - step, pltpu.roll(b, c - step, 0), 0.0)
        else:
            b = b + jnp.where(row >= step, pltpu.roll(b, step, 0), 0.0)
        step *= 2
    total = b[0:1, :] if reverse else b[c - 1:c, :]
    qf = q.astype(F32)
    kf = k.astype(F32)

    st = st_ref[...]
    out = _nt((qf * jnp.exp(b)).astype(BF16), st.astype(BF16))

    ri = lax.broadcasted_iota(jnp.int32, (c, c), 0)
    ci = lax.broadcasted_iota(jnp.int32, (c, c), 1)
    a = jnp.where(ri == ci, _nt(q, k), 0.0)
    s = 1
    while s < c:
        ref = _level_ref(b, s, row, reverse)
        upper = (row % (2 * s)) >= s
        q_side = jnp.logical_not(upper) if reverse else upper
        eq = jnp.exp(jnp.where(q_side, b - ref, 0.0))
        ek = jnp.exp(jnp.where(q_side, 0.0, ref - b))
        ql = jnp.where(q_side, qf * eq, 0.0).astype(BF16)
        kl = jnp.where(q_side, 0.0, kf * ek).astype(BF16)
        a = a + jnp.where((ri ^ ci) < 2 * s, _nt(ql, kl), 0.0)
        s *= 2
    out = out + _mm(a.astype(BF16), v)

    kd = (kf * jnp.exp(total - b)).astype(BF16)
    st_ref[...] = st * jnp.exp(total) + _tn(v, kd)
    return out


def _gla_kernel(qf_ref, kf_ref, vf_ref, gf_ref, qb_ref, kb_ref, vb_ref, gb_ref, sf0_ref, sb0_ref,
                of_ref, ob_ref, sf1_ref, sb1_ref, sf_ref, sb_ref):
    i = pl.program_id(2)

    @pl.when(i == 0)
    def _():
        sf_ref[...] = sf0_ref[...]
        sb_ref[...] = sb0_ref[...]

    of_ref[...] = _gla_direction(qf_ref[...], kf_ref[...], vf_ref[...], gf_ref[...], sf_ref, False)
    ob_ref[...] = _gla_direction(qb_ref[...], kb_ref[...], vb_ref[...], gb_ref[...], sb_ref, True)
    sf1_ref[...] = sf_ref[...]
    sb1_ref[...] = sb_ref[...]


def _gla_call(gq, gk, gv, gate, sf0, sb0):
    b, t, _ = gq.shape
    c = _largest_divisor(t, (GLA_CHUNK,))
    n = t // c
    fwd = lambda w, off=0: pl.BlockSpec((None, c, w), lambda bi, h, i: (bi, i, h + off))
    bwd = lambda w, off=0: pl.BlockSpec((None, c, w), lambda bi, h, i: (bi, n - 1 - i, h + off))
    state = pl.BlockSpec((None, None, GLA_DV, GLA_DK), lambda bi, h, i: (bi, h, 0, 0))
    o_shape = jax.ShapeDtypeStruct((b, t, GLA_HEADS * GLA_DV), F32)
    s_shape = jax.ShapeDtypeStruct((b, GLA_HEADS, GLA_DV, GLA_DK), F32)
    return pl.pallas_call(
        _gla_kernel,
        out_shape=[o_shape, o_shape, s_shape, s_shape],
        grid=(b, GLA_HEADS, n),
        in_specs=[fwd(GLA_DK), fwd(GLA_DK), fwd(GLA_DV), fwd(GLA_DK),
                  bwd(GLA_DK), bwd(GLA_DK), bwd(GLA_DV), bwd(GLA_DK, GLA_HEADS),
                  state, state],
        out_specs=[fwd(GLA_DV), bwd(GLA_DV), state, state],
        scratch_shapes=[pltpu.VMEM((GLA_DV, GLA_DK), F32), pltpu.VMEM((GLA_DV, GLA_DK), F32)],
        compiler_params=_params("parallel", "parallel", "arbitrary"),
        name="gla",
    )(gq, gk, gv, gate, gq, gk, gv, gate, sf0, sb0)


def _da_kernel(lq1_ref, lk1_ref, lq2_ref, lk2_ref, gain_ref, qt_ref, k_ref, vt_ref, y_ref,
               qm_ref, m_ref, acc_ref, s0_ref, s1_ref, p0_ref, p1_ref, a0_ref, a1_ref, x0_ref, x1_ref,
               *, lam_init):
    tq = qt_ref.shape[-1]
    nk = k_ref.shape[0]
    sbuf, pbuf, abuf, xbuf = (s0_ref, s1_ref), (p0_ref, p1_ref), (a0_ref, a1_ref), (x0_ref, x1_ref)
    qt = qt_ref[...]
    sub = lax.broadcasted_iota(jnp.int32, qt.shape, 0)
    zero = jnp.zeros_like(qt)
    qm_ref[:, :tq] = jnp.where(sub < DA_DH, qt, zero)
    qm_ref[:, tq:] = jnp.where(sub < DA_DH, zero, qt)
    m_ref[...] = jnp.full(m_ref.shape, NEG_BIG, F32)
    acc_ref[...] = jnp.zeros(acc_ref.shape, F32)

    def scores(j, par):
        s = _mm(k_ref[j], qm_ref[...])
        sbuf[par][...] = s
        xbuf[par][...] = jnp.broadcast_to(jnp.max(s, axis=0, keepdims=True), xbuf[par].shape)

    def softmax(par):
        m_prev = m_ref[...]
        m_new = jnp.maximum(m_prev, xbuf[par][...])
        abuf[par][...] = jnp.exp2(m_prev - m_new)
        s = sbuf[par][...]
        groups = s.shape[0] // SUBLANES
        p = jnp.exp2(s.reshape(groups, SUBLANES, s.shape[1]) - m_new[None])
        pbuf[par][...] = p.reshape(s.shape).astype(BF16)
        m_ref[...] = m_new

    def accumulate(j, par):
        acc_ref[...] = abuf[par][0:1, :] * acc_ref[...] + _mm(vt_ref[j], pbuf[par][...])

    def step(j, par, first=False, last=False):
        if not last:
            scores(j + 1, 1 - par)
        softmax(par)
        if not first:
            accumulate(j - 1, 1 - par)

    scores(0, 0)
    if nk == 1:
        softmax(0)
    else:
        step(0, 0, first=True)
        middle = nk - 2

        def pair(i, carry):
            j = 2 * i + 1
            step(j, 1)
            step(j + 1, 0)
            return carry

        for jj in range(1, nk - 1):
            step(jj, jj % 2)
        step(nk - 1, (nk - 1) % 2, last=True)
    accumulate(nk - 1, (nk - 1) % 2)

    lam = (jnp.exp(jnp.sum(lq1_ref[...] * lk1_ref[...], keepdims=True))
           - jnp.exp(jnp.sum(lq2_ref[...] * lk2_ref[...], keepdims=True)) + lam_init)
    acc = acc_ref[...]
    o = acc[:DA_DV] / acc[DA_DV:DA_DV + 1]
    ot = o[:, :tq] - lam * o[:, tq:]
    y = _rms(ot.T, gain_ref[...]) * (1.0 - lam_init)
    y_ref[...] = y.astype(y_ref.dtype)


def _da_call(lams, gain, qt, k5, vt5, lam_init):
    b, h, dq, t = qt.shape
    nk, tk = k5.shape[2], k5.shape[3]
    dva = vt5.shape[3]
    tq = _largest_divisor(t, DA_QUERY_BLOCKS)
    small = pl.BlockSpec((1, DA_DH), lambda bi, hi, i: (0, 0))
    return pl.pallas_call(
        functools.partial(_da_kernel, lam_init=lam_init),
        out_shape=jax.ShapeDtypeStruct((b, t, h * DA_DV), BF16),
        grid=(b, h, t // tq),
        in_specs=[small, small, small, small,
                  pl.BlockSpec((1, DA_DV), lambda bi, hi, i: (0, 0)),
                  pl.BlockSpec((None, None, dq, tq), lambda bi, hi, i: (bi, hi, 0, i)),
                  pl.BlockSpec((None, None, nk, tk, dq), lambda bi, hi, i: (bi, hi, 0, 0, 0)),
                  pl.BlockSpec((None, None, nk, dva, tk), lambda bi, hi, i: (bi, hi, 0, 0, 0))],
        out_specs=pl.BlockSpec((None, tq, DA_DV), lambda bi, hi, i: (bi, i, hi)),
        scratch_shapes=[pltpu.VMEM((dq, 2 * tq), BF16),
                        pltpu.VMEM((SUBLANES, 2 * tq), F32),
                        pltpu.VMEM((dva, 2 * tq), F32),
                        pltpu.VMEM((tk, 2 * tq), F32), pltpu.VMEM((tk, 2 * tq), F32),
                        pltpu.VMEM((tk, 2 * tq), BF16), pltpu.VMEM((tk, 2 * tq), BF16),
                        pltpu.VMEM((SUBLANES, 2 * tq), F32), pltpu.VMEM((SUBLANES, 2 * tq), F32),
                        pltpu.VMEM((SUBLANES, 2 * tq), F32), pltpu.VMEM((SUBLANES, 2 * tq), F32)],
        compiler_params=_params("parallel", "parallel", "arbitrary"),
        name="da",
    )(*lams, gain, qt, k5, vt5)


def _merge_kernel(x_ref, gt_ref, yda_ref, of_ref, ob_ref, gout_ref, mrg_ref, ggain_ref, pgain_ref,
                  wda_ref, wgla_ref, wout_ref, o_ref):
    d = x_ref.shape[-1]
    o_gla = of_ref[...] + ob_ref[...]
    ggain = ggain_ref[...]
    heads = [_rms(o_gla[:, h * GLA_DV:(h + 1) * GLA_DV], ggain) for h in range(GLA_HEADS)]
    y_gla = (jnp.concatenate(heads, axis=-1) * gout_ref[...].astype(F32)).astype(BF16)
    gates = mrg_ref[...].astype(F32)
    y = (gates[:, :d] * _mm(yda_ref[...], wda_ref[...])
         + gates[:, d:] * _mm(y_gla, wgla_ref[...]))
    z = _mm(y.astype(BF16), wout_ref[...])
    o_ref[...] = x_ref[...] + gt_ref[...] * _rms(z, pgain_ref[...])


def _merge_call(x, gt1, y_da, o_f, o_b, gout, mrg, ggain, pgain, w_da, w_gla, w_out):
    b, t, d = x.shape
    tm = _largest_divisor(t, (256, 128))
    row = lambda width: pl.BlockSpec((None, tm, width), lambda bi, i: (bi, i, 0))
    vec = pl.BlockSpec((None, 1, d), lambda bi, i: (bi, 0, 0))
    return pl.pallas_call(
        _merge_kernel,
        out_shape=jax.ShapeDtypeStruct((b, t, d), F32),
        grid=(b, t // tm),
        in_specs=[row(d), vec, row(d), row(d), row(d), row(d), row(2 * d),
                  _resident(ggain.shape), _resident(pgain.shape),
                  _resident(w_da.shape), _resident(w_gla.shape), _resident(w_out.shape)],
        out_specs=row(d),
        compiler_params=_params("parallel", "parallel"),
        name="merge",
    )(x, gt1, y_da, o_f, o_b, gout, mrg, ggain, pgain, w_da, w_gla, w_out)


def _mlp_kernel(x_ref, sh_ref, sc_ref, gt_ref, pre_ref, post_ref, w1_ref, w2_ref, o_ref):
    x = x_ref[...]
    h = (_rms(x, pre_ref[...]) * (1.0 + sc_ref[...]) + sh_ref[...]).astype(BF16)
    u = jnp.maximum(_mm(h, w1_ref[...]), 0.0)
    z = _mm((u * u).astype(BF16), w2_ref[...])
    o_ref[...] = x + gt_ref[...] * _rms(z, post_ref[...])


def _mlp_call(x, sh2, sc2, gt2, pre, post, w1, w2):
    b, t, d = x.shape
    tm = _largest_divisor(t, (256, 128))
    row = pl.BlockSpec((None, tm, d), lambda bi, i: (bi, i, 0))
    vec = pl.BlockSpec((None, 1, d), lambda bi, i: (bi, 0, 0))
    return pl.pallas_call(
        _mlp_kernel,
        out_shape=jax.ShapeDtypeStruct((b, t, d), F32),
        grid=(b, t // tm),
        in_specs=[row, vec, vec, vec, _resident(pre.shape), _resident(post.shape),
                  _resident(w1.shape), _resident(w2.shape)],
        out_specs=row,
        compiler_params=_params("parallel", "parallel"),
        name="mlp",
    )(x, sh2, sc2, gt2, pre, post, w1, w2)


def _pack_w_in(w):
    low0 = _OFF_LOW
    low1 = low0 + 2 * GLA_GATE_RANK
    pad = jnp.zeros((w.shape[0], LOW_PAD - 2 * GLA_GATE_RANK), w.dtype)
    return jnp.concatenate([w[:, :low0], w[:, low0:low1], pad, w[:, low1:]], axis=1).astype(BF16)


def _pack_gate_up(w_gate_up, b_gate_up):
    hk = GLA_HEADS * GLA_DK
    w = jnp.zeros((LOW_PAD, 2 * hk), F32)
    for z in range(2):
        w = w.at[z * GLA_GATE_RANK:(z + 1) * GLA_GATE_RANK, z * hk:(z + 1) * hk].set(w_gate_up[z])
    return w.astype(BF16), b_gate_up.reshape(1, 2 * hk).astype(F32)


def kernel(x, c, ctx, c_ctx, w_mod, b_mod, pre_norm1, w_in, w_gate_up, b_gate_up, lambda_q1, lambda_k1,
           lambda_q2, lambda_k2, da_head_norm, gla_head_norm, w_branch_da, w_branch_gla, w_out, post_norm1,
           pre_norm2, w_ff1, w_ff2, post_norm2):
    b, t, d = x.shape
    n_ctx = ctx.shape[1]
    assert w_in.shape[0] == 1, "single-layer block only"
    assert d == DA_HEADS * DA_DV == GLA_HEADS * GLA_DV and t % GRID_W == 0
    layer = 0
    lam_init = 0.8 - 0.6 * math.exp(-0.3 * layer)

    rows = -(-(b + 1) // SUBLANES) * SUBLANES
    cc = jnp.zeros((rows, d), F32).at[:b].set(c).at[b].set(c_ctx)
    mod = _mod_call(cc, w_mod[layer], b_mod[layer])
    lat = [mod[:b, i * d:(i + 1) * d].reshape(b, 1, d) for i in range(N_MOD)]
    cmod = [jnp.broadcast_to(mod[b, i * d:(i + 1) * d].reshape(1, 1, d), (b, 1, d)) for i in range(2)]
    sh1, sc1, gt1, sh2, sc2, gt2 = lat

    w_packed = _pack_w_in(w_in[layer])
    w_up, b_up = _pack_gate_up(w_gate_up[layer], b_gate_up[layer])
    gain1 = pre_norm1[layer].reshape(1, d)

    q, k, v, gq, gk, gv, gate, gout, mrg = _proj_call(
        x, sh1, sc1, gain1, w_packed, w_up, b_up, *_rope_tables(t))
    _, ck, cv, _, cgk, cgv, cgate, _, _ = _proj_call(
        ctx, cmod[0], cmod[1], gain1, w_packed, w_up, b_up, *_identity_tables(n_ctx))

    zero_state = jnp.zeros((b, GLA_HEADS, GLA_DV, GLA_DK), F32)
    _, _, sf0, sb0 = _gla_call(jnp.zeros_like(cgk), cgk, cgv, cgate, zero_state, zero_state)
    o_f, o_b, _, _ = _gla_call(gq, gk, gv, gate, sf0, sb0)

    tk_all = t + n_ctx
    tk = _largest_divisor(tk_all, DA_KEY_BLOCKS)
    nk = tk_all // tk
    qt = q.reshape(b, t, DA_HEADS, 2 * DA_DH).transpose(0, 2, 3, 1)
    k5 = (jnp.concatenate([k, ck], axis=1).reshape(b, nk, tk, DA_HEADS, 2 * DA_DH)
          .transpose(0, 3, 1, 2, 4))
    vt5 = (jnp.concatenate([v, cv], axis=1).reshape(b, nk, tk, DA_HEADS, DA_DV)
           .transpose(0, 3, 1, 4, 2))
    vt5 = jnp.concatenate([vt5, jnp.ones((b, DA_HEADS, nk, BF16_ROWS, tk), BF16)], axis=3)
    lams = [a[layer].reshape(1, DA_DH) for a in (lambda_q1, lambda_k1, lambda_q2, lambda_k2)]
    y_da = _da_call(lams, da_head_norm[layer].reshape(1, DA_DV), qt, k5, vt5, lam_init)

    x1 = _merge_call(x, gt1, y_da, o_f, o_b, gout, mrg,
                     gla_head_norm[layer].reshape(1, GLA_DV), post_norm1[layer].reshape(1, d),
                     w_branch_da[layer].astype(BF16), w_branch_gla[layer].astype(BF16),
                     w_out[layer].astype(BF16))
    return _mlp_call(x1, sh2, sc2, gt2, pre_norm2[layer].reshape(1, d), post_norm2[layer].reshape(1, d),
                     w_ff1[layer].astype(BF16), w_ff2[layer].astype(BF16))
```

```python
import functools
import math

import numpy as np
import jax
import jax.numpy as jnp
from jax import lax
from jax.experimental import pallas as pl
from jax.experimental.pallas import tpu as pltpu

F32 = jnp.float32
BF16 = jnp.bfloat16

GRID_W = 64
DA_HEADS = 8
DA_DH = 64
DA_DV = 2 * DA_DH
GLA_HEADS = 4
GLA_DK = 128
GLA_DV = 256
GLA_GATE_RANK = 16
GLA_GATE_NORM = 16.0
ROPE_THETA = 10000.0
ROPE_AXIS_DIM = DA_DH // 2
ROPE_HALF = ROPE_AXIS_DIM // 2
N_MOD = 6
EPS = 1e-6
NEG_BIG = -1e30

LANES = 128
SUBLANES = 8
BF16_ROWS = 16
LOG2_E = math.log2(math.e)
VMEM_LIMIT = 56 * 1024 * 1024

GLA_CHUNK = 128
GLA_HEAD_BLOCK = 4
DA_KEY_BLOCKS = (768, 512, 384, 256, 128)
DA_QUERY_BLOCKS = (512, 256, 128)
LOW_PAD = LANES


def _nt(a, b):
    return lax.dot_general(a, b, (((1,), (1,)), ((), ())), preferred_element_type=F32)


def _tn(a, b):
    return lax.dot_general(a, b, (((0,), (0,)), ((), ())), preferred_element_type=F32)


def _mm(a, b):
    return jnp.dot(a, b, preferred_element_type=F32)


def _rms(x, gain):
    return x * lax.rsqrt(jnp.mean(x * x, axis=-1, keepdims=True) + EPS) * gain


def _params(*sem):
    return pltpu.CompilerParams(dimension_semantics=sem, vmem_limit_bytes=VMEM_LIMIT)


def _resident(shape):
    nd = len(shape)
    return pl.BlockSpec(shape, lambda *_: (0,) * nd, pipeline_mode=pl.Buffered(1))


def _largest_divisor(n, candidates):
    for cand in candidates:
        if n % cand == 0:
            return cand
    raise ValueError(f"no tile in {candidates} divides {n}")


def _mod_kernel(c_ref, w_ref, b_ref, o_ref):
    c = c_ref[...]
    s = (c * jax.nn.sigmoid(c)).astype(BF16)
    o_ref[...] = _mm(s, w_ref[...].astype(BF16)) + b_ref[...]


def _mod_call(cc, w_mod, b_mod):
    rows, d = cc.shape
    n = w_mod.shape[1]
    bn = _largest_divisor(n, (1024, 512, 256, 128))
    return pl.pallas_call(
        _mod_kernel,
        out_shape=jax.ShapeDtypeStruct((rows, n), F32),
        grid=(n // bn,),
        in_specs=[pl.BlockSpec((rows, d), lambda j: (0, 0)),
                  pl.BlockSpec((d, bn), lambda j: (0, j)),
                  pl.BlockSpec((1, bn), lambda j: (0, j))],
        out_specs=pl.BlockSpec((rows, bn), lambda j: (0, j)),
        compiler_params=_params("arbitrary"),
        name="mod",
    )(cc, w_mod, b_mod.reshape(1, n))


_OFF_Q = 0
_OFF_K = _OFF_Q + DA_HEADS * 2 * DA_DH
_OFF_V = _OFF_K + DA_HEADS * 2 * DA_DH
_OFF_GQ = _OFF_V + DA_HEADS * DA_DV
_OFF_GK = _OFF_GQ + GLA_HEADS * GLA_DK
_OFF_GV = _OFF_GK + GLA_HEADS * GLA_DK
_OFF_LOW = _OFF_GV + GLA_HEADS * GLA_DV
_OFF_GOUT = _OFF_LOW + LOW_PAD
_OFF_MERGE = _OFF_GOUT + GLA_HEADS * GLA_DV
_N_GATE = 2 * GLA_HEADS * GLA_DK


def _proj_kernel(x_ref, sh_ref, sc_ref, gain_ref, w_ref, wup_ref, bup_ref, cos_ref, sa_ref, sb_ref,
                 q_ref, k_ref, v_ref, gq_ref, gk_ref, gv_ref, gate_ref, gout_ref, mrg_ref):
    d = x_ref.shape[-1]
    h = (_rms(x_ref[...], gain_ref[...]) * (1.0 + sc_ref[...]) + sh_ref[...]).astype(BF16)
    cos, sa, sb = cos_ref[...], sa_ref[...], sb_ref[...]

    def roped(off, out_ref, scale):
        acc = _mm(h, w_ref[:, off:off + d])
        for s in range(d // LANES):
            a = acc[:, s * LANES:(s + 1) * LANES]
            r = (a * cos + pltpu.roll(a, ROPE_HALF, 1) * sa
                 + pltpu.roll(a, LANES - ROPE_HALF, 1) * sb)
            out_ref[:, s * LANES:(s + 1) * LANES] = (r * scale).astype(out_ref.dtype)

    roped(_OFF_Q, q_ref, DA_DH ** -0.5 * LOG2_E)
    roped(_OFF_K, k_ref, 1.0)
    v_ref[...] = _mm(h, w_ref[:, _OFF_V:_OFF_GQ]).astype(v_ref.dtype)
    gq_ref[...] = (_mm(h, w_ref[:, _OFF_GQ:_OFF_GK]) * (GLA_DK ** -0.5)).astype(gq_ref.dtype)
    gk_ref[...] = _mm(h, w_ref[:, _OFF_GK:_OFF_GV]).astype(gk_ref.dtype)
    gv_ref[...] = _mm(h, w_ref[:, _OFF_GV:_OFF_LOW]).astype(gv_ref.dtype)
    low = _mm(h, w_ref[:, _OFF_LOW:_OFF_GOUT]).astype(BF16)
    z = _mm(low, wup_ref[...]) + bup_ref[...]
    log_sig = jnp.minimum(z, 0.0) - jnp.log1p(jnp.exp(-jnp.abs(z)))
    gate_ref[...] = log_sig / GLA_GATE_NORM
    go = _mm(h, w_ref[:, _OFF_GOUT:_OFF_MERGE])
    gout_ref[...] = (go * jax.nn.sigmoid(go)).astype(gout_ref.dtype)
    mg = _mm(h, w_ref[:, _OFF_MERGE:_OFF_MERGE + 2 * d])
    mrg_ref[...] = jax.nn.sigmoid(mg).astype(mrg_ref.dtype)


def _proj_call(x, shift, scale, gain, w_packed, w_up, b_up, cos, sa, sb):
    b, t, d = x.shape
    tm = _largest_divisor(t, (256, 128))
    row = lambda width: pl.BlockSpec((None, tm, width), lambda bi, i: (bi, i, 0))
    vec = pl.BlockSpec((None, 1, d), lambda bi, i: (bi, 0, 0))
    tab = pl.BlockSpec((tm, LANES), lambda bi, i: (i, 0))
    widths = (d, d, d, GLA_HEADS * GLA_DK, GLA_HEADS * GLA_DK, GLA_HEADS * GLA_DV,
              _N_GATE, GLA_HEADS * GLA_DV, 2 * d)
    dtypes = (BF16, BF16, BF16, BF16, BF16, BF16, F32, BF16, BF16)
    return pl.pallas_call(
        _proj_kernel,
        out_shape=[jax.ShapeDtypeStruct((b, t, w), dt) for w, dt in zip(widths, dtypes)],
        grid=(b, t // tm),
        in_specs=[row(d), vec, vec, _resident((1, d)), _resident(w_packed.shape),
                  _resident(w_up.shape), _resident(b_up.shape), tab, tab, tab],
        out_specs=[row(w) for w in widths],
        compiler_params=_params("parallel", "parallel"),
        name="proj",
    )(x, shift, scale, gain, w_packed, w_up, b_up, cos, sa, sb)


def _rope_tables(t):
    pos = jnp.arange(t)
    inv = ROPE_THETA ** (-jnp.arange(0, ROPE_AXIS_DIM, 2, dtype=F32) / ROPE_AXIS_DIM)
    ang_r = (pos // GRID_W).astype(F32)[:, None] * inv
    ang_c = (pos % GRID_W).astype(F32)[:, None] * inv
    zero = jnp.zeros_like(ang_r)
    cos = jnp.concatenate([jnp.cos(ang_r)] * 2 + [jnp.cos(ang_c)] * 2, axis=-1)
    sa = jnp.concatenate([zero, jnp.sin(ang_r), zero, jnp.sin(ang_c)], axis=-1)
    sb = jnp.concatenate([-jnp.sin(ang_r), zero, -jnp.sin(ang_c), zero], axis=-1)
    rep = LANES // DA_DH
    return tuple(jnp.tile(a, (1, rep)) for a in (cos, sa, sb))


def _identity_tables(t):
    return jnp.ones((t, LANES), F32), jnp.zeros((t, LANES), F32), jnp.zeros((t, LANES), F32)


def _level_ref(b, s, row, reverse):
    c = b.shape[0]
    if 2 * s >= SUBLANES:
        pieces = []
        for blk in range(c // (2 * s)):
            r = blk * 2 * s + s - 1 + (1 if reverse else 0)
            pieces.append(jnp.broadcast_to(b[r:r + 1, :], (2 * s, b.shape[1])))
        return jnp.concatenate(pieces, axis=0)
    pos = row % (2 * s)
    mid = s if reverse else s - 1
    ref = b
    for p in range(2 * s):
        if p != mid:
            ref = jnp.where(pos == p, pltpu.roll(b, (p - mid) % c, 0), ref)
    return ref


def _split3(x):
    hi = x.astype(BF16)
    r = x - hi.astype(F32)
    mid = r.astype(BF16)
    lo = (r - mid.astype(F32)).astype(BF16)
    return hi, mid, lo


def _chunk_cumsum(g, reverse):
    c = g.shape[0]
    ri = lax.broadcasted_iota(jnp.int32, (c, c), 0)
    ci = lax.broadcasted_iota(jnp.int32, (c, c), 1)
    tri = jnp.where((ci >= ri) if reverse else (ci <= ri), 1.0, 0.0).astype(BF16)
    hi, mid, lo = _split3(g)
    return _mm(tri, lo) + _mm(tri, mid) + _mm(tri, hi)


def _gla_tables(c):
    levels = int(math.log2(c))
    row = np.arange(c)
    side = np.zeros((levels, 2, 2, c, GLA_DK), np.float32)
    same = np.zeros((levels, c, c), np.float32)
    for lvl in range(levels):
        s = 1 << lvl
        upper = (row % (2 * s)) >= s
        for rev in range(2):
            q_side = ~upper if rev else upper
            side[lvl, rev, 0] = np.where(q_side, 0.0, NEG_BIG)[:, None]
            side[lvl, rev, 1] = np.where(q_side, NEG_BIG, 0.0)[:, None]
        same[lvl] = (row[:, None] // (2 * s)) == (row[None, :] // (2 * s))
    return jnp.asarray(side), jnp.asarray(same)


def _gla_direction(q, k, v, b, st_ref, side_ref, same_ref, reverse):
    c, dk = b.shape
    row = lax.broadcasted_iota(jnp.int32, (c, dk), 0)
    total = b[0:1, :] if reverse else b[c - 1:c, :]
    qf = q.astype(F32)
    kf = k.astype(F32)
    rev = 1 if reverse else 0

    st = st_ref[...]
    out = _nt((qf * jnp.exp2(b)).astype(BF16), st.astype(BF16))

    ri = lax.broadcasted_iota(jnp.int32, (c, c), 0)
    ci = lax.broadcasted_iota(jnp.int32, (c, c), 1)
    a = jnp.where(ri == ci, _nt(q, k), 0.0)
    s = 1
    lvl = 0
    while s < c:
        t = b - _level_ref(b, s, row, reverse)
        ql = (qf * jnp.exp2(t + side_ref[lvl, rev, 0])).astype(BF16)
        kl = (kf * jnp.exp2(side_ref[lvl, rev, 1] - t)).astype(BF16)
        al = _nt(ql, kl)
        a = a + (al if 2 * s == c else al * same_ref[lvl])
        s *= 2
        lvl += 1
    out = out + _mm(a.astype(BF16), v)

    kd = (kf * jnp.exp2(total - b)).astype(BF16)
    st_ref[...] = st * jnp.exp2(total) + _tn(v, kd)
    return out


def _gla_kernel(qf_ref, kf_ref, vf_ref, gf_ref, qb_ref, kb_ref, vb_ref, gb_ref, sf0_ref, sb0_ref,
                side_ref, same_ref, of_ref, ob_ref, sf1_ref, sb1_ref, sf_ref, sb_ref):
    i = pl.program_id(2)
    heads = sf_ref.shape[0]

    @pl.when(i == 0)
    def _():
        sf_ref[...] = sf0_ref[...]
        sb_ref[...] = sb0_ref[...]

    bf = _chunk_cumsum(gf_ref[...] * LOG2_E, False)
    bb = _chunk_cumsum(gb_ref[...] * LOG2_E, True)
    for h in range(heads):
        ks = slice(h * GLA_DK, (h + 1) * GLA_DK)
        vs = slice(h * GLA_DV, (h + 1) * GLA_DV)
        of_ref[:, vs] = _gla_direction(qf_ref[:, ks], kf_ref[:, ks], vf_ref[:, vs], bf[:, ks],
                                       sf_ref.at[h], side_ref, same_ref, False)
        ob_ref[:, vs] = _gla_direction(qb_ref[:, ks], kb_ref[:, ks], vb_ref[:, vs], bb[:, ks],
                                       sb_ref.at[h], side_ref, same_ref, True)
    sf1_ref[...] = sf_ref[...]
    sb1_ref[...] = sb_ref[...]


def _gla_call(gq, gk, gv, gate, sf0, sb0):
    b, t, _ = gq.shape
    c = _largest_divisor(t, (GLA_CHUNK,))
    n = t // c
    hb = GLA_HEAD_BLOCK
    nhb = GLA_HEADS // hb
    fwd = lambda w, off=0: pl.BlockSpec((None, c, hb * w), lambda bi, h, i: (bi, i, h + off))
    bwd = lambda w, off=0: pl.BlockSpec((None, c, hb * w), lambda bi, h, i: (bi, n - 1 - i, h + off))
    state = pl.BlockSpec((None, hb, GLA_DV, GLA_DK), lambda bi, h, i: (bi, h, 0, 0))
    o_shape = jax.ShapeDtypeStruct((b, t, GLA_HEADS * GLA_DV), F32)
    s_shape = jax.ShapeDtypeStruct((b, GLA_HEADS, GLA_DV, GLA_DK), F32)
    side, same = _gla_tables(c)
    return pl.pallas_call(
        _gla_kernel,
        out_shape=[o_shape, o_shape, s_shape, s_shape],
        grid=(b, nhb, n),
        in_specs=[fwd(GLA_DK), fwd(GLA_DK), fwd(GLA_DV), fwd(GLA_DK),
                  bwd(GLA_DK), bwd(GLA_DK), bwd(GLA_DV), bwd(GLA_DK, nhb),
                  state, state, _resident(side.shape), _resident(same.shape)],
        out_specs=[fwd(GLA_DV), bwd(GLA_DV), state, state],
        scratch_shapes=[pltpu.VMEM((hb, GLA_DV, GLA_DK), F32), pltpu.VMEM((hb, GLA_DV, GLA_DK), F32)],
        compiler_params=_params("parallel", "parallel", "arbitrary"),
        name="gla",
    )(gq, gk, gv, gate, gq, gk, gv, gate, sf0, sb0, side, same)


def _da_kernel(lq1_ref, lk1_ref, lq2_ref, lk2_ref, gain_ref, qt_ref, k_ref, vt_ref, y_ref,
               *, lam_init):
    tq = qt_ref.shape[-1]
    nk = k_ref.shape[0]
    qt = qt_ref[...]
    sub = lax.broadcasted_iota(jnp.int32, qt.shape, 0)
    zero = jnp.zeros_like(qt)
    qm = jnp.concatenate([jnp.where(sub < DA_DH, qt, zero), jnp.where(sub < DA_DH, zero, qt)], axis=1)

    def scores(j):
        s = _mm(k_ref[j], qm)
        return s, jnp.broadcast_to(jnp.max(s, axis=0, keepdims=True), (SUBLANES, s.shape[1]))

    def softmax(s, x, m_prev):
        m_new = jnp.maximum(m_prev, x)
        alpha = jnp.exp2(m_prev - m_new)
        groups = s.shape[0] // SUBLANES
        p = jnp.exp2(s.reshape(groups, SUBLANES, s.shape[1]) - m_new[None])
        return p.reshape(s.shape).astype(BF16), alpha[0:1, :], m_new

    m = jnp.full((SUBLANES, 2 * tq), NEG_BIG, F32)
    acc = jnp.zeros((vt_ref.shape[1], 2 * tq), F32)
    s, x = scores(0)
    pending = None
    for j in range(nk):
        p, alpha, m = softmax(s, x, m)
        if j + 1 < nk:
            s, x = scores(j + 1)
        if pending is not None:
            acc = pending[2] * acc + _mm(vt_ref[pending[0]], pending[1])
        pending = (j, p, alpha)
    acc = pending[2] * acc + _mm(vt_ref[pending[0]], pending[1])

    lam = (jnp.exp(jnp.sum(lq1_ref[...] * lk1_ref[...], keepdims=True))
           - jnp.exp(jnp.sum(lq2_ref[...] * lk2_ref[...], keepdims=True)) + lam_init)
    o = acc[:DA_DV] / acc[DA_DV:DA_DV + 1]
    ot = o[:, :tq] - lam * o[:, tq:]
    y = _rms(ot.T, gain_ref[...]) * (1.0 - lam_init)
    y_ref[...] = y.astype(y_ref.dtype)


def _da_call(lams, gain, qt, k5, vt5, lam_init):
    b, h, dq, t = qt.shape
    nk, tk = k5.shape[2], k5.shape[3]
    dva = vt5.shape[3]
    tq = _largest_divisor(t, DA_QUERY_BLOCKS)
    small = pl.BlockSpec((1, DA_DH), lambda bi, hi, i: (0, 0))
    return pl.pallas_call(
        functools.partial(_da_kernel, lam_init=lam_init),
        out_shape=jax.ShapeDtypeStruct((b, t, h * DA_DV), BF16),
        grid=(b, h, t // tq),
        in_specs=[small, small, small, small,
                  pl.BlockSpec((1, DA_DV), lambda bi, hi, i: (0, 0)),
                  pl.BlockSpec((None, None, dq, tq), lambda bi, hi, i: (bi, hi, 0, i)),
                  pl.BlockSpec((None, None, nk, tk, dq), lambda bi, hi, i: (bi, hi, 0, 0, 0)),
                  pl.BlockSpec((None, None, nk, dva, tk), lambda bi, hi, i: (bi, hi, 0, 0, 0))],
        out_specs=pl.BlockSpec((None, tq, DA_DV), lambda bi, hi, i: (bi, i, hi)),
        compiler_params=_params("parallel", "parallel", "parallel"),
        name="da",
    )(*lams, gain, qt, k5, vt5)


def _merge_kernel(x_ref, gt_ref, yda_ref, of_ref, ob_ref, gout_ref, mrg_ref, ggain_ref, pgain_ref,
                  wda_ref, wgla_ref, wout_ref, o_ref):
    d = x_ref.shape[-1]
    o_gla = of_ref[...] + ob_ref[...]
    ggain = ggain_ref[...]
    heads = [_rms(o_gla[:, h * GLA_DV:(h + 1) * GLA_DV], ggain) for h in range(GLA_HEADS)]
    y_gla = (jnp.concatenate(heads, axis=-1) * gout_ref[...].astype(F32)).astype(BF16)
    gates = mrg_ref[...].astype(F32)
    y = (gates[:, :d] * _mm(yda_ref[...], wda_ref[...])
         + gates[:, d:] * _mm(y_gla, wgla_ref[...]))
    z = _mm(y.astype(BF16), wout_ref[...])
    o_ref[...] = x_ref[...] + gt_ref[...] * _rms(z, pgain_ref[...])


def _merge_call(x, gt1, y_da, o_f, o_b, gout, mrg, ggain, pgain, w_da, w_gla, w_out):
    b, t, d = x.shape
    tm = _largest_divisor(t, (256, 128))
    row = lambda width: pl.BlockSpec((None, tm, width), lambda bi, i: (bi, i, 0))
    vec = pl.BlockSpec((None, 1, d), lambda bi, i: (bi, 0, 0))
    return pl.pallas_call(
        _merge_kernel,
        out_shape=jax.ShapeDtypeStruct((b, t, d), F32),
        grid=(b, t // tm),
        in_specs=[row(d), vec, row(d), row(d), row(d), row(d), row(2 * d),
                  _resident(ggain.shape), _resident(pgain.shape),
                  _resident(w_da.shape), _resident(w_gla.shape), _resident(w_out.shape)],
        out_specs=row(d),
        compiler_params=_params("parallel", "parallel"),
        name="merge",
    )(x, gt1, y_da, o_f, o_b, gout, mrg, ggain, pgain, w_da, w_gla, w_out)


def _mlp_kernel(x_ref, sh_ref, sc_ref, gt_ref, pre_ref, post_ref, w1_ref, w2_ref, o_ref):
    x = x_ref[...]
    h = (_rms(x, pre_ref[...]) * (1.0 + sc_ref[...]) + sh_ref[...]).astype(BF16)
    u = jnp.maximum(_mm(h, w1_ref[...]), 0.0)
    z = _mm((u * u).astype(BF16), w2_ref[...])
    o_ref[...] = x + gt_ref[...] * _rms(z, post_ref[...])


def _mlp_call(x, sh2, sc2, gt2, pre, post, w1, w2):
    b, t, d = x.shape
    tm = _largest_divisor(t, (256, 128))
    row = pl.BlockSpec((None, tm, d), lambda bi, i: (bi, i, 0))
    vec = pl.BlockSpec((None, 1, d), lambda bi, i: (bi, 0, 0))
    return pl.pallas_call(
        _mlp_kernel,
        out_shape=jax.ShapeDtypeStruct((b, t, d), F32),
        grid=(b, t // tm),
        in_specs=[row, vec, vec, vec, _resident(pre.shape), _resident(post.shape),
                  _resident(w1.shape), _resident(w2.shape)],
        out_specs=row,
        compiler_params=_params("parallel", "parallel"),
        name="mlp",
    )(x, sh2, sc2, gt2, pre, post, w1, w2)


def _pack_w_in(w):
    low0 = _OFF_LOW
    low1 = low0 + 2 * GLA_GATE_RANK
    pad = jnp.zeros((w.shape[0], LOW_PAD - 2 * GLA_GATE_RANK), w.dtype)
    return jnp.concatenate([w[:, :low0], w[:, low0:low1], pad, w[:, low1:]], axis=1).astype(BF16)


def _pack_gate_up(w_gate_up, b_gate_up):
    hk = GLA_HEADS * GLA_DK
    w = jnp.zeros((LOW_PAD, 2 * hk), F32)
    for z in range(2):
        w = w.at[z * GLA_GATE_RANK:(z + 1) * GLA_GATE_RANK, z * hk:(z + 1) * hk].set(w_gate_up[z])
    return w.astype(BF16), b_gate_up.reshape(1, 2 * hk).astype(F32)


def kernel(x, c, ctx, c_ctx, w_mod, b_mod, pre_norm1, w_in, w_gate_up, b_gate_up, lambda_q1, lambda_k1,
           lambda_q2, lambda_k2, da_head_norm, gla_head_norm, w_branch_da, w_branch_gla, w_out, post_norm1,
           pre_norm2, w_ff1, w_ff2, post_norm2):
    b, t, d = x.shape
    n_ctx = ctx.shape[1]
    assert w_in.shape[0] == 1, "single-layer block only"
    assert d == DA_HEADS * DA_DV == GLA_HEADS * GLA_DV and t % GRID_W == 0
    layer = 0
    lam_init = 0.8 - 0.6 * math.exp(-0.3 * layer)

    rows = -(-(b + 1) // SUBLANES) * SUBLANES
    cc = jnp.zeros((rows, d), F32).at[:b].set(c).at[b].set(c_ctx)
    mod = _mod_call(cc, w_mod[layer], b_mod[layer])
    lat = [mod[:b, i * d:(i + 1) * d].reshape(b, 1, d) for i in range(N_MOD)]
    cmod = [jnp.broadcast_to(mod[b, i * d:(i + 1) * d].reshape(1, 1, d), (b, 1, d)) for i in range(2)]
    sh1, sc1, gt1, sh2, sc2, gt2 = lat

    w_packed = _pack_w_in(w_in[layer])
    w_up, b_up = _pack_gate_up(w_gate_up[layer], b_gate_up[layer])
    gain1 = pre_norm1[layer].reshape(1, d)

    q, k, v, gq, gk, gv, gate, gout, mrg = _proj_call(
        x, sh1, sc1, gain1, w_packed, w_up, b_up, *_rope_tables(t))
    _, ck, cv, _, cgk, cgv, cgate, _, _ = _proj_call(
        ctx, cmod[0], cmod[1], gain1, w_packed, w_up, b_up, *_identity_tables(n_ctx))

    zero_state = jnp.zeros((b, GLA_HEADS, GLA_DV, GLA_DK), F32)
    _, _, sf0, sb0 = _gla_call(jnp.zeros_like(cgk), cgk, cgv, cgate, zero_state, zero_state)
    o_f, o_b, _, _ = _gla_call(gq, gk, gv, gate, sf0, sb0)

    tk_all = t + n_ctx
    tk = _largest_divisor(tk_all, DA_KEY_BLOCKS)
    nk = tk_all // tk
    qt = q.reshape(b, t, DA_HEADS, 2 * DA_DH).transpose(0, 2, 3, 1)
    k5 = (jnp.concatenate([k, ck], axis=1).reshape(b, nk, tk, DA_HEADS, 2 * DA_DH)
          .transpose(0, 3, 1, 2, 4))
    vt5 = (jnp.concatenate([v, cv], axis=1).reshape(b, nk, tk, DA_HEADS, DA_DV)
           .transpose(0, 3, 1, 4, 2))
    vt5 = jnp.concatenate([vt5, jnp.ones((b, DA_HEADS, nk, BF16_ROWS, tk), BF16)], axis=3)
    lams = [a[layer].reshape(1, DA_DH) for a in (lambda_q1, lambda_k1, lambda_q2, lambda_k2)]
    y_da = _da_call(lams, da_head_norm[layer].reshape(1, DA_DV), qt, k5, vt5, lam_init)

    x1 = _merge_call(x, gt1, y_da, o_f, o_b, gout, mrg,
                     gla_head_norm[layer].reshape(1, GLA_DV), post_norm1[layer].reshape(1, d),
                     w_branch_da[layer].astype(BF16), w_branch_gla[layer].astype(BF16),
                     w_out[layer].astype(BF16))
    return _mlp_call(x1, sh2, sc2, gt2, pre_norm2[layer].reshape(1, d), post_norm2[layer].reshape(1, d),
                     w_ff1[layer].astype(BF16), w_ff2[layer].astype(BF16))
```

```python
import functools
import math

import numpy as np
import jax
import jax.numpy as jnp
from jax import lax
from jax.experimental import pallas as pl
from jax.experimental.pallas import tpu as pltpu

F32 = jnp.float32
BF16 = jnp.bfloat16

GRID_W = 64
DA_HEADS = 8
DA_DH = 64
DA_DV = 2 * DA_DH
GLA_HEADS = 4
GLA_DK = 128
GLA_DV = 256
GLA_GATE_RANK = 16
GLA_GATE_NORM = 16.0
ROPE_THETA = 10000.0
ROPE_AXIS_DIM = DA_DH // 2
ROPE_HALF = ROPE_AXIS_DIM // 2
N_MOD = 6
EPS = 1e-6
NEG_BIG = -1e30

LANES = 128
SUBLANES = 8
BF16_ROWS = 16
LOG2_E = math.log2(math.e)
VMEM_LIMIT = 56 * 1024 * 1024

GLA_CHUNK = 128
GLA_HEAD_BLOCK = 4
DA_KEY_BLOCKS = (768, 512, 384, 256, 128)
DA_QUERY_BLOCKS = (512, 256, 128)
DA_BOUND_LIMIT = 50.0
DA_BOUND_SLACK = 1.01
LOW_PAD = LANES


def _nt(a, b):
    return lax.dot_general(a, b, (((1,), (1,)), ((), ())), preferred_element_type=F32)


def _tn(a, b):
    return lax.dot_general(a, b, (((0,), (0,)), ((), ())), preferred_element_type=F32)


def _mm(a, b):
    return jnp.dot(a, b, preferred_element_type=F32)


def _rms(x, gain):
    return x * lax.rsqrt(jnp.mean(x * x, axis=-1, keepdims=True) + EPS) * gain


def _params(*sem):
    return pltpu.CompilerParams(dimension_semantics=sem, vmem_limit_bytes=VMEM_LIMIT)


def _resident(shape):
    nd = len(shape)
    return pl.BlockSpec(shape, lambda *_: (0,) * nd, pipeline_mode=pl.Buffered(1))


def _largest_divisor(n, candidates):
    for cand in candidates:
        if n % cand == 0:
            return cand
    raise ValueError(f"no tile in {candidates} divides {n}")


def _mod_kernel(c_ref, w_ref, b_ref, o_ref):
    c = c_ref[...]
    s = (c * jax.nn.sigmoid(c)).astype(BF16)
    o_ref[...] = _mm(s, w_ref[...].astype(BF16)) + b_ref[...]


def _mod_call(cc, w_mod, b_mod):
    rows, d = cc.shape
    n = w_mod.shape[1]
    bn = _largest_divisor(n, (1024, 512, 256, 128))
    return pl.pallas_call(
        _mod_kernel,
        out_shape=jax.ShapeDtypeStruct((rows, n), F32),
        grid=(n // bn,),
        in_specs=[pl.BlockSpec((rows, d), lambda j: (0, 0)),
                  pl.BlockSpec((d, bn), lambda j: (0, j)),
                  pl.BlockSpec((1, bn), lambda j: (0, j))],
        out_specs=pl.BlockSpec((rows, bn), lambda j: (0, j)),
        compiler_params=_params("arbitrary"),
        name="mod",
    )(cc, w_mod, b_mod.reshape(1, n))


_OFF_Q = 0
_OFF_K = _OFF_Q + DA_HEADS * 2 * DA_DH
_OFF_V = _OFF_K + DA_HEADS * 2 * DA_DH
_OFF_GQ = _OFF_V + DA_HEADS * DA_DV
_OFF_GK = _OFF_GQ + GLA_HEADS * GLA_DK
_OFF_GV = _OFF_GK + GLA_HEADS * GLA_DK
_OFF_LOW = _OFF_GV + GLA_HEADS * GLA_DV
_OFF_GOUT = _OFF_LOW + LOW_PAD
_OFF_MERGE = _OFF_GOUT + GLA_HEADS * GLA_DV
_N_GATE = 2 * GLA_HEADS * GLA_DK


def _proj_kernel(x_ref, sh_ref, sc_ref, gain_ref, w_ref, wup_ref, bup_ref, cos_ref, sa_ref, sb_ref,
                 q_ref, k_ref, v_ref, gq_ref, gk_ref, gv_ref, gate_ref, gout_ref, mrg_ref):
    d = x_ref.shape[-1]
    h = (_rms(x_ref[...], gain_ref[...]) * (1.0 + sc_ref[...]) + sh_ref[...]).astype(BF16)
    cos, sa, sb = cos_ref[...], sa_ref[...], sb_ref[...]

    def roped(off, out_ref, scale):
        acc = _mm(h, w_ref[:, off:off + d])
        for s in range(d // LANES):
            a = acc[:, s * LANES:(s + 1) * LANES]
            r = (a * cos + pltpu.roll(a, ROPE_HALF, 1) * sa
                 + pltpu.roll(a, LANES - ROPE_HALF, 1) * sb)
            out_ref[:, s * LANES:(s + 1) * LANES] = (r * scale).astype(out_ref.dtype)

    roped(_OFF_Q, q_ref, DA_DH ** -0.5 * LOG2_E)
    roped(_OFF_K, k_ref, 1.0)
    v_ref[...] = _mm(h, w_ref[:, _OFF_V:_OFF_GQ]).astype(v_ref.dtype)
    gq_ref[...] = (_mm(h, w_ref[:, _OFF_GQ:_OFF_GK]) * (GLA_DK ** -0.5)).astype(gq_ref.dtype)
    gk_ref[...] = _mm(h, w_ref[:, _OFF_GK:_OFF_GV]).astype(gk_ref.dtype)
    gv_ref[...] = _mm(h, w_ref[:, _OFF_GV:_OFF_LOW]).astype(gv_ref.dtype)
    low = _mm(h, w_ref[:, _OFF_LOW:_OFF_GOUT]).astype(BF16)
    z = _mm(low, wup_ref[...]) + bup_ref[...]
    log_sig = jnp.minimum(z, 0.0) - jnp.log1p(jnp.exp(-jnp.abs(z)))
    gate_ref[...] = log_sig / GLA_GATE_NORM
    go = _mm(h, w_ref[:, _OFF_GOUT:_OFF_MERGE])
    gout_ref[...] = (go * jax.nn.sigmoid(go)).astype(gout_ref.dtype)
    mg = _mm(h, w_ref[:, _OFF_MERGE:_OFF_MERGE + 2 * d])
    mrg_ref[...] = jax.nn.sigmoid(mg).astype(mrg_ref.dtype)


def _proj_call(x, shift, scale, gain, w_packed, w_up, b_up, cos, sa, sb):
    b, t, d = x.shape
    tm = _largest_divisor(t, (256, 128))
    row = lambda width: pl.BlockSpec((None, tm, width), lambda bi, i: (bi, i, 0))
    vec = pl.BlockSpec((None, 1, d), lambda bi, i: (bi, 0, 0))
    tab = pl.BlockSpec((tm, LANES), lambda bi, i: (i, 0))
    widths = (d, d, d, GLA_HEADS * GLA_DK, GLA_HEADS * GLA_DK, GLA_HEADS * GLA_DV,
              _N_GATE, GLA_HEADS * GLA_DV, 2 * d)
    dtypes = (BF16, BF16, BF16, BF16, BF16, BF16, F32, BF16, BF16)
    return pl.pallas_call(
        _proj_kernel,
        out_shape=[jax.ShapeDtypeStruct((b, t, w), dt) for w, dt in zip(widths, dtypes)],
        grid=(b, t // tm),
        in_specs=[row(d), vec, vec, _resident((1, d)), _resident(w_packed.shape),
                  _resident(w_up.shape), _resident(b_up.shape), tab, tab, tab],
        out_specs=[row(w) for w in widths],
        compiler_params=_params("parallel", "parallel"),
        name="proj",
    )(x, shift, scale, gain, w_packed, w_up, b_up, cos, sa, sb)


def _rope_tables(t):
    pos = jnp.arange(t)
    inv = ROPE_THETA ** (-jnp.arange(0, ROPE_AXIS_DIM, 2, dtype=F32) / ROPE_AXIS_DIM)
    ang_r = (pos // GRID_W).astype(F32)[:, None] * inv
    ang_c = (pos % GRID_W).astype(F32)[:, None] * inv
    zero = jnp.zeros_like(ang_r)
    cos = jnp.concatenate([jnp.cos(ang_r)] * 2 + [jnp.cos(ang_c)] * 2, axis=-1)
    sa = jnp.concatenate([zero, jnp.sin(ang_r), zero, jnp.sin(ang_c)], axis=-1)
    sb = jnp.concatenate([-jnp.sin(ang_r), zero, -jnp.sin(ang_c), zero], axis=-1)
    rep = LANES // DA_DH
    return tuple(jnp.tile(a, (1, rep)) for a in (cos, sa, sb))


def _identity_tables(t):
    return jnp.ones((t, LANES), F32), jnp.zeros((t, LANES), F32), jnp.zeros((t, LANES), F32)


def _level_ref(b, s, row, reverse):
    c = b.shape[0]
    if 2 * s >= SUBLANES:
        pieces = []
        for blk in range(c // (2 * s)):
            r = blk * 2 * s + s - 1 + (1 if reverse else 0)
            pieces.append(jnp.broadcast_to(b[r:r + 1, :], (2 * s, b.shape[1])))
        return jnp.concatenate(pieces, axis=0)
    pos = row % (2 * s)
    mid = s if reverse else s - 1
    ref = b
    for p in range(2 * s):
        if p != mid:
            ref = jnp.where(pos == p, pltpu.roll(b, (p - mid) % c, 0), ref)
    return ref


def _split3(x):
    hi = x.astype(BF16)
    r = x - hi.astype(F32)
    mid = r.astype(BF16)
    lo = (r - mid.astype(F32)).astype(BF16)
    return hi, mid, lo


def _chunk_cumsum(g, reverse):
    c = g.shape[0]
    ri = lax.broadcasted_iota(jnp.int32, (c, c), 0)
    ci = lax.broadcasted_iota(jnp.int32, (c, c), 1)
    tri = jnp.where((ci >= ri) if reverse else (ci <= ri), 1.0, 0.0).astype(BF16)
    hi, mid, lo = _split3(g)
    return _mm(tri, lo) + _mm(tri, mid) + _mm(tri, hi)


def _gla_tables(c):
    levels = int(math.log2(c))
    row = np.arange(c)
    side = np.zeros((levels, 2, 2, c, GLA_DK), np.float32)
    same = np.zeros((levels, c, c), np.float32)
    for lvl in range(levels):
        s = 1 << lvl
        upper = (row % (2 * s)) >= s
        for rev in range(2):
            q_side = ~upper if rev else upper
            side[lvl, rev, 0] = np.where(q_side, 0.0, NEG_BIG)[:, None]
            side[lvl, rev, 1] = np.where(q_side, NEG_BIG, 0.0)[:, None]
        same[lvl] = (row[:, None] // (2 * s)) == (row[None, :] // (2 * s))
    return jnp.asarray(side), jnp.asarray(same)


def _gla_direction(q, k, v, b, st_ref, side_ref, same_ref, reverse):
    c, dk = b.shape
    row = lax.broadcasted_iota(jnp.int32, (c, dk), 0)
    total = b[0:1, :] if reverse else b[c - 1:c, :]
    qf = q.astype(F32)
    kf = k.astype(F32)
    rev = 1 if reverse else 0

    st = st_ref[...]
    out = _nt((qf * jnp.exp2(b)).astype(BF16), st.astype(BF16))

    ri = lax.broadcasted_iota(jnp.int32, (c, c), 0)
    ci = lax.broadcasted_iota(jnp.int32, (c, c), 1)
    a = jnp.where(ri == ci, _nt(q, k), 0.0)
    s = 1
    lvl = 0
    while s < c:
        t = b - _level_ref(b, s, row, reverse)
        ql = (qf * jnp.exp2(t + side_ref[lvl, rev, 0])).astype(BF16)
        kl = (kf * jnp.exp2(side_ref[lvl, rev, 1] - t)).astype(BF16)
        al = _nt(ql, kl)
        a = a + (al if 2 * s == c else al * same_ref[lvl])
        s *= 2
        lvl += 1
    out = out + _mm(a.astype(BF16), v)

    kd = (kf * jnp.exp2(total - b)).astype(BF16)
    st_ref[...] = st * jnp.exp2(total) + _tn(v, kd)
    return out


def _gla_kernel(qf_ref, kf_ref, vf_ref, gf_ref, qb_ref, kb_ref, vb_ref, gb_ref, sf0_ref, sb0_ref,
                side_ref, same_ref, of_ref, ob_ref, sf1_ref, sb1_ref, sf_ref, sb_ref):
    i = pl.program_id(2)
    heads = sf_ref.shape[0]

    @pl.when(i == 0)
    def _():
        sf_ref[...] = sf0_ref[...]
        sb_ref[...] = sb0_ref[...]

    bf = _chunk_cumsum(gf_ref[...] * LOG2_E, False)
    bb = _chunk_cumsum(gb_ref[...] * LOG2_E, True)
    for h in range(heads):
        ks = slice(h * GLA_DK, (h + 1) * GLA_DK)
        vs = slice(h * GLA_DV, (h + 1) * GLA_DV)
        of_ref[:, vs] = _gla_direction(qf_ref[:, ks], kf_ref[:, ks], vf_ref[:, vs], bf[:, ks],
                                       sf_ref.at[h], side_ref, same_ref, False)
        ob_ref[:, vs] = _gla_direction(qb_ref[:, ks], kb_ref[:, ks], vb_ref[:, vs], bb[:, ks],
                                       sb_ref.at[h], side_ref, same_ref, True)
    sf1_ref[...] = sf_ref[...]
    sb1_ref[...] = sb_ref[...]


def _gla_call(gq, gk, gv, gate, sf0, sb0):
    b, t, _ = gq.shape
    c = _largest_divisor(t, (GLA_CHUNK,))
    n = t // c
    hb = GLA_HEAD_BLOCK
    nhb = GLA_HEADS // hb
    fwd = lambda w, off=0: pl.BlockSpec((None, c, hb * w), lambda bi, h, i: (bi, i, h + off))
    bwd = lambda w, off=0: pl.BlockSpec((None, c, hb * w), lambda bi, h, i: (bi, n - 1 - i, h + off))
    state = pl.BlockSpec((None, hb, GLA_DV, GLA_DK), lambda bi, h, i: (bi, h, 0, 0))
    o_shape = jax.ShapeDtypeStruct((b, t, GLA_HEADS * GLA_DV), F32)
    s_shape = jax.ShapeDtypeStruct((b, GLA_HEADS, GLA_DV, GLA_DK), F32)
    side, same = _gla_tables(c)
    return pl.pallas_call(
        _gla_kernel,
        out_shape=[o_shape, o_shape, s_shape, s_shape],
        grid=(b, nhb, n),
        in_specs=[fwd(GLA_DK), fwd(GLA_DK), fwd(GLA_DV), fwd(GLA_DK),
                  bwd(GLA_DK), bwd(GLA_DK), bwd(GLA_DV), bwd(GLA_DK, nhb),
                  state, state, _resident(side.shape), _resident(same.shape)],
        out_specs=[fwd(GLA_DV), bwd(GLA_DV), state, state],
        scratch_shapes=[pltpu.VMEM((hb, GLA_DV, GLA_DK), F32), pltpu.VMEM((hb, GLA_DV, GLA_DK), F32)],
        compiler_params=_params("parallel", "parallel", "arbitrary"),
        name="gla",
    )(gq, gk, gv, gate, gq, gk, gv, gate, sf0, sb0, side, same)


def _da_kernel(lq1_ref, lk1_ref, lq2_ref, lk2_ref, gain_ref, qt_ref, k_ref, vt_ref, y_ref,
               kmax_ref, *, lam_init):
    tq = qt_ref.shape[-1]
    nk, tk, dq = k_ref.shape
    dva = vt_ref.shape[1]
    qt = qt_ref[...]
    sub = lax.broadcasted_iota(jnp.int32, qt.shape, 0)
    zero = jnp.zeros_like(qt)
    qm = jnp.concatenate([jnp.where(sub < DA_DH, qt, zero), jnp.where(sub < DA_DH, zero, qt)], axis=1)

    @pl.when(pl.program_id(2) == 0)
    def _():
        di = lax.broadcasted_iota(jnp.int32, (dq, dq), 0)
        ci = lax.broadcasted_iota(jnp.int32, (dq, dq), 1)
        same_comp = jnp.where((di < DA_DH) == (ci < DA_DH), 1.0, 0.0).astype(BF16)

        def block_max(j, mx):
            kf = k_ref[j].astype(F32)
            nsq = _mm((kf * kf).astype(BF16), same_comp)
            return jnp.maximum(mx, jnp.max(nsq.reshape(tk // SUBLANES, SUBLANES, dq), axis=0))

        mx = lax.fori_loop(0, nk, block_max, jnp.zeros((SUBLANES, dq), F32))
        kmax_ref[...] = jnp.broadcast_to(jnp.max(mx, axis=0, keepdims=True), kmax_ref.shape)

    qf = qt.astype(F32)
    qsq = qf * qf
    kmsq = kmax_ref[...]
    bsq = jnp.concatenate([jnp.sum(qsq[:DA_DH], axis=0, keepdims=True) * kmsq[0:1, 0:1],
                           jnp.sum(qsq[DA_DH:], axis=0, keepdims=True) * kmsq[0:1, DA_DH:DA_DH + 1]],
                          axis=1)
    bound = jnp.sqrt(bsq) * DA_BOUND_SLACK
    in_range = jnp.max(bound) <= DA_BOUND_LIMIT

    def probs(s, shift):
        groups = s.shape[0] // SUBLANES
        p = jnp.exp2(s.reshape(groups, SUBLANES, s.shape[1]) - shift[None])
        return p.reshape(s.shape).astype(BF16)

    def finish(acc):
        lam = (jnp.exp(jnp.sum(lq1_ref[...] * lk1_ref[...], keepdims=True))
               - jnp.exp(jnp.sum(lq2_ref[...] * lk2_ref[...], keepdims=True)) + lam_init)
        o = acc[:DA_DV] / acc[DA_DV:DA_DV + 1]
        ot = o[:, :tq] - lam * o[:, tq:]
        y = _rms(ot.T, gain_ref[...]) * (1.0 - lam_init)
        y_ref[...] = y.astype(y_ref.dtype)

    @pl.when(in_range)
    def _():
        shift = jnp.broadcast_to(bound, (SUBLANES, 2 * tq))
        acc = jnp.zeros((dva, 2 * tq), F32)
        for j in range(nk):
            acc = acc + _mm(vt_ref[j], probs(_mm(k_ref[j], qm), shift))
        finish(acc)

    @pl.when(jnp.logical_not(in_range))
    def _():
        def block(j, carry):
            m_prev, acc = carry
            s = _mm(k_ref[j], qm)
            m_new = jnp.maximum(m_prev, jnp.max(s, axis=0, keepdims=True))
            acc = jnp.exp2(m_prev - m_new) * acc + _mm(vt_ref[j], probs(s, jnp.broadcast_to(m_new, (SUBLANES, 2 * tq))))
            return m_new, acc

        _, acc = lax.fori_loop(0, nk, block, (jnp.full((1, 2 * tq), NEG_BIG, F32),
                                              jnp.zeros((dva, 2 * tq), F32)))
        finish(acc)


def _da_call(lams, gain, qt, k5, vt5, lam_init):
    b, h, dq, t = qt.shape
    nk, tk = k5.shape[2], k5.shape[3]
    dva = vt5.shape[3]
    tq = _largest_divisor(t, DA_QUERY_BLOCKS)
    small = pl.BlockSpec((1, DA_DH), lambda bi, hi, i: (0, 0))
    return pl.pallas_call(
        functools.partial(_da_kernel, lam_init=lam_init),
        out_shape=jax.ShapeDtypeStruct((b, t, h * DA_DV), BF16),
        grid=(b, h, t // tq),
        in_specs=[small, small, small, small,
                  pl.BlockSpec((1, DA_DV), lambda bi, hi, i: (0, 0)),
                  pl.BlockSpec((None, None, dq, tq), lambda bi, hi, i: (bi, hi, 0, i)),
                  pl.BlockSpec((None, None, nk, tk, dq), lambda bi, hi, i: (bi, hi, 0, 0, 0)),
                  pl.BlockSpec((None, None, nk, dva, tk), lambda bi, hi, i: (bi, hi, 0, 0, 0))],
        out_specs=pl.BlockSpec((None, tq, DA_DV), lambda bi, hi, i: (bi, i, hi)),
        scratch_shapes=[pltpu.VMEM((SUBLANES, dq), F32)],
        compiler_params=_params("parallel", "parallel", "arbitrary"),
        name="da",
    )(*lams, gain, qt, k5, vt5)


def _merge_kernel(x_ref, gt_ref, yda_ref, of_ref, ob_ref, gout_ref, mrg_ref, ggain_ref, pgain_ref,
                  wda_ref, wgla_ref, wout_ref, o_ref):
    d = x_ref.shape[-1]
    o_gla = of_ref[...] + ob_ref[...]
    ggain = ggain_ref[...]
    heads = [_rms(o_gla[:, h * GLA_DV:(h + 1) * GLA_DV], ggain) for h in range(GLA_HEADS)]
    y_gla = (jnp.concatenate(heads, axis=-1) * gout_ref[...].astype(F32)).astype(BF16)
    gates = mrg_ref[...].astype(F32)
    y = (gates[:, :d] * _mm(yda_ref[...], wda_ref[...])
         + gates[:, d:] * _mm(y_gla, wgla_ref[...]))
    z = _mm(y.astype(BF16), wout_ref[...])
    o_ref[...] = x_ref[...] + gt_ref[...] * _rms(z, pgain_ref[...])


def _merge_call(x, gt1, y_da, o_f, o_b, gout, mrg, ggain, pgain, w_da, w_gla, w_out):
    b, t, d = x.shape
    tm = _largest_divisor(t, (256, 128))
    row = lambda width: pl.BlockSpec((None, tm, width), lambda bi, i: (bi, i, 0))
    vec = pl.BlockSpec((None, 1, d), lambda bi, i: (bi, 0, 0))
    return pl.pallas_call(
        _merge_kernel,
        out_shape=jax.ShapeDtypeStruct((b, t, d), F32),
        grid=(b, t // tm),
        in_specs=[row(d), vec, row(d), row(d), row(d), row(d), row(2 * d),
                  _resident(ggain.shape), _resident(pgain.shape),
                  _resident(w_da.shape), _resident(w_gla.shape), _resident(w_out.shape)],
        out_specs=row(d),
        compiler_params=_params("parallel", "parallel"),
        name="merge",
    )(x, gt1, y_da, o_f, o_b, gout, mrg, ggain, pgain, w_da, w_gla, w_out)


def _mlp_kernel(x_ref, sh_ref, sc_ref, gt_ref, pre_ref, post_ref, w1_ref, w2_ref, o_ref):
    x = x_ref[...]
    h = (_rms(x, pre_ref[...]) * (1.0 + sc_ref[...]) + sh_ref[...]).astype(BF16)
    u = jnp.maximum(_mm(h, w1_ref[...]), 0.0)
    z = _mm((u * u).astype(BF16), w2_ref[...])
    o_ref[...] = x + gt_ref[...] * _rms(z, post_ref[...])


def _mlp_call(x, sh2, sc2, gt2, pre, post, w1, w2):
    b, t, d = x.shape
    tm = _largest_divisor(t, (256, 128))
    row = pl.BlockSpec((None, tm, d), lambda bi, i: (bi, i, 0))
    vec = pl.BlockSpec((None, 1, d), lambda bi, i: (bi, 0, 0))
    return pl.pallas_call(
        _mlp_kernel,
        out_shape=jax.ShapeDtypeStruct((b, t, d), F32),
        grid=(b, t // tm),
        in_specs=[row, vec, vec, vec, _resident(pre.shape), _resident(post.shape),
                  _resident(w1.shape), _resident(w2.shape)],
        out_specs=row,
        compiler_params=_params("parallel", "parallel"),
        name="mlp",
    )(x, sh2, sc2, gt2, pre, post, w1, w2)


def _pack_w_in(w):
    low0 = _OFF_LOW
    low1 = low0 + 2 * GLA_GATE_RANK
    pad = jnp.zeros((w.shape[0], LOW_PAD - 2 * GLA_GATE_RANK), w.dtype)
    return jnp.concatenate([w[:, :low0], w[:, low0:low1], pad, w[:, low1:]], axis=1).astype(BF16)


def _pack_gate_up(w_gate_up, b_gate_up):
    hk = GLA_HEADS * GLA_DK
    w = jnp.zeros((LOW_PAD, 2 * hk), F32)
    for z in range(2):
        w = w.at[z * GLA_GATE_RANK:(z + 1) * GLA_GATE_RANK, z * hk:(z + 1) * hk].set(w_gate_up[z])
    return w.astype(BF16), b_gate_up.reshape(1, 2 * hk).astype(F32)


def kernel(x, c, ctx, c_ctx, w_mod, b_mod, pre_norm1, w_in, w_gate_up, b_gate_up, lambda_q1, lambda_k1,
           lambda_q2, lambda_k2, da_head_norm, gla_head_norm, w_branch_da, w_branch_gla, w_out, post_norm1,
           pre_norm2, w_ff1, w_ff2, post_norm2):
    b, t, d = x.shape
    n_ctx = ctx.shape[1]
    assert w_in.shape[0] == 1, "single-layer block only"
    assert d == DA_HEADS * DA_DV == GLA_HEADS * GLA_DV and t % GRID_W == 0
    layer = 0
    lam_init = 0.8 - 0.6 * math.exp(-0.3 * layer)

    rows = -(-(b + 1) // SUBLANES) * SUBLANES
    cc = jnp.zeros((rows, d), F32).at[:b].set(c).at[b].set(c_ctx)
    mod = _mod_call(cc, w_mod[layer], b_mod[layer])
    lat = [mod[:b, i * d:(i + 1) * d].reshape(b, 1, d) for i in range(N_MOD)]
    cmod = [jnp.broadcast_to(mod[b, i * d:(i + 1) * d].reshape(1, 1, d), (b, 1, d)) for i in range(2)]
    sh1, sc1, gt1, sh2, sc2, gt2 = lat

    w_packed = _pack_w_in(w_in[layer])
    w_up, b_up = _pack_gate_up(w_gate_up[layer], b_gate_up[layer])
    gain1 = pre_norm1[layer].reshape(1, d)

    q, k, v, gq, gk, gv, gate, gout, mrg = _proj_call(
        x, sh1, sc1, gain1, w_packed, w_up, b_up, *_rope_tables(t))
    _, ck, cv, _, cgk, cgv, cgate, _, _ = _proj_call(
        ctx, cmod[0], cmod[1], gain1, w_packed, w_up, b_up, *_identity_tables(n_ctx))

    zero_state = jnp.zeros((b, GLA_HEADS, GLA_DV, GLA_DK), F32)
    _, _, sf0, sb0 = _gla_call(jnp.zeros_like(cgk), cgk, cgv, cgate, zero_state, zero_state)
    o_f, o_b, _, _ = _gla_call(gq, gk, gv, gate, sf0, sb0)

    tk_all = t + n_ctx
    tk = _largest_divisor(tk_all, DA_KEY_BLOCKS)
    nk = tk_all // tk
    qt = q.reshape(b, t, DA_HEADS, 2 * DA_DH).transpose(0, 2, 3, 1)
    k5 = (jnp.concatenate([k, ck], axis=1).reshape(b, nk, tk, DA_HEADS, 2 * DA_DH)
          .transpose(0, 3, 1, 2, 4))
    vt5 = (jnp.concatenate([v, cv], axis=1).reshape(b, nk, tk, DA_HEADS, DA_DV)
           .transpose(0, 3, 1, 4, 2))
    vt5 = jnp.concatenate([vt5, jnp.ones((b, DA_HEADS, nk, BF16_ROWS, tk), BF16)], axis=3)
    lams = [a[layer].reshape(1, DA_DH) for a in (lambda_q1, lambda_k1, lambda_q2, lambda_k2)]
    y_da = _da_call(lams, da_head_norm[layer].reshape(1, DA_DV), qt, k5, vt5, lam_init)

    x1 = _merge_call(x, gt1, y_da, o_f, o_b, gout, mrg,
                     gla_head_norm[layer].reshape(1, GLA_DV), post_norm1[layer].reshape(1, d),
                     w_branch_da[layer].astype(BF16), w_branch_gla[layer].astype(BF16),
                     w_out[layer].astype(BF16))
    return _mlp_call(x1, sh2, sc2, gt2, pre_norm2[layer].reshape(1, d), post_norm2[layer].reshape(1, d),
                     w_ff1[layer].astype(BF16), w_ff2[layer].astype(BF16))
```

```python
import functools
import math

import numpy as np
import jax
import jax.numpy as jnp
from jax import lax
from jax.experimental import pallas as pl
from jax.experimental.pallas import tpu as pltpu

F32 = jnp.float32
BF16 = jnp.bfloat16

GRID_W = 64
DA_HEADS = 8
DA_DH = 64
DA_DV = 2 * DA_DH
GLA_HEADS = 4
GLA_DK = 128
GLA_DV = 256
GLA_GATE_RANK = 16
GLA_GATE_NORM = 16.0
ROPE_THETA = 10000.0
ROPE_AXIS_DIM = DA_DH // 2
ROPE_HALF = ROPE_AXIS_DIM // 2
N_MOD = 6
EPS = 1e-6
NEG_BIG = -1e30

LANES = 128
SUBLANES = 8
LOG2_E = math.log2(math.e)
VMEM_LIMIT = 56 * 1024 * 1024

GLA_CHUNK = 128
GLA_HEAD_BLOCK = 4
DA_KEY_BLOCKS = (1024, 512, 256, 128)
DA_QUERY_BLOCKS = (512, 256, 128)
DA_BOUND_LIMIT = 50.0
DA_BOUND_SLACK = 1.01
LOW_PAD = LANES


def _nt(a, b):
    return lax.dot_general(a, b, (((1,), (1,)), ((), ())), preferred_element_type=F32)


def _tn(a, b):
    return lax.dot_general(a, b, (((0,), (0,)), ((), ())), preferred_element_type=F32)


def _mm(a, b):
    return jnp.dot(a, b, preferred_element_type=F32)


def _rms(x, gain):
    return x * lax.rsqrt(jnp.mean(x * x, axis=-1, keepdims=True) + EPS) * gain


def _params(*sem):
    return pltpu.CompilerParams(dimension_semantics=sem, vmem_limit_bytes=VMEM_LIMIT)


def _resident(shape):
    nd = len(shape)
    return pl.BlockSpec(shape, lambda *_: (0,) * nd, pipeline_mode=pl.Buffered(1))


def _largest_divisor(n, candidates):
    for cand in candidates:
        if n % cand == 0:
            return cand
    raise ValueError(f"no tile in {candidates} divides {n}")


def _mod_kernel(c_ref, w_ref, b_ref, o_ref):
    c = c_ref[...]
    s = (c * jax.nn.sigmoid(c)).astype(BF16)
    o_ref[...] = _mm(s, w_ref[...].astype(BF16)) + b_ref[...]


def _mod_call(cc, w_mod, b_mod):
    rows, d = cc.shape
    n = w_mod.shape[1]
    bn = _largest_divisor(n, (1024, 512, 256, 128))
    return pl.pallas_call(
        _mod_kernel,
        out_shape=jax.ShapeDtypeStruct((rows, n), F32),
        grid=(n // bn,),
        in_specs=[pl.BlockSpec((rows, d), lambda j: (0, 0)),
                  pl.BlockSpec((d, bn), lambda j: (0, j)),
                  pl.BlockSpec((1, bn), lambda j: (0, j))],
        out_specs=pl.BlockSpec((rows, bn), lambda j: (0, j)),
        compiler_params=_params("arbitrary"),
        name="mod",
    )(cc, w_mod, b_mod.reshape(1, n))


_OFF_Q = 0
_OFF_K = _OFF_Q + DA_HEADS * 2 * DA_DH
_OFF_V = _OFF_K + DA_HEADS * 2 * DA_DH
_OFF_GQ = _OFF_V + DA_HEADS * DA_DV
_OFF_GK = _OFF_GQ + GLA_HEADS * GLA_DK
_OFF_GV = _OFF_GK + GLA_HEADS * GLA_DK
_OFF_LOW = _OFF_GV + GLA_HEADS * GLA_DV
_OFF_GOUT = _OFF_LOW + LOW_PAD
_OFF_MERGE = _OFF_GOUT + GLA_HEADS * GLA_DV
_N_GATE = 2 * GLA_HEADS * GLA_DK


def _proj_kernel(x_ref, sh_ref, sc_ref, gain_ref, w_ref, wup_ref, bup_ref, cos_ref, sa_ref, sb_ref,
                 q_ref, k_ref, v_ref, gq_ref, gk_ref, gv_ref, gate_ref, gout_ref, mrg_ref):
    d = x_ref.shape[-1]
    h = (_rms(x_ref[...], gain_ref[...]) * (1.0 + sc_ref[...]) + sh_ref[...]).astype(BF16)
    cos, sa, sb = cos_ref[...], sa_ref[...], sb_ref[...]

    def roped(off, scale):
        acc = _mm(h, w_ref[:, off:off + d])
        for s in range(d // LANES):
            a = acc[:, s * LANES:(s + 1) * LANES]
            r = (a * cos + pltpu.roll(a, ROPE_HALF, 1) * sa
                 + pltpu.roll(a, LANES - ROPE_HALF, 1) * sb)
            yield s, r * scale

    for hd, slab in roped(_OFF_Q, DA_DH ** -0.5 * LOG2_E):
        q_ref[hd] = slab.T.astype(q_ref.dtype)
    for hd, slab in roped(_OFF_K, 1.0):
        k_ref[hd] = slab.astype(k_ref.dtype)
    v = _mm(h, w_ref[:, _OFF_V:_OFF_GQ])
    for hd in range(DA_HEADS):
        v_ref[hd] = v[:, hd * DA_DV:(hd + 1) * DA_DV].T.astype(v_ref.dtype)
    gq_ref[...] = (_mm(h, w_ref[:, _OFF_GQ:_OFF_GK]) * (GLA_DK ** -0.5)).astype(gq_ref.dtype)
    gk_ref[...] = _mm(h, w_ref[:, _OFF_GK:_OFF_GV]).astype(gk_ref.dtype)
    gv_ref[...] = _mm(h, w_ref[:, _OFF_GV:_OFF_LOW]).astype(gv_ref.dtype)
    low = _mm(h, w_ref[:, _OFF_LOW:_OFF_GOUT]).astype(BF16)
    z = _mm(low, wup_ref[...]) + bup_ref[...]
    log_sig = jnp.minimum(z, 0.0) - jnp.log1p(jnp.exp(-jnp.abs(z)))
    gate_ref[...] = log_sig / GLA_GATE_NORM
    go = _mm(h, w_ref[:, _OFF_GOUT:_OFF_MERGE])
    gout_ref[...] = (go * jax.nn.sigmoid(go)).astype(gout_ref.dtype)
    mg = _mm(h, w_ref[:, _OFF_MERGE:_OFF_MERGE + 2 * d])
    mrg_ref[...] = jax.nn.sigmoid(mg).astype(mrg_ref.dtype)


def _proj_call(x, shift, scale, gain, w_packed, w_up, b_up, cos, sa, sb):
    b, t, d = x.shape
    tm = _largest_divisor(t, (256, 128))
    tk = _da_key_block(t)
    per_block = tk // tm
    row = lambda width: pl.BlockSpec((None, tm, width), lambda bi, i: (bi, i, 0))
    vec = pl.BlockSpec((None, 1, d), lambda bi, i: (bi, 0, 0))
    tab = pl.BlockSpec((tm, LANES), lambda bi, i: (i, 0))
    widths = (GLA_HEADS * GLA_DK, GLA_HEADS * GLA_DK, GLA_HEADS * GLA_DV,
              _N_GATE, GLA_HEADS * GLA_DV, 2 * d)
    dtypes = (BF16, BF16, BF16, F32, BF16, BF16)
    dq = 2 * DA_DH
    head_shapes = [jax.ShapeDtypeStruct((b, DA_HEADS, dq, t), BF16),
                   jax.ShapeDtypeStruct((b, DA_HEADS, t, dq), BF16),
                   jax.ShapeDtypeStruct((b, DA_HEADS, t // tk, DA_DV, tk), BF16)]
    head_specs = [pl.BlockSpec((None, DA_HEADS, dq, tm), lambda bi, i: (bi, 0, 0, i)),
                  pl.BlockSpec((None, DA_HEADS, tm, dq), lambda bi, i: (bi, 0, i, 0)),
                  pl.BlockSpec((None, DA_HEADS, None, DA_DV, tm),
                               lambda bi, i: (bi, 0, i // per_block, 0, i % per_block))]
    return pl.pallas_call(
        _proj_kernel,
        out_shape=head_shapes + [jax.ShapeDtypeStruct((b, t, w), dt) for w, dt in zip(widths, dtypes)],
        grid=(b, t // tm),
        in_specs=[row(d), vec, vec, _resident((1, d)), _resident(w_packed.shape),
                  _resident(w_up.shape), _resident(b_up.shape), tab, tab, tab],
        out_specs=head_specs + [row(w) for w in widths],
        compiler_params=_params("parallel", "parallel"),
        name="proj",
    )(x, shift, scale, gain, w_packed, w_up, b_up, cos, sa, sb)


def _rope_tables(t):
    pos = jnp.arange(t)
    inv = ROPE_THETA ** (-jnp.arange(0, ROPE_AXIS_DIM, 2, dtype=F32) / ROPE_AXIS_DIM)
    ang_r = (pos // GRID_W).astype(F32)[:, None] * inv
    ang_c = (pos % GRID_W).astype(F32)[:, None] * inv
    zero = jnp.zeros_like(ang_r)
    cos = jnp.concatenate([jnp.cos(ang_r)] * 2 + [jnp.cos(ang_c)] * 2, axis=-1)
    sa = jnp.concatenate([zero, jnp.sin(ang_r), zero, jnp.sin(ang_c)], axis=-1)
    sb = jnp.concatenate([-jnp.sin(ang_r), zero, -jnp.sin(ang_c), zero], axis=-1)
    rep = LANES // DA_DH
    return tuple(jnp.tile(a, (1, rep)) for a in (cos, sa, sb))


def _identity_tables(t):
    return jnp.ones((t, LANES), F32), jnp.zeros((t, LANES), F32), jnp.zeros((t, LANES), F32)


def _level_ref(b, s, row, reverse):
    c = b.shape[0]
    if 2 * s >= SUBLANES:
        pieces = []
        for blk in range(c // (2 * s)):
            r = blk * 2 * s + s - 1 + (1 if reverse else 0)
            pieces.append(jnp.broadcast_to(b[r:r + 1, :], (2 * s, b.shape[1])))
        return jnp.concatenate(pieces, axis=0)
    pos = row % (2 * s)
    mid = s if reverse else s - 1
    ref = b
    for p in range(2 * s):
        if p != mid:
            ref = jnp.where(pos == p, pltpu.roll(b, (p - mid) % c, 0), ref)
    return ref


def _split3(x):
    hi = x.astype(BF16)
    r = x - hi.astype(F32)
    mid = r.astype(BF16)
    lo = (r - mid.astype(F32)).astype(BF16)
    return hi, mid, lo


def _chunk_cumsum(g, reverse):
    c = g.shape[0]
    ri = lax.broadcasted_iota(jnp.int32, (c, c), 0)
    ci = lax.broadcasted_iota(jnp.int32, (c, c), 1)
    tri = jnp.where((ci >= ri) if reverse else (ci <= ri), 1.0, 0.0).astype(BF16)
    hi, mid, lo = _split3(g)
    return _mm(tri, lo) + _mm(tri, mid) + _mm(tri, hi)


def _gla_tables(c):
    levels = int(math.log2(c))
    row = np.arange(c)
    side = np.zeros((levels, 2, 2, c, GLA_DK), np.float32)
    same = np.zeros((levels, c, c), np.float32)
    for lvl in range(levels):
        s = 1 << lvl
        upper = (row % (2 * s)) >= s
        for rev in range(2):
            q_side = ~upper if rev else upper
            side[lvl, rev, 0] = np.where(q_side, 0.0, NEG_BIG)[:, None]
            side[lvl, rev, 1] = np.where(q_side, NEG_BIG, 0.0)[:, None]
        same[lvl] = (row[:, None] // (2 * s)) == (row[None, :] // (2 * s))
    return jnp.asarray(side), jnp.asarray(same)


def _gla_direction(q, k, v, b, st_ref, side_ref, same_ref, reverse):
    c, dk = b.shape
    row = lax.broadcasted_iota(jnp.int32, (c, dk), 0)
    total = b[0:1, :] if reverse else b[c - 1:c, :]
    qf = q.astype(F32)
    kf = k.astype(F32)
    rev = 1 if reverse else 0

    st = st_ref[...]
    out = _nt((qf * jnp.exp2(b)).astype(BF16), st.astype(BF16))

    ri = lax.broadcasted_iota(jnp.int32, (c, c), 0)
    ci = lax.broadcasted_iota(jnp.int32, (c, c), 1)
    a = jnp.where(ri == ci, _nt(q, k), 0.0)
    s = 1
    lvl = 0
    while s < c:
        t = b - _level_ref(b, s, row, reverse)
        ql = (qf * jnp.exp2(t + side_ref[lvl, rev, 0])).astype(BF16)
        kl = (kf * jnp.exp2(side_ref[lvl, rev, 1] - t)).astype(BF16)
        al = _nt(ql, kl)
        a = a + (al if 2 * s == c else al * same_ref[lvl])
        s *= 2
        lvl += 1
    out = out + _mm(a.astype(BF16), v)

    kd = (kf * jnp.exp2(total - b)).astype(BF16)
    st_ref[...] = st * jnp.exp2(total) + _tn(v, kd)
    return out


def _gla_kernel(qf_ref, kf_ref, vf_ref, gf_ref, qb_ref, kb_ref, vb_ref, gb_ref, sf0_ref, sb0_ref,
                side_ref, same_ref, of_ref, ob_ref, sf1_ref, sb1_ref, sf_ref, sb_ref):
    i = pl.program_id(2)
    heads = sf_ref.shape[0]

    @pl.when(i == 0)
    def _():
        sf_ref[...] = sf0_ref[...]
        sb_ref[...] = sb0_ref[...]

    bf = _chunk_cumsum(gf_ref[...] * LOG2_E, False)
    bb = _chunk_cumsum(gb_ref[...] * LOG2_E, True)
    for h in range(heads):
        ks = slice(h * GLA_DK, (h + 1) * GLA_DK)
        vs = slice(h * GLA_DV, (h + 1) * GLA_DV)
        of_ref[:, vs] = _gla_direction(qf_ref[:, ks], kf_ref[:, ks], vf_ref[:, vs], bf[:, ks],
                                       sf_ref.at[h], side_ref, same_ref, False)
        ob_ref[:, vs] = _gla_direction(qb_ref[:, ks], kb_ref[:, ks], vb_ref[:, vs], bb[:, ks],
                                       sb_ref.at[h], side_ref, same_ref, True)
    sf1_ref[...] = sf_ref[...]
    sb1_ref[...] = sb_ref[...]


def _gla_call(gq, gk, gv, gate, sf0, sb0):
    b, t, _ = gq.shape
    c = _largest_divisor(t, (GLA_CHUNK,))
    n = t // c
    hb = GLA_HEAD_BLOCK
    nhb = GLA_HEADS // hb
    fwd = lambda w, off=0: pl.BlockSpec((None, c, hb * w), lambda bi, h, i: (bi, i, h + off))
    bwd = lambda w, off=0: pl.BlockSpec((None, c, hb * w), lambda bi, h, i: (bi, n - 1 - i, h + off))
    state = pl.BlockSpec((None, hb, GLA_DV, GLA_DK), lambda bi, h, i: (bi, h, 0, 0))
    o_shape = jax.ShapeDtypeStruct((b, t, GLA_HEADS * GLA_DV), F32)
    s_shape = jax.ShapeDtypeStruct((b, GLA_HEADS, GLA_DV, GLA_DK), F32)
    side, same = _gla_tables(c)
    return pl.pallas_call(
        _gla_kernel,
        out_shape=[o_shape, o_shape, s_shape, s_shape],
        grid=(b, nhb, n),
        in_specs=[fwd(GLA_DK), fwd(GLA_DK), fwd(GLA_DV), fwd(GLA_DK),
                  bwd(GLA_DK), bwd(GLA_DK), bwd(GLA_DV), bwd(GLA_DK, nhb),
                  state, state, _resident(side.shape), _resident(same.shape)],
        out_specs=[fwd(GLA_DV), bwd(GLA_DV), state, state],
        scratch_shapes=[pltpu.VMEM((hb, GLA_DV, GLA_DK), F32), pltpu.VMEM((hb, GLA_DV, GLA_DK), F32)],
        compiler_params=_params("parallel", "parallel", "arbitrary"),
        name="gla",
    )(gq, gk, gv, gate, gq, gk, gv, gate, sf0, sb0, side, same)


def _da_kernel(lq1_ref, lk1_ref, lq2_ref, lk2_ref, gain_ref, qt_ref, k_ref, vt_ref, ck_ref, cvt_ref,
               y_ref, kmax_ref, *, lam_init):
    tq = qt_ref.shape[-1]
    nk, tk, dq = k_ref.shape
    nck = ck_ref.shape[0]
    qt = qt_ref[...]
    sub = lax.broadcasted_iota(jnp.int32, qt.shape, 0)
    zero = jnp.zeros_like(qt)
    qm = jnp.concatenate([jnp.where(sub < DA_DH, qt, zero), jnp.where(sub < DA_DH, zero, qt)], axis=1)

    @pl.when(pl.program_id(2) == 0)
    def _():
        di = lax.broadcasted_iota(jnp.int32, (dq, dq), 0)
        ci = lax.broadcasted_iota(jnp.int32, (dq, dq), 1)
        same_comp = jnp.where((di < DA_DH) == (ci < DA_DH), 1.0, 0.0).astype(BF16)

        def block_max(kb, mx):
            kf = kb.astype(F32)
            nsq = _mm((kf * kf).astype(BF16), same_comp)
            return jnp.maximum(mx, jnp.max(nsq.reshape(-1, SUBLANES, dq), axis=0))

        mx = lax.fori_loop(0, nk, lambda j, mx: block_max(k_ref[j], mx), jnp.zeros((SUBLANES, dq), F32))
        for j in range(nck):
            mx = block_max(ck_ref[j], mx)
        kmax_ref[...] = jnp.broadcast_to(jnp.max(mx, axis=0, keepdims=True), kmax_ref.shape)

    qf = qt.astype(F32)
    qsq = qf * qf
    kmsq = kmax_ref[...]
    bsq = jnp.concatenate([jnp.sum(qsq[:DA_DH], axis=0, keepdims=True) * kmsq[0:1, 0:1],
                           jnp.sum(qsq[DA_DH:], axis=0, keepdims=True) * kmsq[0:1, DA_DH:DA_DH + 1]],
                          axis=1)
    bound = jnp.sqrt(bsq) * DA_BOUND_SLACK
    in_range = jnp.max(bound) <= DA_BOUND_LIMIT

    def probs(kb, shift):
        s = _mm(kb, qm)
        p = jnp.exp2(s.reshape(-1, SUBLANES, s.shape[1]) - shift[None])
        return p.reshape(s.shape)

    def finish(acc, l):
        lam = (jnp.exp(jnp.sum(lq1_ref[...] * lk1_ref[...], keepdims=True))
               - jnp.exp(jnp.sum(lq2_ref[...] * lk2_ref[...], keepdims=True)) + lam_init)
        o = acc / l
        ot = o[:, :tq] - lam * o[:, tq:]
        y = _rms(ot.T, gain_ref[...]) * (1.0 - lam_init)
        y_ref[...] = y.astype(y_ref.dtype)

    @pl.when(in_range)
    def _():
        shift = jnp.broadcast_to(bound, (SUBLANES, 2 * tq))
        acc = jnp.zeros((DA_DV, 2 * tq), F32)
        l = jnp.zeros((1, 2 * tq), F32)
        blocks = [(k_ref[j], vt_ref[j]) for j in range(nk)] + [(ck_ref[j], cvt_ref[j]) for j in range(nck)]
        for kb, vb in blocks:
            p = probs(kb, shift)
            l = l + jnp.sum(p, axis=0, keepdims=True)
            acc = acc + _mm(vb, p.astype(BF16))
        finish(acc, l)

    @pl.when(jnp.logical_not(in_range))
    def _():
        def block(kb, vb, carry):
            m_prev, l, acc = carry
            s = _mm(kb, qm)
            m_new = jnp.maximum(m_prev, jnp.max(s, axis=0, keepdims=True))
            alpha = jnp.exp2(m_prev - m_new)
            p = jnp.exp2(s - m_new)
            return (m_new, alpha * l + jnp.sum(p, axis=0, keepdims=True),
                    alpha * acc + _mm(vb, p.astype(BF16)))

        carry = (jnp.full((1, 2 * tq), NEG_BIG, F32), jnp.zeros((1, 2 * tq), F32),
                 jnp.zeros((DA_DV, 2 * tq), F32))
        carry = lax.fori_loop(0, nk, lambda j, c: block(k_ref[j], vt_ref[j], c), carry)
        for j in range(nck):
            carry = block(ck_ref[j], cvt_ref[j], carry)
        finish(carry[2], carry[1])


def _da_key_block(t):
    return _largest_divisor(t, DA_KEY_BLOCKS)


def _da_call(lams, gain, qt, k5, vt5, ck5, cvt5, lam_init):
    b, h, dq, t = qt.shape
    tq = _largest_divisor(t, DA_QUERY_BLOCKS)
    small = pl.BlockSpec((1, DA_DH), lambda bi, hi, i: (0, 0))
    per_head = lambda a: pl.BlockSpec((None, None) + a.shape[2:], lambda bi, hi, i: (bi, hi, 0, 0, 0))
    return pl.pallas_call(
        functools.partial(_da_kernel, lam_init=lam_init),
        out_shape=jax.ShapeDtypeStruct((b, t, h * DA_DV), BF16),
        grid=(b, h, t // tq),
        in_specs=[small, small, small, small,
                  pl.BlockSpec((1, DA_DV), lambda bi, hi, i: (0, 0)),
                  pl.BlockSpec((None, None, dq, tq), lambda bi, hi, i: (bi, hi, 0, i)),
                  per_head(k5), per_head(vt5), per_head(ck5), per_head(cvt5)],
        out_specs=pl.BlockSpec((None, tq, DA_DV), lambda bi, hi, i: (bi, i, hi)),
        scratch_shapes=[pltpu.VMEM((SUBLANES, dq), F32)],
        compiler_params=_params("parallel", "parallel", "arbitrary"),
        name="da",
    )(*lams, gain, qt, k5, vt5, ck5, cvt5)


def _merge_kernel(x_ref, gt_ref, yda_ref, of_ref, ob_ref, gout_ref, mrg_ref, ggain_ref, pgain_ref,
                  wda_ref, wgla_ref, wout_ref, o_ref):
    d = x_ref.shape[-1]
    o_gla = of_ref[...] + ob_ref[...]
    ggain = ggain_ref[...]
    heads = [_rms(o_gla[:, h * GLA_DV:(h + 1) * GLA_DV], ggain) for h in range(GLA_HEADS)]
    y_gla = (jnp.concatenate(heads, axis=-1) * gout_ref[...].astype(F32)).astype(BF16)
    gates = mrg_ref[...].astype(F32)
    y = (gates[:, :d] * _mm(yda_ref[...], wda_ref[...])
         + gates[:, d:] * _mm(y_gla, wgla_ref[...]))
    z = _mm(y.astype(BF16), wout_ref[...])
    o_ref[...] = x_ref[...] + gt_ref[...] * _rms(z, pgain_ref[...])


def _merge_call(x, gt1, y_da, o_f, o_b, gout, mrg, ggain, pgain, w_da, w_gla, w_out):
    b, t, d = x.shape
    tm = _largest_divisor(t, (256, 128))
    row = lambda width: pl.BlockSpec((None, tm, width), lambda bi, i: (bi, i, 0))
    vec = pl.BlockSpec((None, 1, d), lambda bi, i: (bi, 0, 0))
    return pl.pallas_call(
        _merge_kernel,
        out_shape=jax.ShapeDtypeStruct((b, t, d), F32),
        grid=(b, t // tm),
        in_specs=[row(d), vec, row(d), row(d), row(d), row(d), row(2 * d),
                  _resident(ggain.shape), _resident(pgain.shape),
                  _resident(w_da.shape), _resident(w_gla.shape), _resident(w_out.shape)],
        out_specs=row(d),
        compiler_params=_params("parallel", "parallel"),
        name="merge",
    )(x, gt1, y_da, o_f, o_b, gout, mrg, ggain, pgain, w_da, w_gla, w_out)


def _mlp_kernel(x_ref, sh_ref, sc_ref, gt_ref, pre_ref, post_ref, w1_ref, w2_ref, o_ref):
    x = x_ref[...]
    h = (_rms(x, pre_ref[...]) * (1.0 + sc_ref[...]) + sh_ref[...]).astype(BF16)
    u = jnp.maximum(_mm(h, w1_ref[...]), 0.0)
    z = _mm((u * u).astype(BF16), w2_ref[...])
    o_ref[...] = x + gt_ref[...] * _rms(z, post_ref[...])


def _mlp_call(x, sh2, sc2, gt2, pre, post, w1, w2):
    b, t, d = x.shape
    tm = _largest_divisor(t, (256, 128))
    row = pl.BlockSpec((None, tm, d), lambda bi, i: (bi, i, 0))
    vec = pl.BlockSpec((None, 1, d), lambda bi, i: (bi, 0, 0))
    return pl.pallas_call(
        _mlp_kernel,
        out_shape=jax.ShapeDtypeStruct((b, t, d), F32),
        grid=(b, t // tm),
        in_specs=[row, vec, vec, vec, _resident(pre.shape), _resident(post.shape),
                  _resident(w1.shape), _resident(w2.shape)],
        out_specs=row,
        compiler_params=_params("parallel", "parallel"),
        name="mlp",
    )(x, sh2, sc2, gt2, pre, post, w1, w2)


def _pack_w_in(w):
    low0 = _OFF_LOW
    low1 = low0 + 2 * GLA_GATE_RANK
    pad = jnp.zeros((w.shape[0], LOW_PAD - 2 * GLA_GATE_RANK), w.dtype)
    return jnp.concatenate([w[:, :low0], w[:, low0:low1], pad, w[:, low1:]], axis=1).astype(BF16)


def _pack_gate_up(w_gate_up, b_gate_up):
    hk = GLA_HEADS * GLA_DK
    w = jnp.zeros((LOW_PAD, 2 * hk), F32)
    for z in range(2):
        w = w.at[z * GLA_GATE_RANK:(z + 1) * GLA_GATE_RANK, z * hk:(z + 1) * hk].set(w_gate_up[z])
    return w.astype(BF16), b_gate_up.reshape(1, 2 * hk).astype(F32)


def kernel(x, c, ctx, c_ctx, w_mod, b_mod, pre_norm1, w_in, w_gate_up, b_gate_up, lambda_q1, lambda_k1,
           lambda_q2, lambda_k2, da_head_norm, gla_head_norm, w_branch_da, w_branch_gla, w_out, post_norm1,
           pre_norm2, w_ff1, w_ff2, post_norm2):
    b, t, d = x.shape
    n_ctx = ctx.shape[1]
    assert w_in.shape[0] == 1, "single-layer block only"
    assert d == DA_HEADS * DA_DV == GLA_HEADS * GLA_DV and t % GRID_W == 0
    layer = 0
    lam_init = 0.8 - 0.6 * math.exp(-0.3 * layer)

    rows = -(-(b + 1) // SUBLANES) * SUBLANES
    cc = jnp.zeros((rows, d), F32).at[:b].set(c).at[b].set(c_ctx)
    mod = _mod_call(cc, w_mod[layer], b_mod[layer])
    lat = [mod[:b, i * d:(i + 1) * d].reshape(b, 1, d) for i in range(N_MOD)]
    cmod = [jnp.broadcast_to(mod[b, i * d:(i + 1) * d].reshape(1, 1, d), (b, 1, d)) for i in range(2)]
    sh1, sc1, gt1, sh2, sc2, gt2 = lat

    w_packed = _pack_w_in(w_in[layer])
    w_up, b_up = _pack_gate_up(w_gate_up[layer], b_gate_up[layer])
    gain1 = pre_norm1[layer].reshape(1, d)

    qt, k, vt5, gq, gk, gv, gate, gout, mrg = _proj_call(
        x, sh1, sc1, gain1, w_packed, w_up, b_up, *_rope_tables(t))
    _, ck, cvt5, _, cgk, cgv, cgate, _, _ = _proj_call(
        ctx, cmod[0], cmod[1], gain1, w_packed, w_up, b_up, *_identity_tables(n_ctx))

    zero_state = jnp.zeros((b, GLA_HEADS, GLA_DV, GLA_DK), F32)
    _, _, sf0, sb0 = _gla_call(jnp.zeros_like(cgk), cgk, cgv, cgate, zero_state, zero_state)
    o_f, o_b, _, _ = _gla_call(gq, gk, gv, gate, sf0, sb0)

    k5 = k.reshape(b, DA_HEADS, vt5.shape[2], vt5.shape[4], 2 * DA_DH)
    ck5 = ck.reshape(b, DA_HEADS, cvt5.shape[2], cvt5.shape[4], 2 * DA_DH)
    lams = [a[layer].reshape(1, DA_DH) for a in (lambda_q1, lambda_k1, lambda_q2, lambda_k2)]
    y_da = _da_call(lams, da_head_norm[layer].reshape(1, DA_DV), qt, k5, vt5, ck5, cvt5, lam_init)

    x1 = _merge_call(x, gt1, y_da, o_f, o_b, gout, mrg,
                     gla_head_norm[layer].reshape(1, GLA_DV), post_norm1[layer].reshape(1, d),
                     w_branch_da[layer].astype(BF16), w_branch_gla[layer].astype(BF16),
                     w_out[layer].astype(BF16))
    return _mlp_call(x1, sh2, sc2, gt2, pre_norm2[layer].reshape(1, d), post_norm2[layer].reshape(1, d),
                     w_ff1[layer].astype(BF16), w_ff2[layer].astype(BF16))
```

```python
import functools
import math

import numpy as np
import jax
import jax.numpy as jnp
from jax import lax
from jax.experimental import pallas as pl
from jax.experimental.pallas import tpu as pltpu

F32 = jnp.float32
BF16 = jnp.bfloat16

GRID_W = 64
DA_HEADS = 8
DA_DH = 64
DA_DV = 2 * DA_DH
GLA_HEADS = 4
GLA_DK = 128
GLA_DV = 256
GLA_GATE_RANK = 16
GLA_GATE_NORM = 16.0
ROPE_THETA = 10000.0
ROPE_AXIS_DIM = DA_DH // 2
ROPE_HALF = ROPE_AXIS_DIM // 2
N_MOD = 6
EPS = 1e-6
NEG_BIG = -1e30

LANES = 128
SUBLANES = 8
LOG2_E = math.log2(math.e)
VMEM_LIMIT = 56 * 1024 * 1024

ROW_TILES = (512, 256, 128)
GLA_CHUNK = 128
GLA_HEAD_BLOCK = 4
DA_KEY_BLOCKS = (1024, 512, 256, 128)
DA_QUERY_BLOCKS = (1024, 512, 256, 128)
DA_BOUND_LIMIT = 50.0
DA_BOUND_SLACK = 1.01
LOW_PAD = LANES


def _nt(a, b):
    return lax.dot_general(a, b, (((1,), (1,)), ((), ())), preferred_element_type=F32)


def _tn(a, b):
    return lax.dot_general(a, b, (((0,), (0,)), ((), ())), preferred_element_type=F32)


def _mm(a, b):
    return jnp.dot(a, b, preferred_element_type=F32)


def _rms(x, gain):
    return x * lax.rsqrt(jnp.mean(x * x, axis=-1, keepdims=True) + EPS) * gain


def _params(*sem):
    return pltpu.CompilerParams(dimension_semantics=sem, vmem_limit_bytes=VMEM_LIMIT)


def _resident(shape):
    nd = len(shape)
    return pl.BlockSpec(shape, lambda *_: (0,) * nd, pipeline_mode=pl.Buffered(1))


def _largest_divisor(n, candidates):
    for cand in candidates:
        if n % cand == 0:
            return cand
    raise ValueError(f"no tile in {candidates} divides {n}")


def _mod_kernel(c_ref, w_ref, b_ref, o_ref):
    c = c_ref[...]
    s = (c * jax.nn.sigmoid(c)).astype(BF16)
    o_ref[...] = _mm(s, w_ref[...].astype(BF16)) + b_ref[...]


def _mod_call(cc, w_mod, b_mod):
    rows, d = cc.shape
    n = w_mod.shape[1]
    bn = _largest_divisor(n, (1024, 512, 256, 128))
    return pl.pallas_call(
        _mod_kernel,
        out_shape=jax.ShapeDtypeStruct((rows, n), F32),
        grid=(n // bn,),
        in_specs=[pl.BlockSpec((rows, d), lambda j: (0, 0)),
                  pl.BlockSpec((d, bn), lambda j: (0, j)),
                  pl.BlockSpec((1, bn), lambda j: (0, j))],
        out_specs=pl.BlockSpec((rows, bn), lambda j: (0, j)),
        compiler_params=_params("arbitrary"),
        name="mod",
    )(cc, w_mod, b_mod.reshape(1, n))


_OFF_Q = 0
_OFF_K = _OFF_Q + DA_HEADS * 2 * DA_DH
_OFF_V = _OFF_K + DA_HEADS * 2 * DA_DH
_OFF_GQ = _OFF_V + DA_HEADS * DA_DV
_OFF_GK = _OFF_GQ + GLA_HEADS * GLA_DK
_OFF_GV = _OFF_GK + GLA_HEADS * GLA_DK
_OFF_LOW = _OFF_GV + GLA_HEADS * GLA_DV
_OFF_GOUT = _OFF_LOW + LOW_PAD
_OFF_MERGE = _OFF_GOUT + GLA_HEADS * GLA_DV
_N_GATE = 2 * GLA_HEADS * GLA_DK


def _proj_kernel(x_ref, sh_ref, sc_ref, gain_ref, w_ref, wup_ref, bup_ref, cos_ref, sa_ref, sb_ref,
                 q_ref, k_ref, v_ref, gq_ref, gk_ref, gv_ref, gate_ref, gout_ref, mrg_ref):
    d = x_ref.shape[-1]
    h = (_rms(x_ref[...], gain_ref[...]) * (1.0 + sc_ref[...]) + sh_ref[...]).astype(BF16)
    cos, sa, sb = cos_ref[...], sa_ref[...], sb_ref[...]

    def roped(off, scale):
        acc = _mm(h, w_ref[:, off:off + d])
        for s in range(d // LANES):
            a = acc[:, s * LANES:(s + 1) * LANES]
            r = (a * cos + pltpu.roll(a, ROPE_HALF, 1) * sa
                 + pltpu.roll(a, LANES - ROPE_HALF, 1) * sb)
            yield s, r * scale

    for hd, slab in roped(_OFF_Q, DA_DH ** -0.5 * LOG2_E):
        q_ref[hd] = slab.T.astype(q_ref.dtype)
    for hd, slab in roped(_OFF_K, 1.0):
        k_ref[hd] = slab.astype(k_ref.dtype)
    v = _mm(h, w_ref[:, _OFF_V:_OFF_GQ])
    for hd in range(DA_HEADS):
        v_ref[hd] = v[:, hd * DA_DV:(hd + 1) * DA_DV].T.astype(v_ref.dtype)
    gq_ref[...] = (_mm(h, w_ref[:, _OFF_GQ:_OFF_GK]) * (GLA_DK ** -0.5)).astype(gq_ref.dtype)
    gk_ref[...] = _mm(h, w_ref[:, _OFF_GK:_OFF_GV]).astype(gk_ref.dtype)
    gv_ref[...] = _mm(h, w_ref[:, _OFF_GV:_OFF_LOW]).astype(gv_ref.dtype)
    low = _mm(h, w_ref[:, _OFF_LOW:_OFF_GOUT]).astype(BF16)
    z = _mm(low, wup_ref[...]) + bup_ref[...]
    log_sig = jnp.minimum(z, 0.0) - jnp.log1p(jnp.exp(-jnp.abs(z)))
    gate_ref[...] = log_sig / GLA_GATE_NORM
    go = _mm(h, w_ref[:, _OFF_GOUT:_OFF_MERGE])
    gout_ref[...] = (go * jax.nn.sigmoid(go)).astype(gout_ref.dtype)
    mg = _mm(h, w_ref[:, _OFF_MERGE:_OFF_MERGE + 2 * d])
    mrg_ref[...] = jax.nn.sigmoid(mg).astype(mrg_ref.dtype)


def _proj_call(x, shift, scale, gain, w_packed, w_up, b_up, cos, sa, sb):
    b, t, d = x.shape
    tm = _largest_divisor(t, ROW_TILES)
    tk = _da_key_block(t)
    per_block = tk // tm
    row = lambda width: pl.BlockSpec((None, tm, width), lambda bi, i: (bi, i, 0))
    vec = pl.BlockSpec((None, 1, d), lambda bi, i: (bi, 0, 0))
    tab = pl.BlockSpec((tm, LANES), lambda bi, i: (i, 0))
    widths = (GLA_HEADS * GLA_DK, GLA_HEADS * GLA_DK, GLA_HEADS * GLA_DV,
              _N_GATE, GLA_HEADS * GLA_DV, 2 * d)
    dtypes = (BF16, BF16, BF16, F32, BF16, BF16)
    dq = 2 * DA_DH
    head_shapes = [jax.ShapeDtypeStruct((b, DA_HEADS, dq, t), BF16),
                   jax.ShapeDtypeStruct((b, DA_HEADS, t, dq), BF16),
                   jax.ShapeDtypeStruct((b, DA_HEADS, t // tk, DA_DV, tk), BF16)]
    head_specs = [pl.BlockSpec((None, DA_HEADS, dq, tm), lambda bi, i: (bi, 0, 0, i)),
                  pl.BlockSpec((None, DA_HEADS, tm, dq), lambda bi, i: (bi, 0, i, 0)),
                  pl.BlockSpec((None, DA_HEADS, None, DA_DV, tm),
                               lambda bi, i: (bi, 0, i // per_block, 0, i % per_block))]
    return pl.pallas_call(
        _proj_kernel,
        out_shape=head_shapes + [jax.ShapeDtypeStruct((b, t, w), dt) for w, dt in zip(widths, dtypes)],
        grid=(b, t // tm),
        in_specs=[row(d), vec, vec, _resident((1, d)), _resident(w_packed.shape),
                  _resident(w_up.shape), _resident(b_up.shape), tab, tab, tab],
        out_specs=head_specs + [row(w) for w in widths],
        compiler_params=_params("parallel", "parallel"),
        name="proj",
    )(x, shift, scale, gain, w_packed, w_up, b_up, cos, sa, sb)


def _rope_tables(t):
    pos = jnp.arange(t)
    inv = ROPE_THETA ** (-jnp.arange(0, ROPE_AXIS_DIM, 2, dtype=F32) / ROPE_AXIS_DIM)
    ang_r = (pos // GRID_W).astype(F32)[:, None] * inv
    ang_c = (pos % GRID_W).astype(F32)[:, None] * inv
    zero = jnp.zeros_like(ang_r)
    cos = jnp.concatenate([jnp.cos(ang_r)] * 2 + [jnp.cos(ang_c)] * 2, axis=-1)
    sa = jnp.concatenate([zero, jnp.sin(ang_r), zero, jnp.sin(ang_c)], axis=-1)
    sb = jnp.concatenate([-jnp.sin(ang_r), zero, -jnp.sin(ang_c), zero], axis=-1)
    rep = LANES // DA_DH
    return tuple(jnp.tile(a, (1, rep)) for a in (cos, sa, sb))


def _identity_tables(t):
    return jnp.ones((t, LANES), F32), jnp.zeros((t, LANES), F32), jnp.zeros((t, LANES), F32)


def _level_ref(b, s, row, reverse):
    c = b.shape[0]
    if 2 * s >= SUBLANES:
        pieces = []
        for blk in range(c // (2 * s)):
            r = blk * 2 * s + s - 1 + (1 if reverse else 0)
            pieces.append(jnp.broadcast_to(b[r:r + 1, :], (2 * s, b.shape[1])))
        return jnp.concatenate(pieces, axis=0)
    pos = row % (2 * s)
    mid = s if reverse else s - 1
    ref = b
    for p in range(2 * s):
        if p != mid:
            ref = jnp.where(pos == p, pltpu.roll(b, (p - mid) % c, 0), ref)
    return ref


def _split3(x):
    hi = x.astype(BF16)
    r = x - hi.astype(F32)
    mid = r.astype(BF16)
    lo = (r - mid.astype(F32)).astype(BF16)
    return hi, mid, lo


def _chunk_cumsum(g, reverse):
    c = g.shape[0]
    ri = lax.broadcasted_iota(jnp.int32, (c, c), 0)
    ci = lax.broadcasted_iota(jnp.int32, (c, c), 1)
    tri = jnp.where((ci >= ri) if reverse else (ci <= ri), 1.0, 0.0).astype(BF16)
    hi, mid, lo = _split3(g)
    return _mm(tri, lo) + _mm(tri, mid) + _mm(tri, hi)


def _gla_tables(c):
    levels = int(math.log2(c))
    row = np.arange(c)
    side = np.zeros((levels, 2, 2, c, GLA_DK), np.float32)
    same = np.zeros((levels, c, c), np.float32)
    for lvl in range(levels):
        s = 1 << lvl
        upper = (row % (2 * s)) >= s
        for rev in range(2):
            q_side = ~upper if rev else upper
            side[lvl, rev, 0] = np.where(q_side, 0.0, NEG_BIG)[:, None]
            side[lvl, rev, 1] = np.where(q_side, NEG_BIG, 0.0)[:, None]
        same[lvl] = (row[:, None] // (2 * s)) == (row[None, :] // (2 * s))
    return jnp.asarray(side), jnp.asarray(same)


def _gla_direction(q, k, v, b, st_ref, side_ref, same_ref, reverse):
    c, dk = b.shape
    row = lax.broadcasted_iota(jnp.int32, (c, dk), 0)
    total = b[0:1, :] if reverse else b[c - 1:c, :]
    qf = q.astype(F32)
    kf = k.astype(F32)
    rev = 1 if reverse else 0

    st = st_ref[...]
    out = _nt((qf * jnp.exp2(b)).astype(BF16), st.astype(BF16))

    ri = lax.broadcasted_iota(jnp.int32, (c, c), 0)
    ci = lax.broadcasted_iota(jnp.int32, (c, c), 1)
    a = jnp.where(ri == ci, _nt(q, k), 0.0)
    s = 1
    lvl = 0
    while s < c:
        t = b - _level_ref(b, s, row, reverse)
        ql = (qf * jnp.exp2(t + side_ref[lvl, rev, 0])).astype(BF16)
        kl = (kf * jnp.exp2(side_ref[lvl, rev, 1] - t)).astype(BF16)
        al = _nt(ql, kl)
        a = a + (al if 2 * s == c else al * same_ref[lvl])
        s *= 2
        lvl += 1
    out = out + _mm(a.astype(BF16), v)

    kd = (kf * jnp.exp2(total - b)).astype(BF16)
    st_ref[...] = st * jnp.exp2(total) + _tn(v, kd)
    return out


def _gla_kernel(qf_ref, kf_ref, vf_ref, gf_ref, qb_ref, kb_ref, vb_ref, gb_ref, sf0_ref, sb0_ref,
                side_ref, same_ref, of_ref, ob_ref, sf1_ref, sb1_ref, sf_ref, sb_ref):
    i = pl.program_id(2)
    heads = sf_ref.shape[0]

    @pl.when(i == 0)
    def _():
        sf_ref[...] = sf0_ref[...]
        sb_ref[...] = sb0_ref[...]

    bf = _chunk_cumsum(gf_ref[...] * LOG2_E, False)
    bb = _chunk_cumsum(gb_ref[...] * LOG2_E, True)
    for h in range(heads):
        ks = slice(h * GLA_DK, (h + 1) * GLA_DK)
        vs = slice(h * GLA_DV, (h + 1) * GLA_DV)
        of_ref[:, vs] = _gla_direction(qf_ref[:, ks], kf_ref[:, ks], vf_ref[:, vs], bf[:, ks],
                                       sf_ref.at[h], side_ref, same_ref, False).astype(of_ref.dtype)
        ob_ref[:, vs] = _gla_direction(qb_ref[:, ks], kb_ref[:, ks], vb_ref[:, vs], bb[:, ks],
                                       sb_ref.at[h], side_ref, same_ref, True).astype(ob_ref.dtype)
    sf1_ref[...] = sf_ref[...]
    sb1_ref[...] = sb_ref[...]


def _gla_call(gq, gk, gv, gate, sf0, sb0):
    b, t, _ = gq.shape
    c = _largest_divisor(t, (GLA_CHUNK,))
    n = t // c
    hb = GLA_HEAD_BLOCK
    nhb = GLA_HEADS // hb
    fwd = lambda w, off=0: pl.BlockSpec((None, c, hb * w), lambda bi, h, i: (bi, i, h + off))
    bwd = lambda w, off=0: pl.BlockSpec((None, c, hb * w), lambda bi, h, i: (bi, n - 1 - i, h + off))
    state = pl.BlockSpec((None, hb, GLA_DV, GLA_DK), lambda bi, h, i: (bi, h, 0, 0))
    o_shape = jax.ShapeDtypeStruct((b, t, GLA_HEADS * GLA_DV), BF16)
    s_shape = jax.ShapeDtypeStruct((b, GLA_HEADS, GLA_DV, GLA_DK), F32)
    side, same = _gla_tables(c)
    return pl.pallas_call(
        _gla_kernel,
        out_shape=[o_shape, o_shape, s_shape, s_shape],
        grid=(b, nhb, n),
        in_specs=[fwd(GLA_DK), fwd(GLA_DK), fwd(GLA_DV), fwd(GLA_DK),
                  bwd(GLA_DK), bwd(GLA_DK), bwd(GLA_DV), bwd(GLA_DK, nhb),
                  state, state, _resident(side.shape), _resident(same.shape)],
        out_specs=[fwd(GLA_DV), bwd(GLA_DV), state, state],
        scratch_shapes=[pltpu.VMEM((hb, GLA_DV, GLA_DK), F32), pltpu.VMEM((hb, GLA_DV, GLA_DK), F32)],
        compiler_params=_params("parallel", "parallel", "arbitrary"),
        name="gla",
    )(gq, gk, gv, gate, gq, gk, gv, gate, sf0, sb0, side, same)


def _da_kernel(lq1_ref, lk1_ref, lq2_ref, lk2_ref, gain_ref, qt_ref, k_ref, vt_ref, ck_ref, cvt_ref,
               y_ref, kmax_ref, *, lam_init):
    tq = qt_ref.shape[-1]
    nk, tk, dq = k_ref.shape
    nck = ck_ref.shape[0]
    qt = qt_ref[...]
    sub = lax.broadcasted_iota(jnp.int32, qt.shape, 0)
    zero = jnp.zeros_like(qt)
    qm = jnp.concatenate([jnp.where(sub < DA_DH, qt, zero), jnp.where(sub < DA_DH, zero, qt)], axis=1)

    @pl.when(pl.program_id(2) == 0)
    def _():
        di = lax.broadcasted_iota(jnp.int32, (dq, dq), 0)
        ci = lax.broadcasted_iota(jnp.int32, (dq, dq), 1)
        same_comp = jnp.where((di < DA_DH) == (ci < DA_DH), 1.0, 0.0).astype(BF16)

        def block_max(kb, mx):
            kf = kb.astype(F32)
            nsq = _mm((kf * kf).astype(BF16), same_comp)
            return jnp.maximum(mx, jnp.max(nsq.reshape(-1, SUBLANES, dq), axis=0))

        mx = lax.fori_loop(0, nk, lambda j, mx: block_max(k_ref[j], mx), jnp.zeros((SUBLANES, dq), F32))
        for j in range(nck):
            mx = block_max(ck_ref[j], mx)
        kmax_ref[...] = jnp.broadcast_to(jnp.max(mx, axis=0, keepdims=True), kmax_ref.shape)

    qf = qt.astype(F32)
    qsq = qf * qf
    kmsq = kmax_ref[...]
    bsq = jnp.concatenate([jnp.sum(qsq[:DA_DH], axis=0, keepdims=True) * kmsq[0:1, 0:1],
                           jnp.sum(qsq[DA_DH:], axis=0, keepdims=True) * kmsq[0:1, DA_DH:DA_DH + 1]],
                          axis=1)
    bound = jnp.sqrt(bsq) * DA_BOUND_SLACK
    in_range = jnp.max(bound) <= DA_BOUND_LIMIT

    def probs(kb, shift):
        s = _mm(kb, qm)
        p = jnp.exp2(s.reshape(-1, SUBLANES, s.shape[1]) - shift[None])
        return p.reshape(s.shape)

    def finish(acc, l):
        lam = (jnp.exp(jnp.sum(lq1_ref[...] * lk1_ref[...], keepdims=True))
               - jnp.exp(jnp.sum(lq2_ref[...] * lk2_ref[...], keepdims=True)) + lam_init)
        o = acc / l
        ot = o[:, :tq] - lam * o[:, tq:]
        y = _rms(ot.T, gain_ref[...]) * (1.0 - lam_init)
        y_ref[...] = y.astype(y_ref.dtype)

    @pl.when(in_range)
    def _():
        shift = jnp.broadcast_to(bound, (SUBLANES, 2 * tq))
        acc = jnp.zeros((DA_DV, 2 * tq), F32)
        l = jnp.zeros((1, 2 * tq), F32)
        blocks = [(k_ref[j], vt_ref[j]) for j in range(nk)] + [(ck_ref[j], cvt_ref[j]) for j in range(nck)]
        for kb, vb in blocks:
            p = probs(kb, shift)
            l = l + jnp.sum(p, axis=0, keepdims=True)
            acc = acc + _mm(vb, p.astype(BF16))
        finish(acc, l)

    @pl.when(jnp.logical_not(in_range))
    def _():
        def block(kb, vb, carry):
            m_prev, l, acc = carry
            s = _mm(kb, qm)
            m_new = jnp.maximum(m_prev, jnp.max(s, axis=0, keepdims=True))
            alpha = jnp.exp2(m_prev - m_new)
            p = jnp.exp2(s - m_new)
            return (m_new, alpha * l + jnp.sum(p, axis=0, keepdims=True),
                    alpha * acc + _mm(vb, p.astype(BF16)))

        carry = (jnp.full((1, 2 * tq), NEG_BIG, F32), jnp.zeros((1, 2 * tq), F32),
                 jnp.zeros((DA_DV, 2 * tq), F32))
        carry = lax.fori_loop(0, nk, lambda j, c: block(k_ref[j], vt_ref[j], c), carry)
        for j in range(nck):
            carry = block(ck_ref[j], cvt_ref[j], carry)
        finish(carry[2], carry[1])


def _da_key_block(t):
    return _largest_divisor(t, DA_KEY_BLOCKS)


def _da_call(lams, gain, qt, k5, vt5, ck5, cvt5, lam_init):
    b, h, dq, t = qt.shape
    tq = _largest_divisor(t, DA_QUERY_BLOCKS)
    small = pl.BlockSpec((1, DA_DH), lambda bi, hi, i: (0, 0))
    per_head = lambda a: pl.BlockSpec((None, None) + a.shape[2:], lambda bi, hi, i: (bi, hi, 0, 0, 0))
    return pl.pallas_call(
        functools.partial(_da_kernel, lam_init=lam_init),
        out_shape=jax.ShapeDtypeStruct((b, t, h * DA_DV), BF16),
        grid=(b, h, t // tq),
        in_specs=[small, small, small, small,
                  pl.BlockSpec((1, DA_DV), lambda bi, hi, i: (0, 0)),
                  pl.BlockSpec((None, None, dq, tq), lambda bi, hi, i: (bi, hi, 0, i)),
                  per_head(k5), per_head(vt5), per_head(ck5), per_head(cvt5)],
        out_specs=pl.BlockSpec((None, tq, DA_DV), lambda bi, hi, i: (bi, i, hi)),
        scratch_shapes=[pltpu.VMEM((SUBLANES, dq), F32)],
        compiler_params=_params("parallel", "parallel", "arbitrary"),
        name="da",
    )(*lams, gain, qt, k5, vt5, ck5, cvt5)


def _merge_kernel(x_ref, gt_ref, yda_ref, of_ref, ob_ref, gout_ref, mrg_ref, ggain_ref, pgain_ref,
                  wda_ref, wgla_ref, wout_ref, o_ref):
    d = x_ref.shape[-1]
    o_gla = of_ref[...].astype(F32) + ob_ref[...].astype(F32)
    ggain = ggain_ref[...]
    heads = [_rms(o_gla[:, h * GLA_DV:(h + 1) * GLA_DV], ggain) for h in range(GLA_HEADS)]
    y_gla = (jnp.concatenate(heads, axis=-1) * gout_ref[...].astype(F32)).astype(BF16)
    gates = mrg_ref[...].astype(F32)
    y = (gates[:, :d] * _mm(yda_ref[...], wda_ref[...])
         + gates[:, d:] * _mm(y_gla, wgla_ref[...]))
    z = _mm(y.astype(BF16), wout_ref[...])
    o_ref[...] = x_ref[...] + gt_ref[...] * _rms(z, pgain_ref[...])


def _merge_call(x, gt1, y_da, o_f, o_b, gout, mrg, ggain, pgain, w_da, w_gla, w_out):
    b, t, d = x.shape
    tm = _largest_divisor(t, ROW_TILES)
    row = lambda width: pl.BlockSpec((None, tm, width), lambda bi, i: (bi, i, 0))
    vec = pl.BlockSpec((None, 1, d), lambda bi, i: (bi, 0, 0))
    return pl.pallas_call(
        _merge_kernel,
        out_shape=jax.ShapeDtypeStruct((b, t, d), F32),
        grid=(b, t // tm),
        in_specs=[row(d), vec, row(d), row(d), row(d), row(d), row(2 * d),
                  _resident(ggain.shape), _resident(pgain.shape),
                  _resident(w_da.shape), _resident(w_gla.shape), _resident(w_out.shape)],
        out_specs=row(d),
        compiler_params=_params("parallel", "parallel"),
        name="merge",
    )(x, gt1, y_da, o_f, o_b, gout, mrg, ggain, pgain, w_da, w_gla, w_out)


def _mlp_kernel(x_ref, sh_ref, sc_ref, gt_ref, pre_ref, post_ref, w1_ref, w2_ref, o_ref):
    x = x_ref[...]
    h = (_rms(x, pre_ref[...]) * (1.0 + sc_ref[...]) + sh_ref[...]).astype(BF16)
    u = jnp.maximum(_mm(h, w1_ref[...]), 0.0)
    z = _mm((u * u).astype(BF16), w2_ref[...])
    o_ref[...] = x + gt_ref[...] * _rms(z, post_ref[...])


def _mlp_call(x, sh2, sc2, gt2, pre, post, w1, w2):
    b, t, d = x.shape
    tm = _largest_divisor(t, ROW_TILES)
    row = pl.BlockSpec((None, tm, d), lambda bi, i: (bi, i, 0))
    vec = pl.BlockSpec((None, 1, d), lambda bi, i: (bi, 0, 0))
    return pl.pallas_call(
        _mlp_kernel,
        out_shape=jax.ShapeDtypeStruct((b, t, d), F32),
        grid=(b, t // tm),
        in_specs=[row, vec, vec, vec, _resident(pre.shape), _resident(post.shape),
                  _resident(w1.shape), _resident(w2.shape)],
        out_specs=row,
        compiler_params=_params("parallel", "parallel"),
        name="mlp",
    )(x, sh2, sc2, gt2, pre, post, w1, w2)


def _pack_w_in(w):
    low0 = _OFF_LOW
    low1 = low0 + 2 * GLA_GATE_RANK
    pad = jnp.zeros((w.shape[0], LOW_PAD - 2 * GLA_GATE_RANK), w.dtype)
    return jnp.concatenate([w[:, :low0], w[:, low0:low1], pad, w[:, low1:]], axis=1).astype(BF16)


def _pack_gate_up(w_gate_up, b_gate_up):
    hk = GLA_HEADS * GLA_DK
    w = jnp.zeros((LOW_PAD, 2 * hk), F32)
    for z in range(2):
        w = w.at[z * GLA_GATE_RANK:(z + 1) * GLA_GATE_RANK, z * hk:(z + 1) * hk].set(w_gate_up[z])
    return w.astype(BF16), b_gate_up.reshape(1, 2 * hk).astype(F32)


def kernel(x, c, ctx, c_ctx, w_mod, b_mod, pre_norm1, w_in, w_gate_up, b_gate_up, lambda_q1, lambda_k1,
           lambda_q2, lambda_k2, da_head_norm, gla_head_norm, w_branch_da, w_branch_gla, w_out, post_norm1,
           pre_norm2, w_ff1, w_ff2, post_norm2):
    b, t, d = x.shape
    n_ctx = ctx.shape[1]
    assert w_in.shape[0] == 1, "single-layer block only"
    assert d == DA_HEADS * DA_DV == GLA_HEADS * GLA_DV and t % GRID_W == 0
    layer = 0
    lam_init = 0.8 - 0.6 * math.exp(-0.3 * layer)

    rows = -(-(b + 1) // SUBLANES) * SUBLANES
    cc = jnp.zeros((rows, d), F32).at[:b].set(c).at[b].set(c_ctx)
    mod = _mod_call(cc, w_mod[layer], b_mod[layer])
    lat = [mod[:b, i * d:(i + 1) * d].reshape(b, 1, d) for i in range(N_MOD)]
    cmod = [jnp.broadcast_to(mod[b, i * d:(i + 1) * d].reshape(1, 1, d), (b, 1, d)) for i in range(2)]
    sh1, sc1, gt1, sh2, sc2, gt2 = lat

    w_packed = _pack_w_in(w_in[layer])
    w_up, b_up = _pack_gate_up(w_gate_up[layer], b_gate_up[layer])
    gain1 = pre_norm1[layer].reshape(1, d)

    qt, k, vt5, gq, gk, gv, gate, gout, mrg = _proj_call(
        x, sh1, sc1, gain1, w_packed, w_up, b_up, *_rope_tables(t))
    _, ck, cvt5, _, cgk, cgv, cgate, _, _ = _proj_call(
        ctx, cmod[0], cmod[1], gain1, w_packed, w_up, b_up, *_identity_tables(n_ctx))

    zero_state = jnp.zeros((b, GLA_HEADS, GLA_DV, GLA_DK), F32)
    _, _, sf0, sb0 = _gla_call(jnp.zeros_like(cgk), cgk, cgv, cgate, zero_state, zero_state)
    o_f, o_b, _, _ = _gla_call(gq, gk, gv, gate, sf0, sb0)

    k5 = k.reshape(b, DA_HEADS, vt5.shape[2], vt5.shape[4], 2 * DA_DH)
    ck5 = ck.reshape(b, DA_HEADS, cvt5.shape[2], cvt5.shape[4], 2 * DA_DH)
    lams = [a[layer].reshape(1, DA_DH) for a in (lambda_q1, lambda_k1, lambda_q2, lambda_k2)]
    y_da = _da_call(lams, da_head_norm[layer].reshape(1, DA_DV), qt, k5, vt5, ck5, cvt5, lam_init)

    x1 = _merge_call(x, gt1, y_da, o_f, o_b, gout, mrg,
                     gla_head_norm[layer].reshape(1, GLA_DV), post_norm1[layer].reshape(1, d),
                     w_branch_da[layer].astype(BF16), w_branch_gla[layer].astype(BF16),
                     w_out[layer].astype(BF16))
    return _mlp_call(x1, sh2, sc2, gt2, pre_norm2[layer].reshape(1, d), post_norm2[layer].reshape(1, d),
                     w_ff1[layer].astype(BF16), w_ff2[layer].astype(BF16))
```

```python
import functools
import math

import numpy as np
import jax
import jax.numpy as jnp
from jax import lax
from jax.experimental import pallas as pl
from jax.experimental.pallas import tpu as pltpu

F32 = jnp.float32
BF16 = jnp.bfloat16

GRID_W = 64
DA_HEADS = 8
DA_DH = 64
DA_DV = 2 * DA_DH
GLA_HEADS = 4
GLA_DK = 128
GLA_DV = 256
GLA_GATE_RANK = 16
GLA_GATE_NORM = 16.0
ROPE_THETA = 10000.0
ROPE_AXIS_DIM = DA_DH // 2
ROPE_HALF = ROPE_AXIS_DIM // 2
N_MOD = 6
EPS = 1e-6
NEG_BIG = -1e30

LANES = 128
SUBLANES = 8
LOG2_E = math.log2(math.e)
VMEM_LIMIT = 56 * 1024 * 1024

ROW_TILES = (512, 256, 128)
GLA_CHUNK = 128
GLA_BASE = 16
GLA_BASE_LIMIT = 100.0
GLA_HEAD_BLOCK = 4
DA_KEY_BLOCKS = (1024, 512, 256, 128)
DA_QUERY_BLOCKS = (1024, 512, 256, 128)
DA_BOUND_LIMIT = 50.0
DA_BOUND_SLACK = 1.01
LOW_PAD = LANES


def _nt(a, b):
    return lax.dot_general(a, b, (((1,), (1,)), ((), ())), preferred_element_type=F32)


def _tn(a, b):
    return lax.dot_general(a, b, (((0,), (0,)), ((), ())), preferred_element_type=F32)


def _mm(a, b):
    return jnp.dot(a, b, preferred_element_type=F32)


def _rms(x, gain):
    return x * lax.rsqrt(jnp.mean(x * x, axis=-1, keepdims=True) + EPS) * gain


def _params(*sem):
    return pltpu.CompilerParams(dimension_semantics=sem, vmem_limit_bytes=VMEM_LIMIT)


def _resident(shape):
    nd = len(shape)
    return pl.BlockSpec(shape, lambda *_: (0,) * nd, pipeline_mode=pl.Buffered(1))


def _largest_divisor(n, candidates):
    for cand in candidates:
        if n % cand == 0:
            return cand
    raise ValueError(f"no tile in {candidates} divides {n}")


def _mod_kernel(c_ref, w_ref, b_ref, o_ref):
    c = c_ref[...]
    s = (c * jax.nn.sigmoid(c)).astype(BF16)
    o_ref[...] = _mm(s, w_ref[...].astype(BF16)) + b_ref[...]


def _mod_call(cc, w_mod, b_mod):
    rows, d = cc.shape
    n = w_mod.shape[1]
    bn = _largest_divisor(n, (1024, 512, 256, 128))
    return pl.pallas_call(
        _mod_kernel,
        out_shape=jax.ShapeDtypeStruct((rows, n), F32),
        grid=(n // bn,),
        in_specs=[pl.BlockSpec((rows, d), lambda j: (0, 0)),
                  pl.BlockSpec((d, bn), lambda j: (0, j)),
                  pl.BlockSpec((1, bn), lambda j: (0, j))],
        out_specs=pl.BlockSpec((rows, bn), lambda j: (0, j)),
        compiler_params=_params("arbitrary"),
        name="mod",
    )(cc, w_mod, b_mod.reshape(1, n))


_OFF_Q = 0
_OFF_K = _OFF_Q + DA_HEADS * 2 * DA_DH
_OFF_V = _OFF_K + DA_HEADS * 2 * DA_DH
_OFF_GQ = _OFF_V + DA_HEADS * DA_DV
_OFF_GK = _OFF_GQ + GLA_HEADS * GLA_DK
_OFF_GV = _OFF_GK + GLA_HEADS * GLA_DK
_OFF_LOW = _OFF_GV + GLA_HEADS * GLA_DV
_OFF_GOUT = _OFF_LOW + LOW_PAD
_OFF_MERGE = _OFF_GOUT + GLA_HEADS * GLA_DV
_N_GATE = 2 * GLA_HEADS * GLA_DK


def _proj_kernel(x_ref, sh_ref, sc_ref, gain_ref, w_ref, wup_ref, bup_ref, cos_ref, sa_ref, sb_ref,
                 q_ref, k_ref, v_ref, gq_ref, gk_ref, gv_ref, gate_ref, gout_ref, mrg_ref):
    d = x_ref.shape[-1]
    h = (_rms(x_ref[...], gain_ref[...]) * (1.0 + sc_ref[...]) + sh_ref[...]).astype(BF16)
    cos, sa, sb = cos_ref[...], sa_ref[...], sb_ref[...]

    def roped(off, scale):
        acc = _mm(h, w_ref[:, off:off + d])
        for s in range(d // LANES):
            a = acc[:, s * LANES:(s + 1) * LANES]
            r = (a * cos + pltpu.roll(a, ROPE_HALF, 1) * sa
                 + pltpu.roll(a, LANES - ROPE_HALF, 1) * sb)
            yield s, r * scale

    for hd, slab in roped(_OFF_Q, DA_DH ** -0.5 * LOG2_E):
        q_ref[hd] = slab.T.astype(q_ref.dtype)
    for hd, slab in roped(_OFF_K, 1.0):
        k_ref[hd] = slab.astype(k_ref.dtype)
    v = _mm(h, w_ref[:, _OFF_V:_OFF_GQ])
    for hd in range(DA_HEADS):
        v_ref[hd] = v[:, hd * DA_DV:(hd + 1) * DA_DV].T.astype(v_ref.dtype)
    gq_ref[...] = (_mm(h, w_ref[:, _OFF_GQ:_OFF_GK]) * (GLA_DK ** -0.5)).astype(gq_ref.dtype)
    gk_ref[...] = _mm(h, w_ref[:, _OFF_GK:_OFF_GV]).astype(gk_ref.dtype)
    gv_ref[...] = _mm(h, w_ref[:, _OFF_GV:_OFF_LOW]).astype(gv_ref.dtype)
    low = _mm(h, w_ref[:, _OFF_LOW:_OFF_GOUT]).astype(BF16)
    z = _mm(low, wup_ref[...]) + bup_ref[...]
    log_sig = jnp.minimum(z, 0.0) - jnp.log1p(jnp.exp(-jnp.abs(z)))
    gate_ref[...] = log_sig / GLA_GATE_NORM
    go = _mm(h, w_ref[:, _OFF_GOUT:_OFF_MERGE])
    gout_ref[...] = (go * jax.nn.sigmoid(go)).astype(gout_ref.dtype)
    mg = _mm(h, w_ref[:, _OFF_MERGE:_OFF_MERGE + 2 * d])
    mrg_ref[...] = jax.nn.sigmoid(mg).astype(mrg_ref.dtype)


def _proj_call(x, shift, scale, gain, w_packed, w_up, b_up, cos, sa, sb):
    b, t, d = x.shape
    tm = _largest_divisor(t, ROW_TILES)
    tk = _da_key_block(t)
    per_block = tk // tm
    row = lambda width: pl.BlockSpec((None, tm, width), lambda bi, i: (bi, i, 0))
    vec = pl.BlockSpec((None, 1, d), lambda bi, i: (bi, 0, 0))
    tab = pl.BlockSpec((tm, LANES), lambda bi, i: (i, 0))
    widths = (GLA_HEADS * GLA_DK, GLA_HEADS * GLA_DK, GLA_HEADS * GLA_DV,
              _N_GATE, GLA_HEADS * GLA_DV, 2 * d)
    dtypes = (BF16, BF16, BF16, F32, BF16, BF16)
    dq = 2 * DA_DH
    head_shapes = [jax.ShapeDtypeStruct((b, DA_HEADS, dq, t), BF16),
                   jax.ShapeDtypeStruct((b, DA_HEADS, t, dq), BF16),
                   jax.ShapeDtypeStruct((b, DA_HEADS, t // tk, DA_DV, tk), BF16)]
    head_specs = [pl.BlockSpec((None, DA_HEADS, dq, tm), lambda bi, i: (bi, 0, 0, i)),
                  pl.BlockSpec((None, DA_HEADS, tm, dq), lambda bi, i: (bi, 0, i, 0)),
                  pl.BlockSpec((None, DA_HEADS, None, DA_DV, tm),
                               lambda bi, i: (bi, 0, i // per_block, 0, i % per_block))]
    return pl.pallas_call(
        _proj_kernel,
        out_shape=head_shapes + [jax.ShapeDtypeStruct((b, t, w), dt) for w, dt in zip(widths, dtypes)],
        grid=(b, t // tm),
        in_specs=[row(d), vec, vec, _resident((1, d)), _resident(w_packed.shape),
                  _resident(w_up.shape), _resident(b_up.shape), tab, tab, tab],
        out_specs=head_specs + [row(w) for w in widths],
        compiler_params=_params("parallel", "parallel"),
        name="proj",
    )(x, shift, scale, gain, w_packed, w_up, b_up, cos, sa, sb)


def _rope_tables(t):
    pos = jnp.arange(t)
    inv = ROPE_THETA ** (-jnp.arange(0, ROPE_AXIS_DIM, 2, dtype=F32) / ROPE_AXIS_DIM)
    ang_r = (pos // GRID_W).astype(F32)[:, None] * inv
    ang_c = (pos % GRID_W).astype(F32)[:, None] * inv
    zero = jnp.zeros_like(ang_r)
    cos = jnp.concatenate([jnp.cos(ang_r)] * 2 + [jnp.cos(ang_c)] * 2, axis=-1)
    sa = jnp.concatenate([zero, jnp.sin(ang_r), zero, jnp.sin(ang_c)], axis=-1)
    sb = jnp.concatenate([-jnp.sin(ang_r), zero, -jnp.sin(ang_c), zero], axis=-1)
    rep = LANES // DA_DH
    return tuple(jnp.tile(a, (1, rep)) for a in (cos, sa, sb))


def _identity_tables(t):
    return jnp.ones((t, LANES), F32), jnp.zeros((t, LANES), F32), jnp.zeros((t, LANES), F32)


def _level_ref(b, s, row, reverse):
    c = b.shape[0]
    if 2 * s >= SUBLANES:
        pieces = []
        for blk in range(c // (2 * s)):
            r = blk * 2 * s + s - 1 + (1 if reverse else 0)
            pieces.append(jnp.broadcast_to(b[r:r + 1, :], (2 * s, b.shape[1])))
        return jnp.concatenate(pieces, axis=0)
    pos = row % (2 * s)
    mid = s if reverse else s - 1
    ref = b
    for p in range(2 * s):
        if p != mid:
            ref = jnp.where(pos == p, pltpu.roll(b, (p - mid) % c, 0), ref)
    return ref


def _split3(x):
    hi = x.astype(BF16)
    r = x - hi.astype(F32)
    mid = r.astype(BF16)
    lo = (r - mid.astype(F32)).astype(BF16)
    return hi, mid, lo


def _chunk_cumsum(g, reverse):
    c = g.shape[0]
    ri = lax.broadcasted_iota(jnp.int32, (c, c), 0)
    ci = lax.broadcasted_iota(jnp.int32, (c, c), 1)
    tri = jnp.where((ci >= ri) if reverse else (ci <= ri), 1.0, 0.0).astype(BF16)
    hi, mid, lo = _split3(g)
    return _mm(tri, lo) + _mm(tri, mid) + _mm(tri, hi)


def _gla_tables(c):
    levels = int(math.log2(c))
    row = np.arange(c)
    side = np.zeros((levels, 2, 2, c, GLA_DK), np.float32)
    same = np.zeros((levels, c, c), np.float32)
    for lvl in range(levels):
        s = 1 << lvl
        upper = (row % (2 * s)) >= s
        for rev in range(2):
            q_side = ~upper if rev else upper
            side[lvl, rev, 0] = np.where(q_side, 0.0, NEG_BIG)[:, None]
            side[lvl, rev, 1] = np.where(q_side, NEG_BIG, 0.0)[:, None]
        same[lvl] = (row[:, None] // (2 * s)) == (row[None, :] // (2 * s))
    blk = row // GLA_BASE
    causal = np.stack([(blk[:, None] == blk[None, :]) & (row[None, :] <= row[:, None]),
                       (blk[:, None] == blk[None, :]) & (row[None, :] >= row[:, None])]).astype(np.float32)
    return jnp.asarray(side), jnp.asarray(same), jnp.asarray(causal)


def _base_offset(b, reverse):
    c, n = b.shape
    nblk = c // GLA_BASE
    pieces = []
    for i in range(nblk):
        r = (i + 1) * GLA_BASE if reverse else i * GLA_BASE - 1
        edge = b[r:r + 1, :] if 0 <= r < c else jnp.zeros((1, n), F32)
        pieces.append(jnp.broadcast_to(edge, (GLA_BASE, n)))
    return b - jnp.concatenate(pieces, axis=0)


def _gla_direction(q, k, v, b, st_ref, side_ref, same_ref, causal_ref, reverse, base_offset):
    c, dk = b.shape
    row = lax.broadcasted_iota(jnp.int32, (c, dk), 0)
    total = b[0:1, :] if reverse else b[c - 1:c, :]
    qf = q.astype(F32)
    kf = k.astype(F32)
    rev = 1 if reverse else 0

    st = st_ref[...]
    out = _nt((qf * jnp.exp2(b)).astype(BF16), st.astype(BF16))

    if base_offset is None:
        ri = lax.broadcasted_iota(jnp.int32, (c, c), 0)
        ci = lax.broadcasted_iota(jnp.int32, (c, c), 1)
        a = jnp.where(ri == ci, _nt(q, k), 0.0)
        s = 1
        lvl = 0
    else:
        qd = (qf * jnp.exp2(base_offset)).astype(BF16)
        kd = (kf * jnp.exp2(-base_offset)).astype(BF16)
        a = _nt(qd, kd) * causal_ref[rev]
        s = GLA_BASE
        lvl = int(math.log2(GLA_BASE))
    while s < c:
        t = b - _level_ref(b, s, row, reverse)
        ql = (qf * jnp.exp2(t + side_ref[lvl, rev, 0])).astype(BF16)
        kl = (kf * jnp.exp2(side_ref[lvl, rev, 1] - t)).astype(BF16)
        al = _nt(ql, kl)
        a = a + (al if 2 * s == c else al * same_ref[lvl])
        s *= 2
        lvl += 1
    out = out + _mm(a.astype(BF16), v)

    kd = (kf * jnp.exp2(total - b)).astype(BF16)
    st_ref[...] = st * jnp.exp2(total) + _tn(v, kd)
    return out


def _gla_kernel(qf_ref, kf_ref, vf_ref, gf_ref, qb_ref, kb_ref, vb_ref, gb_ref, sf0_ref, sb0_ref,
                side_ref, same_ref, causal_ref, of_ref, ob_ref, sf1_ref, sb1_ref, sf_ref, sb_ref):
    i = pl.program_id(2)
    heads = sf_ref.shape[0]

    @pl.when(i == 0)
    def _():
        sf_ref[...] = sf0_ref[...]
        sb_ref[...] = sb0_ref[...]

    gf = gf_ref[...] * LOG2_E
    gb = gb_ref[...] * LOG2_E
    mild = GLA_BASE * jnp.maximum(jnp.max(jnp.abs(gf)), jnp.max(jnp.abs(gb))) <= GLA_BASE_LIMIT

    def run(use_base):
        bf = _chunk_cumsum(gf, False)
        bb = _chunk_cumsum(gb, True)
        df = _base_offset(bf, False) if use_base else None
        db = _base_offset(bb, True) if use_base else None
        for h in range(heads):
            ks = slice(h * GLA_DK, (h + 1) * GLA_DK)
            vs = slice(h * GLA_DV, (h + 1) * GLA_DV)
            of_ref[:, vs] = _gla_direction(
                qf_ref[:, ks], kf_ref[:, ks], vf_ref[:, vs], bf[:, ks], sf_ref.at[h], side_ref, same_ref,
                causal_ref, False, df[:, ks] if use_base else None).astype(of_ref.dtype)
            ob_ref[:, vs] = _gla_direction(
                qb_ref[:, ks], kb_ref[:, ks], vb_ref[:, vs], bb[:, ks], sb_ref.at[h], side_ref, same_ref,
                causal_ref, True, db[:, ks] if use_base else None).astype(ob_ref.dtype)

    pl.when(mild)(lambda: run(True))
    pl.when(jnp.logical_not(mild))(lambda: run(False))
    sf1_ref[...] = sf_ref[...]
    sb1_ref[...] = sb_ref[...]


def _gla_call(gq, gk, gv, gate, sf0, sb0):
    b, t, _ = gq.shape
    c = _largest_divisor(t, (GLA_CHUNK,))
    n = t // c
    hb = GLA_HEAD_BLOCK
    nhb = GLA_HEADS // hb
    fwd = lambda w, off=0: pl.BlockSpec((None, c, hb * w), lambda bi, h, i: (bi, i, h + off))
    bwd = lambda w, off=0: pl.BlockSpec((None, c, hb * w), lambda bi, h, i: (bi, n - 1 - i, h + off))
    state = pl.BlockSpec((None, hb, GLA_DV, GLA_DK), lambda bi, h, i: (bi, h, 0, 0))
    o_shape = jax.ShapeDtypeStruct((b, t, GLA_HEADS * GLA_DV), BF16)
    s_shape = jax.ShapeDtypeStruct((b, GLA_HEADS, GLA_DV, GLA_DK), F32)
    side, same, causal = _gla_tables(c)
    return pl.pallas_call(
        _gla_kernel,
        out_shape=[o_shape, o_shape, s_shape, s_shape],
        grid=(b, nhb, n),
        in_specs=[fwd(GLA_DK), fwd(GLA_DK), fwd(GLA_DV), fwd(GLA_DK),
                  bwd(GLA_DK), bwd(GLA_DK), bwd(GLA_DV), bwd(GLA_DK, nhb),
                  state, state, _resident(side.shape), _resident(same.shape), _resident(causal.shape)],
        out_specs=[fwd(GLA_DV), bwd(GLA_DV), state, state],
        scratch_shapes=[pltpu.VMEM((hb, GLA_DV, GLA_DK), F32), pltpu.VMEM((hb, GLA_DV, GLA_DK), F32)],
        compiler_params=_params("parallel", "parallel", "arbitrary"),
        name="gla",
    )(gq, gk, gv, gate, gq, gk, gv, gate, sf0, sb0, side, same, causal)


def _da_kernel(lq1_ref, lk1_ref, lq2_ref, lk2_ref, gain_ref, qt_ref, k_ref, vt_ref, ck_ref, cvt_ref,
               y_ref, kmax_ref, *, lam_init):
    tq = qt_ref.shape[-1]
    nk, tk, dq = k_ref.shape
    nck = ck_ref.shape[0]
    qt = qt_ref[...]
    sub = lax.broadcasted_iota(jnp.int32, qt.shape, 0)
    zero = jnp.zeros_like(qt)
    qm = jnp.concatenate([jnp.where(sub < DA_DH, qt, zero), jnp.where(sub < DA_DH, zero, qt)], axis=1)

    @pl.when(pl.program_id(2) == 0)
    def _():
        di = lax.broadcasted_iota(jnp.int32, (dq, dq), 0)
        ci = lax.broadcasted_iota(jnp.int32, (dq, dq), 1)
        same_comp = jnp.where((di < DA_DH) == (ci < DA_DH), 1.0, 0.0).astype(BF16)

        def block_max(kb, mx):
            kf = kb.astype(F32)
            nsq = _mm((kf * kf).astype(BF16), same_comp)
            return jnp.maximum(mx, jnp.max(nsq.reshape(-1, SUBLANES, dq), axis=0))

        mx = lax.fori_loop(0, nk, lambda j, mx: block_max(k_ref[j], mx), jnp.zeros((SUBLANES, dq), F32))
        for j in range(nck):
            mx = block_max(ck_ref[j], mx)
        kmax_ref[...] = jnp.broadcast_to(jnp.max(mx, axis=0, keepdims=True), kmax_ref.shape)

    qf = qt.astype(F32)
    qsq = qf * qf
    kmsq = kmax_ref[...]
    bsq = jnp.concatenate([jnp.sum(qsq[:DA_DH], axis=0, keepdims=True) * kmsq[0:1, 0:1],
                           jnp.sum(qsq[DA_DH:], axis=0, keepdims=True) * kmsq[0:1, DA_DH:DA_DH + 1]],
                          axis=1)
    bound = jnp.sqrt(bsq) * DA_BOUND_SLACK
    in_range = jnp.max(bound) <= DA_BOUND_LIMIT

    def probs(kb, shift):
        s = _mm(kb, qm)
        p = jnp.exp2(s.reshape(-1, SUBLANES, s.shape[1]) - shift[None])
        return p.reshape(s.shape)

    def finish(acc, l):
        lam = (jnp.exp(jnp.sum(lq1_ref[...] * lk1_ref[...], keepdims=True))
               - jnp.exp(jnp.sum(lq2_ref[...] * lk2_ref[...], keepdims=True)) + lam_init)
        o = acc / l
        ot = o[:, :tq] - lam * o[:, tq:]
        y = _rms(ot.T, gain_ref[...]) * (1.0 - lam_init)
        y_ref[...] = y.astype(y_ref.dtype)

    @pl.when(in_range)
    def _():
        shift = jnp.broadcast_to(bound, (SUBLANES, 2 * tq))
        acc = jnp.zeros((DA_DV, 2 * tq), F32)
        l = jnp.zeros((1, 2 * tq), F32)
        blocks = [(k_ref[j], vt_ref[j]) for j in range(nk)] + [(ck_ref[j], cvt_ref[j]) for j in range(nck)]
        for kb, vb in blocks:
            p = probs(kb, shift)
            l = l + jnp.sum(p, axis=0, keepdims=True)
            acc = acc + _mm(vb, p.astype(BF16))
        finish(acc, l)

    @pl.when(jnp.logical_not(in_range))
    def _():
        def block(kb, vb, carry):
            m_prev, l, acc = carry
            s = _mm(kb, qm)
            m_new = jnp.maximum(m_prev, jnp.max(s, axis=0, keepdims=True))
            alpha = jnp.exp2(m_prev - m_new)
            p = jnp.exp2(s - m_new)
            return (m_new, alpha * l + jnp.sum(p, axis=0, keepdims=True),
                    alpha * acc + _mm(vb, p.astype(BF16)))

        carry = (jnp.full((1, 2 * tq), NEG_BIG, F32), jnp.zeros((1, 2 * tq), F32),
                 jnp.zeros((DA_DV, 2 * tq), F32))
        carry = lax.fori_loop(0, nk, lambda j, c: block(k_ref[j], vt_ref[j], c), carry)
        for j in range(nck):
            carry = block(ck_ref[j], cvt_ref[j], carry)
        finish(carry[2], carry[1])


def _da_key_block(t):
    return _largest_divisor(t, DA_KEY_BLOCKS)


def _da_call(lams, gain, qt, k5, vt5, ck5, cvt5, lam_init):
    b, h, dq, t = qt.shape
    tq = _largest_divisor(t, DA_QUERY_BLOCKS)
    small = pl.BlockSpec((1, DA_DH), lambda bi, hi, i: (0, 0))
    per_head = lambda a: pl.BlockSpec((None, None) + a.shape[2:], lambda bi, hi, i: (bi, hi, 0, 0, 0))
    return pl.pallas_call(
        functools.partial(_da_kernel, lam_init=lam_init),
        out_shape=jax.ShapeDtypeStruct((b, t, h * DA_DV), BF16),
        grid=(b, h, t // tq),
        in_specs=[small, small, small, small,
                  pl.BlockSpec((1, DA_DV), lambda bi, hi, i: (0, 0)),
                  pl.BlockSpec((None, None, dq, tq), lambda bi, hi, i: (bi, hi, 0, i)),
                  per_head(k5), per_head(vt5), per_head(ck5), per_head(cvt5)],
        out_specs=pl.BlockSpec((None, tq, DA_DV), lambda bi, hi, i: (bi, i, hi)),
        scratch_shapes=[pltpu.VMEM((SUBLANES, dq), F32)],
        compiler_params=_params("parallel", "parallel", "arbitrary"),
        name="da",
    )(*lams, gain, qt, k5, vt5, ck5, cvt5)


def _merge_kernel(x_ref, gt_ref, yda_ref, of_ref, ob_ref, gout_ref, mrg_ref, ggain_ref, pgain_ref,
                  wda_ref, wgla_ref, wout_ref, o_ref):
    d = x_ref.shape[-1]
    o_gla = of_ref[...].astype(F32) + ob_ref[...].astype(F32)
    ggain = ggain_ref[...]
    heads = [_rms(o_gla[:, h * GLA_DV:(h + 1) * GLA_DV], ggain) for h in range(GLA_HEADS)]
    y_gla = (jnp.concatenate(heads, axis=-1) * gout_ref[...].astype(F32)).astype(BF16)
    gates = mrg_ref[...].astype(F32)
    y = (gates[:, :d] * _mm(yda_ref[...], wda_ref[...])
         + gates[:, d:] * _mm(y_gla, wgla_ref[...]))
    z = _mm(y.astype(BF16), wout_ref[...])
    o_ref[...] = x_ref[...] + gt_ref[...] * _rms(z, pgain_ref[...])


def _merge_call(x, gt1, y_da, o_f, o_b, gout, mrg, ggain, pgain, w_da, w_gla, w_out):
    b, t, d = x.shape
    tm = _largest_divisor(t, ROW_TILES)
    row = lambda width: pl.BlockSpec((None, tm, width), lambda bi, i: (bi, i, 0))
    vec = pl.BlockSpec((None, 1, d), lambda bi, i: (bi, 0, 0))
    return pl.pallas_call(
        _merge_kernel,
        out_shape=jax.ShapeDtypeStruct((b, t, d), F32),
        grid=(b, t // tm),
        in_specs=[row(d), vec, row(d), row(d), row(d), row(d), row(2 * d),
                  _resident(ggain.shape), _resident(pgain.shape),
                  _resident(w_da.shape), _resident(w_gla.shape), _resident(w_out.shape)],
        out_specs=row(d),
        compiler_params=_params("parallel", "parallel"),
        name="merge",
    )(x, gt1, y_da, o_f, o_b, gout, mrg, ggain, pgain, w_da, w_gla, w_out)


def _mlp_kernel(x_ref, sh_ref, sc_ref, gt_ref, pre_ref, post_ref, w1_ref, w2_ref, o_ref):
    x = x_ref[...]
    h = (_rms(x, pre_ref[...]) * (1.0 + sc_ref[...]) + sh_ref[...]).astype(BF16)
    u = jnp.maximum(_mm(h, w1_ref[...]), 0.0)
    z = _mm((u * u).astype(BF16), w2_ref[...])
    o_ref[...] = x + gt_ref[...] * _rms(z, post_ref[...])


def _mlp_call(x, sh2, sc2, gt2, pre, post, w1, w2):
    b, t, d = x.shape
    tm = _largest_divisor(t, ROW_TILES)
    row = pl.BlockSpec((None, tm, d), lambda bi, i: (bi, i, 0))
    vec = pl.BlockSpec((None, 1, d), lambda bi, i: (bi, 0, 0))
    return pl.pallas_call(
        _mlp_kernel,
        out_shape=jax.ShapeDtypeStruct((b, t, d), F32),
        grid=(b, t // tm),
        in_specs=[row, vec, vec, vec, _resident(pre.shape), _resident(post.shape),
                  _resident(w1.shape), _resident(w2.shape)],
        out_specs=row,
        compiler_params=_params("parallel", "parallel"),
        name="mlp",
    )(x, sh2, sc2, gt2, pre, post, w1, w2)


def _pack_w_in(w):
    low0 = _OFF_LOW
    low1 = low0 + 2 * GLA_GATE_RANK
    pad = jnp.zeros((w.shape[0], LOW_PAD - 2 * GLA_GATE_RANK), w.dtype)
    return jnp.concatenate([w[:, :low0], w[:, low0:low1], pad, w[:, low1:]], axis=1).astype(BF16)


def _pack_gate_up(w_gate_up, b_gate_up):
    hk = GLA_HEADS * GLA_DK
    w = jnp.zeros((LOW_PAD, 2 * hk), F32)
    for z in range(2):
        w = w.at[z * GLA_GATE_RANK:(z + 1) * GLA_GATE_RANK, z * hk:(z + 1) * hk].set(w_gate_up[z])
    return w.astype(BF16), b_gate_up.reshape(1, 2 * hk).astype(F32)


def kernel(x, c, ctx, c_ctx, w_mod, b_mod, pre_norm1, w_in, w_gate_up, b_gate_up, lambda_q1, lambda_k1,
           lambda_q2, lambda_k2, da_head_norm, gla_head_norm, w_branch_da, w_branch_gla, w_out, post_norm1,
           pre_norm2, w_ff1, w_ff2, post_norm2):
    b, t, d = x.shape
    n_ctx = ctx.shape[1]
    assert w_in.shape[0] == 1, "single-layer block only"
    assert d == DA_HEADS * DA_DV == GLA_HEADS * GLA_DV and t % GRID_W == 0
    layer = 0
    lam_init = 0.8 - 0.6 * math.exp(-0.3 * layer)

    rows = -(-(b + 1) // SUBLANES) * SUBLANES
    cc = jnp.zeros((rows, d), F32).at[:b].set(c).at[b].set(c_ctx)
    mod = _mod_call(cc, w_mod[layer], b_mod[layer])
    lat = [mod[:b, i * d:(i + 1) * d].reshape(b, 1, d) for i in range(N_MOD)]
    cmod = [jnp.broadcast_to(mod[b, i * d:(i + 1) * d].reshape(1, 1, d), (b, 1, d)) for i in range(2)]
    sh1, sc1, gt1, sh2, sc2, gt2 = lat

    w_packed = _pack_w_in(w_in[layer])
    w_up, b_up = _pack_gate_up(w_gate_up[layer], b_gate_up[layer])
    gain1 = pre_norm1[layer].reshape(1, d)

    qt, k, vt5, gq, gk, gv, gate, gout, mrg = _proj_call(
        x, sh1, sc1, gain1, w_packed, w_up, b_up, *_rope_tables(t))
    _, ck, cvt5, _, cgk, cgv, cgate, _, _ = _proj_call(
        ctx, cmod[0], cmod[1], gain1, w_packed, w_up, b_up, *_identity_tables(n_ctx))

    zero_state = jnp.zeros((b, GLA_HEADS, GLA_DV, GLA_DK), F32)
    _, _, sf0, sb0 = _gla_call(jnp.zeros_like(cgk), cgk, cgv, cgate, zero_state, zero_state)
    o_f, o_b, _, _ = _gla_call(gq, gk, gv, gate, sf0, sb0)

    k5 = k.reshape(b, DA_HEADS, vt5.shape[2], vt5.shape[4], 2 * DA_DH)
    ck5 = ck.reshape(b, DA_HEADS, cvt5.shape[2], cvt5.shape[4], 2 * DA_DH)
    lams = [a[layer].reshape(1, DA_DH) for a in (lambda_q1, lambda_k1, lambda_q2, lambda_k2)]
    y_da = _da_call(lams, da_head_norm[layer].reshape(1, DA_DV), qt, k5, vt5, ck5, cvt5, lam_init)

    x1 = _merge_call(x, gt1, y_da, o_f, o_b, gout, mrg,
                     gla_head_norm[layer].reshape(1, GLA_DV), post_norm1[layer].reshape(1, d),
                     w_branch_da[layer].astype(BF16), w_branch_gla[layer].astype(BF16),
                     w_out[layer].astype(BF16))
    return _mlp_call(x1, sh2, sc2, gt2, pre_norm2[layer].reshape(1, d), post_norm2[layer].reshape(1, d),
                     w_ff1[layer].astype(BF16), w_ff2[layer].astype(BF16))
```

```python
import functools
import math

import numpy as np
import jax
import jax.numpy as jnp
from jax import lax
from jax.experimental import pallas as pl
from jax.experimental.pallas import tpu as pltpu

F32 = jnp.float32
BF16 = jnp.bfloat16

GRID_W = 64
DA_HEADS = 8
DA_DH = 64
DA_DV = 2 * DA_DH
GLA_HEADS = 4
GLA_DK = 128
GLA_DV = 256
GLA_GATE_RANK = 16
GLA_GATE_NORM = 16.0
ROPE_THETA = 10000.0
ROPE_AXIS_DIM = DA_DH // 2
ROPE_HALF = ROPE_AXIS_DIM // 2
N_MOD = 6
EPS = 1e-6
NEG_BIG = -1e30

LANES = 128
SUBLANES = 8
LOG2_E = math.log2(math.e)
VMEM_LIMIT = 56 * 1024 * 1024

ROW_TILES = (512, 256, 128)
GLA_CHUNK = 128
GLA_BASES = (128, 16)
GLA_BASE_LIMIT = 100.0
GLA_HEAD_BLOCK = 4
DA_KEY_BLOCKS = (1024, 512, 256, 128)
DA_QUERY_BLOCKS = (1024, 512, 256, 128)
DA_BOUND_LIMIT = 50.0
DA_BOUND_SLACK = 1.01
LOW_PAD = LANES


def _nt(a, b):
    return lax.dot_general(a, b, (((1,), (1,)), ((), ())), preferred_element_type=F32)


def _tn(a, b):
    return lax.dot_general(a, b, (((0,), (0,)), ((), ())), preferred_element_type=F32)


def _mm(a, b):
    return jnp.dot(a, b, preferred_element_type=F32)


def _rms(x, gain):
    return x * lax.rsqrt(jnp.mean(x * x, axis=-1, keepdims=True) + EPS) * gain


def _params(*sem):
    return pltpu.CompilerParams(dimension_semantics=sem, vmem_limit_bytes=VMEM_LIMIT)


def _resident(shape):
    nd = len(shape)
    return pl.BlockSpec(shape, lambda *_: (0,) * nd, pipeline_mode=pl.Buffered(1))


def _largest_divisor(n, candidates):
    for cand in candidates:
        if n % cand == 0:
            return cand
    raise ValueError(f"no tile in {candidates} divides {n}")


def _mod_kernel(c_ref, w_ref, b_ref, o_ref):
    c = c_ref[...]
    s = (c * jax.nn.sigmoid(c)).astype(BF16)
    o_ref[...] = _mm(s, w_ref[...].astype(BF16)) + b_ref[...]


def _mod_call(cc, w_mod, b_mod):
    rows, d = cc.shape
    n = w_mod.shape[1]
    bn = _largest_divisor(n, (1024, 512, 256, 128))
    return pl.pallas_call(
        _mod_kernel,
        out_shape=jax.ShapeDtypeStruct((rows, n), F32),
        grid=(n // bn,),
        in_specs=[pl.BlockSpec((rows, d), lambda j: (0, 0)),
                  pl.BlockSpec((d, bn), lambda j: (0, j)),
                  pl.BlockSpec((1, bn), lambda j: (0, j))],
        out_specs=pl.BlockSpec((rows, bn), lambda j: (0, j)),
        compiler_params=_params("arbitrary"),
        name="mod",
    )(cc, w_mod, b_mod.reshape(1, n))


_OFF_Q = 0
_OFF_K = _OFF_Q + DA_HEADS * 2 * DA_DH
_OFF_V = _OFF_K + DA_HEADS * 2 * DA_DH
_OFF_GQ = _OFF_V + DA_HEADS * DA_DV
_OFF_GK = _OFF_GQ + GLA_HEADS * GLA_DK
_OFF_GV = _OFF_GK + GLA_HEADS * GLA_DK
_OFF_LOW = _OFF_GV + GLA_HEADS * GLA_DV
_OFF_GOUT = _OFF_LOW + LOW_PAD
_OFF_MERGE = _OFF_GOUT + GLA_HEADS * GLA_DV
_N_GATE = 2 * GLA_HEADS * GLA_DK


def _proj_kernel(x_ref, sh_ref, sc_ref, gain_ref, w_ref, wup_ref, bup_ref, cos_ref, sa_ref, sb_ref,
                 q_ref, k_ref, v_ref, gq_ref, gk_ref, gv_ref, gate_ref, gout_ref, mrg_ref):
    d = x_ref.shape[-1]
    h = (_rms(x_ref[...], gain_ref[...]) * (1.0 + sc_ref[...]) + sh_ref[...]).astype(BF16)
    cos, sa, sb = cos_ref[...], sa_ref[...], sb_ref[...]

    def roped(off, scale):
        acc = _mm(h, w_ref[:, off:off + d])
        for s in range(d // LANES):
            a = acc[:, s * LANES:(s + 1) * LANES]
            r = (a * cos + pltpu.roll(a, ROPE_HALF, 1) * sa
                 + pltpu.roll(a, LANES - ROPE_HALF, 1) * sb)
            yield s, r * scale

    for hd, slab in roped(_OFF_Q, DA_DH ** -0.5 * LOG2_E):
        q_ref[hd] = slab.T.astype(q_ref.dtype)
    for hd, slab in roped(_OFF_K, 1.0):
        k_ref[hd] = slab.astype(k_ref.dtype)
    v = _mm(h, w_ref[:, _OFF_V:_OFF_GQ])
    for hd in range(DA_HEADS):
        v_ref[hd] = v[:, hd * DA_DV:(hd + 1) * DA_DV].T.astype(v_ref.dtype)
    gq_ref[...] = (_mm(h, w_ref[:, _OFF_GQ:_OFF_GK]) * (GLA_DK ** -0.5)).astype(gq_ref.dtype)
    gk_ref[...] = _mm(h, w_ref[:, _OFF_GK:_OFF_GV]).astype(gk_ref.dtype)
    gv_ref[...] = _mm(h, w_ref[:, _OFF_GV:_OFF_LOW]).astype(gv_ref.dtype)
    low = _mm(h, w_ref[:, _OFF_LOW:_OFF_GOUT]).astype(BF16)
    z = _mm(low, wup_ref[...]) + bup_ref[...]
    log_sig = jnp.minimum(z, 0.0) - jnp.log1p(jnp.exp(-jnp.abs(z)))
    gate_ref[...] = log_sig / GLA_GATE_NORM
    go = _mm(h, w_ref[:, _OFF_GOUT:_OFF_MERGE])
    gout_ref[...] = (go * jax.nn.sigmoid(go)).astype(gout_ref.dtype)
    mg = _mm(h, w_ref[:, _OFF_MERGE:_OFF_MERGE + 2 * d])
    mrg_ref[...] = jax.nn.sigmoid(mg).astype(mrg_ref.dtype)


def _proj_call(x, shift, scale, gain, w_packed, w_up, b_up, cos, sa, sb):
    b, t, d = x.shape
    tm = _largest_divisor(t, ROW_TILES)
    tk = _da_key_block(t)
    per_block = tk // tm
    row = lambda width: pl.BlockSpec((None, tm, width), lambda bi, i: (bi, i, 0))
    vec = pl.BlockSpec((None, 1, d), lambda bi, i: (bi, 0, 0))
    tab = pl.BlockSpec((tm, LANES), lambda bi, i: (i, 0))
    widths = (GLA_HEADS * GLA_DK, GLA_HEADS * GLA_DK, GLA_HEADS * GLA_DV,
              _N_GATE, GLA_HEADS * GLA_DV, 2 * d)
    dtypes = (BF16, BF16, BF16, F32, BF16, BF16)
    dq = 2 * DA_DH
    head_shapes = [jax.ShapeDtypeStruct((b, DA_HEADS, dq, t), BF16),
                   jax.ShapeDtypeStruct((b, DA_HEADS, t, dq), BF16),
                   jax.ShapeDtypeStruct((b, DA_HEADS, t // tk, DA_DV, tk), BF16)]
    head_specs = [pl.BlockSpec((None, DA_HEADS, dq, tm), lambda bi, i: (bi, 0, 0, i)),
                  pl.BlockSpec((None, DA_HEADS, tm, dq), lambda bi, i: (bi, 0, i, 0)),
                  pl.BlockSpec((None, DA_HEADS, None, DA_DV, tm),
                               lambda bi, i: (bi, 0, i // per_block, 0, i % per_block))]
    return pl.pallas_call(
        _proj_kernel,
        out_shape=head_shapes + [jax.ShapeDtypeStruct((b, t, w), dt) for w, dt in zip(widths, dtypes)],
        grid=(b, t // tm),
        in_specs=[row(d), vec, vec, _resident((1, d)), _resident(w_packed.shape),
                  _resident(w_up.shape), _resident(b_up.shape), tab, tab, tab],
        out_specs=head_specs + [row(w) for w in widths],
        compiler_params=_params("parallel", "parallel"),
        name="proj",
    )(x, shift, scale, gain, w_packed, w_up, b_up, cos, sa, sb)


def _rope_tables(t):
    pos = jnp.arange(t)
    inv = ROPE_THETA ** (-jnp.arange(0, ROPE_AXIS_DIM, 2, dtype=F32) / ROPE_AXIS_DIM)
    ang_r = (pos // GRID_W).astype(F32)[:, None] * inv
    ang_c = (pos % GRID_W).astype(F32)[:, None] * inv
    zero = jnp.zeros_like(ang_r)
    cos = jnp.concatenate([jnp.cos(ang_r)] * 2 + [jnp.cos(ang_c)] * 2, axis=-1)
    sa = jnp.concatenate([zero, jnp.sin(ang_r), zero, jnp.sin(ang_c)], axis=-1)
    sb = jnp.concatenate([-jnp.sin(ang_r), zero, -jnp.sin(ang_c), zero], axis=-1)
    rep = LANES // DA_DH
    return tuple(jnp.tile(a, (1, rep)) for a in (cos, sa, sb))


def _identity_tables(t):
    return jnp.ones((t, LANES), F32), jnp.zeros((t, LANES), F32), jnp.zeros((t, LANES), F32)


def _level_ref(b, s, row, reverse):
    c = b.shape[0]
    if 2 * s >= SUBLANES:
        pieces = []
        for blk in range(c // (2 * s)):
            r = blk * 2 * s + s - 1 + (1 if reverse else 0)
            pieces.append(jnp.broadcast_to(b[r:r + 1, :], (2 * s, b.shape[1])))
        return jnp.concatenate(pieces, axis=0)
    pos = row % (2 * s)
    mid = s if reverse else s - 1
    ref = b
    for p in range(2 * s):
        if p != mid:
            ref = jnp.where(pos == p, pltpu.roll(b, (p - mid) % c, 0), ref)
    return ref


def _split3(x):
    hi = x.astype(BF16)
    r = x - hi.astype(F32)
    mid = r.astype(BF16)
    lo = (r - mid.astype(F32)).astype(BF16)
    return hi, mid, lo


def _chunk_cumsum(g, reverse):
    c = g.shape[0]
    ri = lax.broadcasted_iota(jnp.int32, (c, c), 0)
    ci = lax.broadcasted_iota(jnp.int32, (c, c), 1)
    tri = jnp.where((ci >= ri) if reverse else (ci <= ri), 1.0, 0.0).astype(BF16)
    hi, mid, lo = _split3(g)
    return _mm(tri, lo) + _mm(tri, mid) + _mm(tri, hi)


def _gla_tables(c):
    levels = int(math.log2(c))
    row = np.arange(c)
    side = np.zeros((levels, 2, 2, c, GLA_DK), np.float32)
    same = np.zeros((levels, c, c), np.float32)
    for lvl in range(levels):
        s = 1 << lvl
        upper = (row % (2 * s)) >= s
        for rev in range(2):
            q_side = ~upper if rev else upper
            side[lvl, rev, 0] = np.where(q_side, 0.0, NEG_BIG)[:, None]
            side[lvl, rev, 1] = np.where(q_side, NEG_BIG, 0.0)[:, None]
        same[lvl] = (row[:, None] // (2 * s)) == (row[None, :] // (2 * s))
    causal = np.zeros((len(GLA_BASES), 2, c, c), np.float32)
    for i, base in enumerate(GLA_BASES):
        blk = row // min(base, c)
        inside = blk[:, None] == blk[None, :]
        causal[i, 0] = inside & (row[None, :] <= row[:, None])
        causal[i, 1] = inside & (row[None, :] >= row[:, None])
    return jnp.asarray(side), jnp.asarray(same), jnp.asarray(causal)


def _base_offset(b, base, reverse):
    c, n = b.shape
    pieces = []
    for i in range(c // base):
        r = (i + 1) * base if reverse else i * base - 1
        edge = b[r:r + 1, :] if 0 <= r < c else jnp.zeros((1, n), F32)
        pieces.append(jnp.broadcast_to(edge, (base, n)))
    return b - jnp.concatenate(pieces, axis=0)


def _gla_direction(q, k, v, b, st_ref, side_ref, same_ref, causal_ref, reverse, tier, base_offset):
    c, dk = b.shape
    row = lax.broadcasted_iota(jnp.int32, (c, dk), 0)
    total = b[0:1, :] if reverse else b[c - 1:c, :]
    qf = q.astype(F32)
    kf = k.astype(F32)
    rev = 1 if reverse else 0

    st = st_ref[...]
    out = _nt((qf * jnp.exp2(b)).astype(BF16), st.astype(BF16))

    if tier is None:
        ri = lax.broadcasted_iota(jnp.int32, (c, c), 0)
        ci = lax.broadcasted_iota(jnp.int32, (c, c), 1)
        a = jnp.where(ri == ci, _nt(q, k), 0.0)
        s = 1
    else:
        qd = (qf * jnp.exp2(base_offset)).astype(BF16)
        kd = (kf * jnp.exp2(-base_offset)).astype(BF16)
        a = _nt(qd, kd) * causal_ref[tier, rev]
        s = min(GLA_BASES[tier], c)
    lvl = int(math.log2(s))
    while s < c:
        t = b - _level_ref(b, s, row, reverse)
        ql = (qf * jnp.exp2(t + side_ref[lvl, rev, 0])).astype(BF16)
        kl = (kf * jnp.exp2(side_ref[lvl, rev, 1] - t)).astype(BF16)
        al = _nt(ql, kl)
        a = a + (al if 2 * s == c else al * same_ref[lvl])
        s *= 2
        lvl += 1
    out = out + _mm(a.astype(BF16), v)

    kd = (kf * jnp.exp2(total - b)).astype(BF16)
    st_ref[...] = st * jnp.exp2(total) + _tn(v, kd)
    return out


def _gla_kernel(qf_ref, kf_ref, vf_ref, gf_ref, qb_ref, kb_ref, vb_ref, gb_ref, sf0_ref, sb0_ref,
                side_ref, same_ref, causal_ref, of_ref, ob_ref, sf1_ref, sb1_ref, sf_ref, sb_ref):
    i = pl.program_id(2)
    heads = sf_ref.shape[0]

    @pl.when(i == 0)
    def _():
        sf_ref[...] = sf0_ref[...]
        sb_ref[...] = sb0_ref[...]

    gf = gf_ref[...] * LOG2_E
    gb = gb_ref[...] * LOG2_E
    c = gf.shape[0]
    steepest = jnp.maximum(jnp.max(jnp.abs(gf)), jnp.max(jnp.abs(gb)))

    def run(tier):
        bf = _chunk_cumsum(gf, False)
        bb = _chunk_cumsum(gb, True)
        base = None if tier is None else min(GLA_BASES[tier], c)
        df = None if tier is None else _base_offset(bf, base, False)
        db = None if tier is None else _base_offset(bb, base, True)
        for h in range(heads):
            ks = slice(h * GLA_DK, (h + 1) * GLA_DK)
            vs = slice(h * GLA_DV, (h + 1) * GLA_DV)
            of_ref[:, vs] = _gla_direction(
                qf_ref[:, ks], kf_ref[:, ks], vf_ref[:, vs], bf[:, ks], sf_ref.at[h], side_ref, same_ref,
                causal_ref, False, tier, None if tier is None else df[:, ks]).astype(of_ref.dtype)
            ob_ref[:, vs] = _gla_direction(
                qb_ref[:, ks], kb_ref[:, ks], vb_ref[:, vs], bb[:, ks], sb_ref.at[h], side_ref, same_ref,
                causal_ref, True, tier, None if tier is None else db[:, ks]).astype(ob_ref.dtype)

    taken = None
    for tier, base in enumerate(GLA_BASES):
        fits = min(base, c) * steepest <= GLA_BASE_LIMIT
        pl.when(fits if taken is None else jnp.logical_and(fits, jnp.logical_not(taken)))(
            functools.partial(run, tier))
        taken = fits if taken is None else jnp.logical_or(taken, fits)
    pl.when(jnp.logical_not(taken))(functools.partial(run, None))
    sf1_ref[...] = sf_ref[...]
    sb1_ref[...] = sb_ref[...]


def _gla_call(gq, gk, gv, gate, sf0, sb0):
    b, t, _ = gq.shape
    c = _largest_divisor(t, (GLA_CHUNK,))
    n = t // c
    hb = GLA_HEAD_BLOCK
    nhb = GLA_HEADS // hb
    fwd = lambda w, off=0: pl.BlockSpec((None, c, hb * w), lambda bi, h, i: (bi, i, h + off))
    bwd = lambda w, off=0: pl.BlockSpec((None, c, hb * w), lambda bi, h, i: (bi, n - 1 - i, h + off))
    state = pl.BlockSpec((None, hb, GLA_DV, GLA_DK), lambda bi, h, i: (bi, h, 0, 0))
    o_shape = jax.ShapeDtypeStruct((b, t, GLA_HEADS * GLA_DV), BF16)
    s_shape = jax.ShapeDtypeStruct((b, GLA_HEADS, GLA_DV, GLA_DK), F32)
    side, same, causal = _gla_tables(c)
    return pl.pallas_call(
        _gla_kernel,
        out_shape=[o_shape, o_shape, s_shape, s_shape],
        grid=(b, nhb, n),
        in_specs=[fwd(GLA_DK), fwd(GLA_DK), fwd(GLA_DV), fwd(GLA_DK),
                  bwd(GLA_DK), bwd(GLA_DK), bwd(GLA_DV), bwd(GLA_DK, nhb),
                  state, state, _resident(side.shape), _resident(same.shape), _resident(causal.shape)],
        out_specs=[fwd(GLA_DV), bwd(GLA_DV), state, state],
        scratch_shapes=[pltpu.VMEM((hb, GLA_DV, GLA_DK), F32), pltpu.VMEM((hb, GLA_DV, GLA_DK), F32)],
        compiler_params=_params("parallel", "parallel", "arbitrary"),
        name="gla",
    )(gq, gk, gv, gate, gq, gk, gv, gate, sf0, sb0, side, same, causal)


def _da_kernel(lq1_ref, lk1_ref, lq2_ref, lk2_ref, gain_ref, qt_ref, k_ref, vt_ref, ck_ref, cvt_ref,
               y_ref, kmax_ref, *, lam_init):
    tq = qt_ref.shape[-1]
    nk, tk, dq = k_ref.shape
    nck = ck_ref.shape[0]
    qt = qt_ref[...]
    sub = lax.broadcasted_iota(jnp.int32, qt.shape, 0)
    zero = jnp.zeros_like(qt)
    qm = jnp.concatenate([jnp.where(sub < DA_DH, qt, zero), jnp.where(sub < DA_DH, zero, qt)], axis=1)

    @pl.when(pl.program_id(2) == 0)
    def _():
        di = lax.broadcasted_iota(jnp.int32, (dq, dq), 0)
        ci = lax.broadcasted_iota(jnp.int32, (dq, dq), 1)
        same_comp = jnp.where((di < DA_DH) == (ci < DA_DH), 1.0, 0.0).astype(BF16)

        def block_max(kb, mx):
            kf = kb.astype(F32)
            nsq = _mm((kf * kf).astype(BF16), same_comp)
            return jnp.maximum(mx, jnp.max(nsq.reshape(-1, SUBLANES, dq), axis=0))

        mx = lax.fori_loop(0, nk, lambda j, mx: block_max(k_ref[j], mx), jnp.zeros((SUBLANES, dq), F32))
        for j in range(nck):
            mx = block_max(ck_ref[j], mx)
        kmax_ref[...] = jnp.broadcast_to(jnp.max(mx, axis=0, keepdims=True), kmax_ref.shape)

    qf = qt.astype(F32)
    qsq = qf * qf
    kmsq = kmax_ref[...]
    bsq = jnp.concatenate([jnp.sum(qsq[:DA_DH], axis=0, keepdims=True) * kmsq[0:1, 0:1],
                           jnp.sum(qsq[DA_DH:], axis=0, keepdims=True) * kmsq[0:1, DA_DH:DA_DH + 1]],
                          axis=1)
    bound = jnp.sqrt(bsq) * DA_BOUND_SLACK
    in_range = jnp.max(bound) <= DA_BOUND_LIMIT

    def probs(kb, shift):
        s = _mm(kb, qm)
        p = jnp.exp2(s.reshape(-1, SUBLANES, s.shape[1]) - shift[None])
        return p.reshape(s.shape)

    def finish(acc, l):
        lam = (jnp.exp(jnp.sum(lq1_ref[...] * lk1_ref[...], keepdims=True))
               - jnp.exp(jnp.sum(lq2_ref[...] * lk2_ref[...], keepdims=True)) + lam_init)
        o = acc / l
        ot = o[:, :tq] - lam * o[:, tq:]
        y = _rms(ot.T, gain_ref[...]) * (1.0 - lam_init)
        y_ref[...] = y.astype(y_ref.dtype)

    @pl.when(in_range)
    def _():
        shift = jnp.broadcast_to(bound, (SUBLANES, 2 * tq))
        acc = jnp.zeros((DA_DV, 2 * tq), F32)
        l = jnp.zeros((1, 2 * tq), F32)
        blocks = [(k_ref[j], vt_ref[j]) for j in range(nk)] + [(ck_ref[j], cvt_ref[j]) for j in range(nck)]
        for kb, vb in blocks:
            p = probs(kb, shift)
            l = l + jnp.sum(p, axis=0, keepdims=True)
            acc = acc + _mm(vb, p.astype(BF16))
        finish(acc, l)

    @pl.when(jnp.logical_not(in_range))
    def _():
        def block(kb, vb, carry):
            m_prev, l, acc = carry
            s = _mm(kb, qm)
            m_new = jnp.maximum(m_prev, jnp.max(s, axis=0, keepdims=True))
            alpha = jnp.exp2(m_prev - m_new)
            p = jnp.exp2(s - m_new)
            return (m_new, alpha * l + jnp.sum(p, axis=0, keepdims=True),
                    alpha * acc + _mm(vb, p.astype(BF16)))

        carry = (jnp.full((1, 2 * tq), NEG_BIG, F32), jnp.zeros((1, 2 * tq), F32),
                 jnp.zeros((DA_DV, 2 * tq), F32))
        carry = lax.fori_loop(0, nk, lambda j, c: block(k_ref[j], vt_ref[j], c), carry)
        for j in range(nck):
            carry = block(ck_ref[j], cvt_ref[j], carry)
        finish(carry[2], carry[1])


def _da_key_block(t):
    return _largest_divisor(t, DA_KEY_BLOCKS)


def _da_call(lams, gain, qt, k5, vt5, ck5, cvt5, lam_init):
    b, h, dq, t = qt.shape
    tq = _largest_divisor(t, DA_QUERY_BLOCKS)
    small = pl.BlockSpec((1, DA_DH), lambda bi, hi, i: (0, 0))
    per_head = lambda a: pl.BlockSpec((None, None) + a.shape[2:], lambda bi, hi, i: (bi, hi, 0, 0, 0))
    return pl.pallas_call(
        functools.partial(_da_kernel, lam_init=lam_init),
        out_shape=jax.ShapeDtypeStruct((b, t, h * DA_DV), BF16),
        grid=(b, h, t // tq),
        in_specs=[small, small, small, small,
                  pl.BlockSpec((1, DA_DV), lambda bi, hi, i: (0, 0)),
                  pl.BlockSpec((None, None, dq, tq), lambda bi, hi, i: (bi, hi, 0, i)),
                  per_head(k5), per_head(vt5), per_head(ck5), per_head(cvt5)],
        out_specs=pl.BlockSpec((None, tq, DA_DV), lambda bi, hi, i: (bi, i, hi)),
        scratch_shapes=[pltpu.VMEM((SUBLANES, dq), F32)],
        compiler_params=_params("parallel", "parallel", "arbitrary"),
        name="da",
    )(*lams, gain, qt, k5, vt5, ck5, cvt5)


def _merge_kernel(x_ref, gt_ref, yda_ref, of_ref, ob_ref, gout_ref, mrg_ref, ggain_ref, pgain_ref,
                  wda_ref, wgla_ref, wout_ref, o_ref):
    d = x_ref.shape[-1]
    o_gla = of_ref[...].astype(F32) + ob_ref[...].astype(F32)
    ggain = ggain_ref[...]
    heads = [_rms(o_gla[:, h * GLA_DV:(h + 1) * GLA_DV], ggain) for h in range(GLA_HEADS)]
    y_gla = (jnp.concatenate(heads, axis=-1) * gout_ref[...].astype(F32)).astype(BF16)
    gates = mrg_ref[...].astype(F32)
    y = (gates[:, :d] * _mm(yda_ref[...], wda_ref[...])
         + gates[:, d:] * _mm(y_gla, wgla_ref[...]))
    z = _mm(y.astype(BF16), wout_ref[...])
    o_ref[...] = x_ref[...] + gt_ref[...] * _rms(z, pgain_ref[...])


def _merge_call(x, gt1, y_da, o_f, o_b, gout, mrg, ggain, pgain, w_da, w_gla, w_out):
    b, t, d = x.shape
    tm = _largest_divisor(t, ROW_TILES)
    row = lambda width: pl.BlockSpec((None, tm, width), lambda bi, i: (bi, i, 0))
    vec = pl.BlockSpec((None, 1, d), lambda bi, i: (bi, 0, 0))
    return pl.pallas_call(
        _merge_kernel,
        out_shape=jax.ShapeDtypeStruct((b, t, d), F32),
        grid=(b, t // tm),
        in_specs=[row(d), vec, row(d), row(d), row(d), row(d), row(2 * d),
                  _resident(ggain.shape), _resident(pgain.shape),
                  _resident(w_da.shape), _resident(w_gla.shape), _resident(w_out.shape)],
        out_specs=row(d),
        compiler_params=_params("parallel", "parallel"),
        name="merge",
    )(x, gt1, y_da, o_f, o_b, gout, mrg, ggain, pgain, w_da, w_gla, w_out)


def _mlp_kernel(x_ref, sh_ref, sc_ref, gt_ref, pre_ref, post_ref, w1_ref, w2_ref, o_ref):
    x = x_ref[...]
    h = (_rms(x, pre_ref[...]) * (1.0 + sc_ref[...]) + sh_ref[...]).astype(BF16)
    u = jnp.maximum(_mm(h, w1_ref[...]), 0.0)
    z = _mm((u * u).astype(BF16), w2_ref[...])
    o_ref[...] = x + gt_ref[...] * _rms(z, post_ref[...])


def _mlp_call(x, sh2, sc2, gt2, pre, post, w1, w2):
    b, t, d = x.shape
    tm = _largest_divisor(t, ROW_TILES)
    row = pl.BlockSpec((None, tm, d), lambda bi, i: (bi, i, 0))
    vec = pl.BlockSpec((None, 1, d), lambda bi, i: (bi, 0, 0))
    return pl.pallas_call(
        _mlp_kernel,
        out_shape=jax.ShapeDtypeStruct((b, t, d), F32),
        grid=(b, t // tm),
        in_specs=[row, vec, vec, vec, _resident(pre.shape), _resident(post.shape),
                  _resident(w1.shape), _resident(w2.shape)],
        out_specs=row,
        compiler_params=_params("parallel", "parallel"),
        name="mlp",
    )(x, sh2, sc2, gt2, pre, post, w1, w2)


def _pack_w_in(w):
    low0 = _OFF_LOW
    low1 = low0 + 2 * GLA_GATE_RANK
    pad = jnp.zeros((w.shape[0], LOW_PAD - 2 * GLA_GATE_RANK), w.dtype)
    return jnp.concatenate([w[:, :low0], w[:, low0:low1], pad, w[:, low1:]], axis=1).astype(BF16)


def _pack_gate_up(w_gate_up, b_gate_up):
    hk = GLA_HEADS * GLA_DK
    w = jnp.zeros((LOW_PAD, 2 * hk), F32)
    for z in range(2):
        w = w.at[z * GLA_GATE_RANK:(z + 1) * GLA_GATE_RANK, z * hk:(z + 1) * hk].set(w_gate_up[z])
    return w.astype(BF16), b_gate_up.reshape(1, 2 * hk).astype(F32)


def kernel(x, c, ctx, c_ctx, w_mod, b_mod, pre_norm1, w_in, w_gate_up, b_gate_up, lambda_q1, lambda_k1,
           lambda_q2, lambda_k2, da_head_norm, gla_head_norm, w_branch_da, w_branch_gla, w_out, post_norm1,
           pre_norm2, w_ff1, w_ff2, post_norm2):
    b, t, d = x.shape
    n_ctx = ctx.shape[1]
    assert w_in.shape[0] == 1, "single-layer block only"
    assert d == DA_HEADS * DA_DV == GLA_HEADS * GLA_DV and t % GRID_W == 0
    layer = 0
    lam_init = 0.8 - 0.6 * math.exp(-0.3 * layer)

    rows = -(-(b + 1) // SUBLANES) * SUBLANES
    cc = jnp.zeros((rows, d), F32).at[:b].set(c).at[b].set(c_ctx)
    mod = _mod_call(cc, w_mod[layer], b_mod[layer])
    lat = [mod[:b, i * d:(i + 1) * d].reshape(b, 1, d) for i in range(N_MOD)]
    cmod = [jnp.broadcast_to(mod[b, i * d:(i + 1) * d].reshape(1, 1, d), (b, 1, d)) for i in range(2)]
    sh1, sc1, gt1, sh2, sc2, gt2 = lat

    w_packed = _pack_w_in(w_in[layer])
    w_up, b_up = _pack_gate_up(w_gate_up[layer], b_gate_up[layer])
    gain1 = pre_norm1[layer].reshape(1, d)

    qt, k, vt5, gq, gk, gv, gate, gout, mrg = _proj_call(
        x, sh1, sc1, gain1, w_packed, w_up, b_up, *_rope_tables(t))
    _, ck, cvt5, _, cgk, cgv, cgate, _, _ = _proj_call(
        ctx, cmod[0], cmod[1], gain1, w_packed, w_up, b_up, *_identity_tables(n_ctx))

    zero_state = jnp.zeros((b, GLA_HEADS, GLA_DV, GLA_DK), F32)
    _, _, sf0, sb0 = _gla_call(jnp.zeros_like(cgk), cgk, cgv, cgate, zero_state, zero_state)
    o_f, o_b, _, _ = _gla_call(gq, gk, gv, gate, sf0, sb0)

    k5 = k.reshape(b, DA_HEADS, vt5.shape[2], vt5.shape[4], 2 * DA_DH)
    ck5 = ck.reshape(b, DA_HEADS, cvt5.shape[2], cvt5.shape[4], 2 * DA_DH)
    lams = [a[layer].reshape(1, DA_DH) for a in (lambda_q1, lambda_k1, lambda_q2, lambda_k2)]
    y_da = _da_call(lams, da_head_norm[layer].reshape(1, DA_DV), qt, k5, vt5, ck5, cvt5, lam_init)

    x1 = _merge_call(x, gt1, y_da, o_f, o_b, gout, mrg,
                     gla_head_norm[layer].reshape(1, GLA_DV), post_norm1[layer].reshape(1, d),
                     w_branch_da[layer].astype(BF16), w_branch_gla[layer].astype(BF16),
                     w_out[layer].astype(BF16))
    return _mlp_call(x1, sh2, sc2, gt2, pre_norm2[layer].reshape(1, d), post_norm2[layer].reshape(1, d),
                     w_ff1[layer].astype(BF16), w_ff2[layer].astype(BF16))
```

```python
import functools
import math

import numpy as np
import jax
import jax.numpy as jnp
from jax import lax
from jax.experimental import pallas as pl
from jax.experimental.pallas import tpu as pltpu

F32 = jnp.float32
BF16 = jnp.bfloat16

GRID_W = 64
DA_HEADS = 8
DA_DH = 64
DA_DV = 2 * DA_DH
GLA_HEADS = 4
GLA_DK = 128
GLA_DV = 256
GLA_GATE_RANK = 16
GLA_GATE_NORM = 16.0
ROPE_THETA = 10000.0
ROPE_AXIS_DIM = DA_DH // 2
ROPE_HALF = ROPE_AXIS_DIM // 2
N_MOD = 6
EPS = 1e-6
NEG_BIG = -1e30

LANES = 128
SUBLANES = 8
LOG2_E = math.log2(math.e)
VMEM_LIMIT = 56 * 1024 * 1024

ROW_TILES = (512, 256, 128)
GLA_CHUNK = 128
GLA_BASES = (128, 16)
GLA_BASE_LIMIT = 100.0
GLA_HEAD_BLOCK = 4
DA_KEY_BLOCKS = (1024, 512, 256, 128)
DA_QUERY_BLOCKS = (1024, 512, 256, 128)
DA_BOUND_LIMIT = 50.0
DA_BOUND_SLACK = 1.01
LOW_PAD = LANES


def _nt(a, b):
    return lax.dot_general(a, b, (((1,), (1,)), ((), ())), preferred_element_type=F32)


def _tn(a, b):
    return lax.dot_general(a, b, (((0,), (0,)), ((), ())), preferred_element_type=F32)


def _mm(a, b):
    return jnp.dot(a, b, preferred_element_type=F32)


def _rms(x, gain):
    return x * lax.rsqrt(jnp.mean(x * x, axis=-1, keepdims=True) + EPS) * gain


def _params(*sem):
    return pltpu.CompilerParams(dimension_semantics=sem, vmem_limit_bytes=VMEM_LIMIT)


def _resident(shape):
    nd = len(shape)
    return pl.BlockSpec(shape, lambda *_: (0,) * nd, pipeline_mode=pl.Buffered(1))


def _largest_divisor(n, candidates):
    for cand in candidates:
        if n % cand == 0:
            return cand
    raise ValueError(f"no tile in {candidates} divides {n}")


def _mod_kernel(c_ref, w_ref, b_ref, o_ref):
    c = c_ref[...]
    s = (c * jax.nn.sigmoid(c)).astype(BF16)
    o_ref[...] = _mm(s, w_ref[...].astype(BF16)) + b_ref[...]


def _mod_call(cc, w_mod, b_mod):
    rows, d = cc.shape
    n = w_mod.shape[1]
    bn = _largest_divisor(n, (1024, 512, 256, 128))
    return pl.pallas_call(
        _mod_kernel,
        out_shape=jax.ShapeDtypeStruct((rows, n), F32),
        grid=(n // bn,),
        in_specs=[pl.BlockSpec((rows, d), lambda j: (0, 0)),
                  pl.BlockSpec((d, bn), lambda j: (0, j)),
                  pl.BlockSpec((1, bn), lambda j: (0, j))],
        out_specs=pl.BlockSpec((rows, bn), lambda j: (0, j)),
        compiler_params=_params("arbitrary"),
        name="mod",
    )(cc, w_mod, b_mod.reshape(1, n))


_OFF_Q = 0
_OFF_K = _OFF_Q + DA_HEADS * 2 * DA_DH
_OFF_V = _OFF_K + DA_HEADS * 2 * DA_DH
_OFF_GQ = _OFF_V + DA_HEADS * DA_DV
_OFF_GK = _OFF_GQ + GLA_HEADS * GLA_DK
_OFF_GV = _OFF_GK + GLA_HEADS * GLA_DK
_OFF_LOW = _OFF_GV + GLA_HEADS * GLA_DV
_OFF_GOUT = _OFF_LOW + LOW_PAD
_OFF_MERGE = _OFF_GOUT + GLA_HEADS * GLA_DV
_N_GATE = 2 * GLA_HEADS * GLA_DK


def _proj_kernel(x_ref, sh_ref, sc_ref, gain_ref, w_ref, wup_ref, bup_ref, cos_ref, sa_ref, sb_ref,
                 q_ref, k_ref, v_ref, gq_ref, gk_ref, gv_ref, gate_ref, gout_ref, mrg_ref):
    d = x_ref.shape[-1]
    h = (_rms(x_ref[...], gain_ref[...]) * (1.0 + sc_ref[...]) + sh_ref[...]).astype(BF16)
    cos, sa, sb = cos_ref[...], sa_ref[...], sb_ref[...]

    def roped(off, scale):
        acc = _mm(h, w_ref[:, off:off + d])
        for s in range(d // LANES):
            a = acc[:, s * LANES:(s + 1) * LANES]
            r = (a * cos + pltpu.roll(a, ROPE_HALF, 1) * sa
                 + pltpu.roll(a, LANES - ROPE_HALF, 1) * sb)
            yield s, r * scale

    low = _mm(h, w_ref[:, _OFF_LOW:_OFF_GOUT]).astype(BF16)
    z = _mm(low, wup_ref[...]) + bup_ref[...]
    log_sig = jnp.minimum(z, 0.0) - jnp.log1p(jnp.exp(-jnp.abs(z)))
    gate_ref[...] = log_sig / GLA_GATE_NORM
    mg = _mm(h, w_ref[:, _OFF_MERGE:_OFF_MERGE + 2 * d])
    mrg_ref[...] = jax.nn.sigmoid(mg).astype(mrg_ref.dtype)
    go = _mm(h, w_ref[:, _OFF_GOUT:_OFF_MERGE])
    gout_ref[...] = (go * jax.nn.sigmoid(go)).astype(gout_ref.dtype)
    for hd, slab in roped(_OFF_Q, DA_DH ** -0.5 * LOG2_E):
        q_ref[hd] = slab.T.astype(q_ref.dtype)
    for hd, slab in roped(_OFF_K, 1.0):
        k_ref[hd] = slab.astype(k_ref.dtype)
    v = _mm(h, w_ref[:, _OFF_V:_OFF_GQ])
    for hd in range(DA_HEADS):
        v_ref[hd] = v[:, hd * DA_DV:(hd + 1) * DA_DV].T.astype(v_ref.dtype)
    gq_ref[...] = (_mm(h, w_ref[:, _OFF_GQ:_OFF_GK]) * (GLA_DK ** -0.5)).astype(gq_ref.dtype)
    gk_ref[...] = _mm(h, w_ref[:, _OFF_GK:_OFF_GV]).astype(gk_ref.dtype)
    gv_ref[...] = _mm(h, w_ref[:, _OFF_GV:_OFF_LOW]).astype(gv_ref.dtype)


def _proj_call(x, shift, scale, gain, w_packed, w_up, b_up, cos, sa, sb):
    b, t, d = x.shape
    tm = _largest_divisor(t, ROW_TILES)
    tk = _da_key_block(t)
    per_block = tk // tm
    row = lambda width: pl.BlockSpec((None, tm, width), lambda bi, i: (bi, i, 0))
    vec = pl.BlockSpec((None, 1, d), lambda bi, i: (bi, 0, 0))
    tab = pl.BlockSpec((tm, LANES), lambda bi, i: (i, 0))
    widths = (GLA_HEADS * GLA_DK, GLA_HEADS * GLA_DK, GLA_HEADS * GLA_DV,
              _N_GATE, GLA_HEADS * GLA_DV, 2 * d)
    dtypes = (BF16, BF16, BF16, F32, BF16, BF16)
    dq = 2 * DA_DH
    head_shapes = [jax.ShapeDtypeStruct((b, DA_HEADS, dq, t), BF16),
                   jax.ShapeDtypeStruct((b, DA_HEADS, t, dq), BF16),
                   jax.ShapeDtypeStruct((b, DA_HEADS, t // tk, DA_DV, tk), BF16)]
    head_specs = [pl.BlockSpec((None, DA_HEADS, dq, tm), lambda bi, i: (bi, 0, 0, i)),
                  pl.BlockSpec((None, DA_HEADS, tm, dq), lambda bi, i: (bi, 0, i, 0)),
                  pl.BlockSpec((None, DA_HEADS, None, DA_DV, tm),
                               lambda bi, i: (bi, 0, i // per_block, 0, i % per_block))]
    return pl.pallas_call(
        _proj_kernel,
        out_shape=head_shapes + [jax.ShapeDtypeStruct((b, t, w), dt) for w, dt in zip(widths, dtypes)],
        grid=(b, t // tm),
        in_specs=[row(d), vec, vec, _resident((1, d)), _resident(w_packed.shape),
                  _resident(w_up.shape), _resident(b_up.shape), tab, tab, tab],
        out_specs=head_specs + [row(w) for w in widths],
        compiler_params=_params("parallel", "parallel"),
        name="proj",
    )(x, shift, scale, gain, w_packed, w_up, b_up, cos, sa, sb)


def _rope_tables(t):
    pos = jnp.arange(t)
    inv = ROPE_THETA ** (-jnp.arange(0, ROPE_AXIS_DIM, 2, dtype=F32) / ROPE_AXIS_DIM)
    ang_r = (pos // GRID_W).astype(F32)[:, None] * inv
    ang_c = (pos % GRID_W).astype(F32)[:, None] * inv
    zero = jnp.zeros_like(ang_r)
    cos = jnp.concatenate([jnp.cos(ang_r)] * 2 + [jnp.cos(ang_c)] * 2, axis=-1)
    sa = jnp.concatenate([zero, jnp.sin(ang_r), zero, jnp.sin(ang_c)], axis=-1)
    sb = jnp.concatenate([-jnp.sin(ang_r), zero, -jnp.sin(ang_c), zero], axis=-1)
    rep = LANES // DA_DH
    return tuple(jnp.tile(a, (1, rep)) for a in (cos, sa, sb))


def _identity_tables(t):
    return jnp.ones((t, LANES), F32), jnp.zeros((t, LANES), F32), jnp.zeros((t, LANES), F32)


def _level_ref(b, s, row, reverse):
    c = b.shape[0]
    if 2 * s >= SUBLANES:
        pieces = []
        for blk in range(c // (2 * s)):
            r = blk * 2 * s + s - 1 + (1 if reverse else 0)
            pieces.append(jnp.broadcast_to(b[r:r + 1, :], (2 * s, b.shape[1])))
        return jnp.concatenate(pieces, axis=0)
    pos = row % (2 * s)
    mid = s if reverse else s - 1
    ref = b
    for p in range(2 * s):
        if p != mid:
            ref = jnp.where(pos == p, pltpu.roll(b, (p - mid) % c, 0), ref)
    return ref


def _split3(x):
    hi = x.astype(BF16)
    r = x - hi.astype(F32)
    mid = r.astype(BF16)
    lo = (r - mid.astype(F32)).astype(BF16)
    return hi, mid, lo


def _chunk_cumsum(g, reverse):
    c = g.shape[0]
    ri = lax.broadcasted_iota(jnp.int32, (c, c), 0)
    ci = lax.broadcasted_iota(jnp.int32, (c, c), 1)
    tri = jnp.where((ci >= ri) if reverse else (ci <= ri), 1.0, 0.0).astype(BF16)
    hi, mid, lo = _split3(g)
    return _mm(tri, lo) + _mm(tri, mid) + _mm(tri, hi)


def _gla_tables(c):
    levels = int(math.log2(c))
    row = np.arange(c)
    side = np.zeros((levels, 2, 2, c, GLA_DK), np.float32)
    same = np.zeros((levels, c, c), np.float32)
    for lvl in range(levels):
        s = 1 << lvl
        upper = (row % (2 * s)) >= s
        for rev in range(2):
            q_side = ~upper if rev else upper
            side[lvl, rev, 0] = np.where(q_side, 0.0, NEG_BIG)[:, None]
            side[lvl, rev, 1] = np.where(q_side, NEG_BIG, 0.0)[:, None]
        same[lvl] = (row[:, None] // (2 * s)) == (row[None, :] // (2 * s))
    causal = np.zeros((len(GLA_BASES), 2, c, c), np.float32)
    for i, base in enumerate(GLA_BASES):
        blk = row // min(base, c)
        inside = blk[:, None] == blk[None, :]
        causal[i, 0] = inside & (row[None, :] <= row[:, None])
        causal[i, 1] = inside & (row[None, :] >= row[:, None])
    return jnp.asarray(side), jnp.asarray(same), jnp.asarray(causal)


def _base_offset(b, base, reverse):
    c, n = b.shape
    pieces = []
    for i in range(c // base):
        r = (i + 1) * base if reverse else i * base - 1
        edge = b[r:r + 1, :] if 0 <= r < c else jnp.zeros((1, n), F32)
        pieces.append(jnp.broadcast_to(edge, (base, n)))
    return b - jnp.concatenate(pieces, axis=0)


def _gla_direction(q, k, v, b, st_ref, side_ref, same_ref, causal_ref, reverse, tier, base_offset):
    c, dk = b.shape
    row = lax.broadcasted_iota(jnp.int32, (c, dk), 0)
    total = b[0:1, :] if reverse else b[c - 1:c, :]
    qf = q.astype(F32)
    kf = k.astype(F32)
    rev = 1 if reverse else 0

    st = st_ref[...]
    out = _nt((qf * jnp.exp2(b)).astype(BF16), st.astype(BF16))

    if tier is None:
        ri = lax.broadcasted_iota(jnp.int32, (c, c), 0)
        ci = lax.broadcasted_iota(jnp.int32, (c, c), 1)
        a = jnp.where(ri == ci, _nt(q, k), 0.0)
        s = 1
    else:
        qd = (qf * jnp.exp2(base_offset)).astype(BF16)
        kd = (kf * jnp.exp2(-base_offset)).astype(BF16)
        a = _nt(qd, kd) * causal_ref[tier, rev]
        s = min(GLA_BASES[tier], c)
    lvl = int(math.log2(s))
    while s < c:
        t = b - _level_ref(b, s, row, reverse)
        ql = (qf * jnp.exp2(t + side_ref[lvl, rev, 0])).astype(BF16)
        kl = (kf * jnp.exp2(side_ref[lvl, rev, 1] - t)).astype(BF16)
        al = _nt(ql, kl)
        a = a + (al if 2 * s == c else al * same_ref[lvl])
        s *= 2
        lvl += 1
    out = out + _mm(a.astype(BF16), v)

    kd = (kf * jnp.exp2(total - b)).astype(BF16)
    st_ref[...] = st * jnp.exp2(total) + _tn(v, kd)
    return out


def _gla_kernel(qf_ref, kf_ref, vf_ref, gf_ref, qb_ref, kb_ref, vb_ref, gb_ref, sf0_ref, sb0_ref,
                side_ref, same_ref, causal_ref, of_ref, ob_ref, sf1_ref, sb1_ref, sf_ref, sb_ref):
    i = pl.program_id(2)
    heads = sf_ref.shape[0]

    @pl.when(i == 0)
    def _():
        sf_ref[...] = sf0_ref[...]
        sb_ref[...] = sb0_ref[...]

    gf = gf_ref[...] * LOG2_E
    gb = gb_ref[...] * LOG2_E
    c = gf.shape[0]
    steepest = jnp.maximum(jnp.max(jnp.abs(gf)), jnp.max(jnp.abs(gb)))
    bf = _chunk_cumsum(gf, False)
    bb = _chunk_cumsum(gb, True)

    def run(tier):
        base = None if tier is None else min(GLA_BASES[tier], c)
        df = None if tier is None else _base_offset(bf, base, False)
        db = None if tier is None else _base_offset(bb, base, True)
        for h in range(heads):
            ks = slice(h * GLA_DK, (h + 1) * GLA_DK)
            vs = slice(h * GLA_DV, (h + 1) * GLA_DV)
            of_ref[:, vs] = _gla_direction(
                qf_ref[:, ks], kf_ref[:, ks], vf_ref[:, vs], bf[:, ks], sf_ref.at[h], side_ref, same_ref,
                causal_ref, False, tier, None if tier is None else df[:, ks]).astype(of_ref.dtype)
            ob_ref[:, vs] = _gla_direction(
                qb_ref[:, ks], kb_ref[:, ks], vb_ref[:, vs], bb[:, ks], sb_ref.at[h], side_ref, same_ref,
                causal_ref, True, tier, None if tier is None else db[:, ks]).astype(ob_ref.dtype)

    taken = None
    for tier, base in enumerate(GLA_BASES):
        fits = min(base, c) * steepest <= GLA_BASE_LIMIT
        pl.when(fits if taken is None else jnp.logical_and(fits, jnp.logical_not(taken)))(
            functools.partial(run, tier))
        taken = fits if taken is None else jnp.logical_or(taken, fits)
    pl.when(jnp.logical_not(taken))(functools.partial(run, None))

    @pl.when(i == pl.num_programs(2) - 1)
    def _():
        sf1_ref[...] = sf_ref[...]
        sb1_ref[...] = sb_ref[...]


def _gla_call(gq, gk, gv, gate, sf0, sb0):
    b, t, _ = gq.shape
    c = _largest_divisor(t, (GLA_CHUNK,))
    n = t // c
    hb = GLA_HEAD_BLOCK
    nhb = GLA_HEADS // hb
    fwd = lambda w, off=0: pl.BlockSpec((None, c, hb * w), lambda bi, h, i: (bi, i, h + off))
    bwd = lambda w, off=0: pl.BlockSpec((None, c, hb * w), lambda bi, h, i: (bi, n - 1 - i, h + off))
    state = pl.BlockSpec((None, hb, GLA_DV, GLA_DK), lambda bi, h, i: (bi, h, 0, 0))
    o_shape = jax.ShapeDtypeStruct((b, t, GLA_HEADS * GLA_DV), BF16)
    s_shape = jax.ShapeDtypeStruct((b, GLA_HEADS, GLA_DV, GLA_DK), F32)
    side, same, causal = _gla_tables(c)
    return pl.pallas_call(
        _gla_kernel,
        out_shape=[o_shape, o_shape, s_shape, s_shape],
        grid=(b, nhb, n),
        in_specs=[fwd(GLA_DK), fwd(GLA_DK), fwd(GLA_DV), fwd(GLA_DK),
                  bwd(GLA_DK), bwd(GLA_DK), bwd(GLA_DV), bwd(GLA_DK, nhb),
                  state, state, _resident(side.shape), _resident(same.shape), _resident(causal.shape)],
        out_specs=[fwd(GLA_DV), bwd(GLA_DV), state, state],
        scratch_shapes=[pltpu.VMEM((hb, GLA_DV, GLA_DK), F32), pltpu.VMEM((hb, GLA_DV, GLA_DK), F32)],
        compiler_params=_params("parallel", "parallel", "arbitrary"),
        name="gla",
    )(gq, gk, gv, gate, gq, gk, gv, gate, sf0, sb0, side, same, causal)


def _da_kernel(lq1_ref, lk1_ref, lq2_ref, lk2_ref, gain_ref, qt_ref, k_ref, vt_ref, ck_ref, cvt_ref,
               y_ref, kmax_ref, *, lam_init):
    tq = qt_ref.shape[-1]
    nk, tk, dq = k_ref.shape
    nck = ck_ref.shape[0]
    qt = qt_ref[...]
    sub = lax.broadcasted_iota(jnp.int32, qt.shape, 0)
    zero = jnp.zeros_like(qt)
    qm = jnp.concatenate([jnp.where(sub < DA_DH, qt, zero), jnp.where(sub < DA_DH, zero, qt)], axis=1)

    @pl.when(pl.program_id(2) == 0)
    def _():
        di = lax.broadcasted_iota(jnp.int32, (dq, dq), 0)
        ci = lax.broadcasted_iota(jnp.int32, (dq, dq), 1)
        same_comp = jnp.where((di < DA_DH) == (ci < DA_DH), 1.0, 0.0).astype(BF16)

        def block_max(kb, mx):
            nsq = _mm(kb * kb, same_comp)
            return jnp.maximum(mx, jnp.max(nsq.reshape(-1, SUBLANES, dq), axis=0))

        mx = lax.fori_loop(0, nk, lambda j, mx: block_max(k_ref[j], mx), jnp.zeros((SUBLANES, dq), F32))
        for j in range(nck):
            mx = block_max(ck_ref[j], mx)
        kmax_ref[...] = jnp.broadcast_to(jnp.max(mx, axis=0, keepdims=True), kmax_ref.shape)

    qf = qt.astype(F32)
    qsq = qf * qf
    kmsq = kmax_ref[...]
    bsq = jnp.concatenate([jnp.sum(qsq[:DA_DH], axis=0, keepdims=True) * kmsq[0:1, 0:1],
                           jnp.sum(qsq[DA_DH:], axis=0, keepdims=True) * kmsq[0:1, DA_DH:DA_DH + 1]],
                          axis=1)
    bound = jnp.sqrt(bsq) * DA_BOUND_SLACK
    in_range = jnp.max(bound) <= DA_BOUND_LIMIT

    def probs(kb, shift):
        s = _mm(kb, qm)
        p = jnp.exp2(s.reshape(-1, SUBLANES, s.shape[1]) - shift[None])
        return p.reshape(s.shape)

    def finish(acc, l):
        lam = (jnp.exp(jnp.sum(lq1_ref[...] * lk1_ref[...], keepdims=True))
               - jnp.exp(jnp.sum(lq2_ref[...] * lk2_ref[...], keepdims=True)) + lam_init)
        o = acc / l
        ot = o[:, :tq] - lam * o[:, tq:]
        y = _rms(ot.T, gain_ref[...]) * (1.0 - lam_init)
        y_ref[...] = y.astype(y_ref.dtype)

    @pl.when(in_range)
    def _():
        shift = jnp.broadcast_to(bound, (SUBLANES, 2 * tq))
        acc = jnp.zeros((DA_DV, 2 * tq), F32)
        l = jnp.zeros((1, 2 * tq), F32)
        blocks = [(k_ref[j], vt_ref[j]) for j in range(nk)] + [(ck_ref[j], cvt_ref[j]) for j in range(nck)]
        for kb, vb in blocks:
            p = probs(kb, shift)
            l = l + jnp.sum(p, axis=0, keepdims=True)
            acc = acc + _mm(vb, p.astype(BF16))
        finish(acc, l)

    @pl.when(jnp.logical_not(in_range))
    def _():
        def block(kb, vb, carry):
            m_prev, l, acc = carry
            s = _mm(kb, qm)
            m_new = jnp.maximum(m_prev, jnp.max(s, axis=0, keepdims=True))
            alpha = jnp.exp2(m_prev - m_new)
            p = jnp.exp2(s - m_new)
            return (m_new, alpha * l + jnp.sum(p, axis=0, keepdims=True),
                    alpha * acc + _mm(vb, p.astype(BF16)))

        carry = (jnp.full((1, 2 * tq), NEG_BIG, F32), jnp.zeros((1, 2 * tq), F32),
                 jnp.zeros((DA_DV, 2 * tq), F32))
        carry = lax.fori_loop(0, nk, lambda j, c: block(k_ref[j], vt_ref[j], c), carry)
        for j in range(nck):
            carry = block(ck_ref[j], cvt_ref[j], carry)
        finish(carry[2], carry[1])


def _da_key_block(t):
    return _largest_divisor(t, DA_KEY_BLOCKS)


def _da_call(lams, gain, qt, k5, vt5, ck5, cvt5, lam_init):
    b, h, dq, t = qt.shape
    tq = _largest_divisor(t, DA_QUERY_BLOCKS)
    small = pl.BlockSpec((1, DA_DH), lambda bi, hi, i: (0, 0))
    per_head = lambda a: pl.BlockSpec((None, None) + a.shape[2:], lambda bi, hi, i: (bi, hi, 0, 0, 0))
    return pl.pallas_call(
        functools.partial(_da_kernel, lam_init=lam_init),
        out_shape=jax.ShapeDtypeStruct((b, t, h * DA_DV), BF16),
        grid=(b, h, t // tq),
        in_specs=[small, small, small, small,
                  pl.BlockSpec((1, DA_DV), lambda bi, hi, i: (0, 0)),
                  pl.BlockSpec((None, None, dq, tq), lambda bi, hi, i: (bi, hi, 0, i)),
                  per_head(k5), per_head(vt5), per_head(ck5), per_head(cvt5)],
        out_specs=pl.BlockSpec((None, tq, DA_DV), lambda bi, hi, i: (bi, i, hi)),
        scratch_shapes=[pltpu.VMEM((SUBLANES, dq), F32)],
        compiler_params=_params("parallel", "parallel", "arbitrary"),
        name="da",
    )(*lams, gain, qt, k5, vt5, ck5, cvt5)


def _merge_kernel(x_ref, gt_ref, yda_ref, of_ref, ob_ref, gout_ref, mrg_ref, ggain_ref, pgain_ref,
                  wda_ref, wgla_ref, wout_ref, o_ref):
    d = x_ref.shape[-1]
    o_gla = of_ref[...].astype(F32) + ob_ref[...].astype(F32)
    ggain = ggain_ref[...]
    heads = [_rms(o_gla[:, h * GLA_DV:(h + 1) * GLA_DV], ggain) for h in range(GLA_HEADS)]
    y_gla = (jnp.concatenate(heads, axis=-1) * gout_ref[...].astype(F32)).astype(BF16)
    gates = mrg_ref[...].astype(F32)
    y = (gates[:, :d] * _mm(yda_ref[...], wda_ref[...])
         + gates[:, d:] * _mm(y_gla, wgla_ref[...]))
    z = _mm(y.astype(BF16), wout_ref[...])
    o_ref[...] = x_ref[...] + gt_ref[...] * _rms(z, pgain_ref[...])


def _merge_call(x, gt1, y_da, o_f, o_b, gout, mrg, ggain, pgain, w_da, w_gla, w_out):
    b, t, d = x.shape
    tm = _largest_divisor(t, ROW_TILES)
    row = lambda width: pl.BlockSpec((None, tm, width), lambda bi, i: (bi, i, 0))
    vec = pl.BlockSpec((None, 1, d), lambda bi, i: (bi, 0, 0))
    return pl.pallas_call(
        _merge_kernel,
        out_shape=jax.ShapeDtypeStruct((b, t, d), F32),
        grid=(b, t // tm),
        in_specs=[row(d), vec, row(d), row(d), row(d), row(d), row(2 * d),
                  _resident(ggain.shape), _resident(pgain.shape),
                  _resident(w_da.shape), _resident(w_gla.shape), _resident(w_out.shape)],
        out_specs=row(d),
        compiler_params=_params("parallel", "parallel"),
        name="merge",
    )(x, gt1, y_da, o_f, o_b, gout, mrg, ggain, pgain, w_da, w_gla, w_out)


def _mlp_kernel(x_ref, sh_ref, sc_ref, gt_ref, pre_ref, post_ref, w1_ref, w2_ref, o_ref):
    x = x_ref[...]
    h = (_rms(x, pre_ref[...]) * (1.0 + sc_ref[...]) + sh_ref[...]).astype(BF16)
    u = jnp.maximum(_mm(h, w1_ref[...]), 0.0)
    z = _mm((u * u).astype(BF16), w2_ref[...])
    o_ref[...] = x + gt_ref[...] * _rms(z, post_ref[...])


def _mlp_call(x, sh2, sc2, gt2, pre, post, w1, w2):
    b, t, d = x.shape
    tm = _largest_divisor(t, ROW_TILES)
    row = pl.BlockSpec((None, tm, d), lambda bi, i: (bi, i, 0))
    vec = pl.BlockSpec((None, 1, d), lambda bi, i: (bi, 0, 0))
    return pl.pallas_call(
        _mlp_kernel,
        out_shape=jax.ShapeDtypeStruct((b, t, d), F32),
        grid=(b, t // tm),
        in_specs=[row, vec, vec, vec, _resident(pre.shape), _resident(post.shape),
                  _resident(w1.shape), _resident(w2.shape)],
        out_specs=row,
        compiler_params=_params("parallel", "parallel"),
        name="mlp",
    )(x, sh2, sc2, gt2, pre, post, w1, w2)


def _pack_w_in(w):
    low0 = _OFF_LOW
    low1 = low0 + 2 * GLA_GATE_RANK
    pad = jnp.zeros((w.shape[0], LOW_PAD - 2 * GLA_GATE_RANK), w.dtype)
    return jnp.concatenate([w[:, :low0], w[:, low0:low1], pad, w[:, low1:]], axis=1).astype(BF16)


def _pack_gate_up(w_gate_up, b_gate_up):
    hk = GLA_HEADS * GLA_DK
    w = jnp.zeros((LOW_PAD, 2 * hk), F32)
    for z in range(2):
        w = w.at[z * GLA_GATE_RANK:(z + 1) * GLA_GATE_RANK, z * hk:(z + 1) * hk].set(w_gate_up[z])
    return w.astype(BF16), b_gate_up.reshape(1, 2 * hk).astype(F32)


def kernel(x, c, ctx, c_ctx, w_mod, b_mod, pre_norm1, w_in, w_gate_up, b_gate_up, lambda_q1, lambda_k1,
           lambda_q2, lambda_k2, da_head_norm, gla_head_norm, w_branch_da, w_branch_gla, w_out, post_norm1,
           pre_norm2, w_ff1, w_ff2, post_norm2):
    b, t, d = x.shape
    n_ctx = ctx.shape[1]
    assert w_in.shape[0] == 1, "single-layer block only"
    assert d == DA_HEADS * DA_DV == GLA_HEADS * GLA_DV and t % GRID_W == 0
    layer = 0
    lam_init = 0.8 - 0.6 * math.exp(-0.3 * layer)

    rows = -(-(b + 1) // SUBLANES) * SUBLANES
    cc = jnp.zeros((rows, d), F32).at[:b].set(c).at[b].set(c_ctx)
    mod = _mod_call(cc, w_mod[layer], b_mod[layer])
    lat = [mod[:b, i * d:(i + 1) * d].reshape(b, 1, d) for i in range(N_MOD)]
    cmod = [jnp.broadcast_to(mod[b, i * d:(i + 1) * d].reshape(1, 1, d), (b, 1, d)) for i in range(2)]
    sh1, sc1, gt1, sh2, sc2, gt2 = lat

    w_packed = _pack_w_in(w_in[layer])
    w_up, b_up = _pack_gate_up(w_gate_up[layer], b_gate_up[layer])
    gain1 = pre_norm1[layer].reshape(1, d)

    qt, k, vt5, gq, gk, gv, gate, gout, mrg = _proj_call(
        x, sh1, sc1, gain1, w_packed, w_up, b_up, *_rope_tables(t))
    _, ck, cvt5, _, cgk, cgv, cgate, _, _ = _proj_call(
        ctx, cmod[0], cmod[1], gain1, w_packed, w_up, b_up, *_identity_tables(n_ctx))

    zero_state = jnp.zeros((b, GLA_HEADS, GLA_DV, GLA_DK), F32)
    _, _, sf0, sb0 = _gla_call(jnp.zeros_like(cgk), cgk, cgv, cgate, zero_state, zero_state)
    o_f, o_b, _, _ = _gla_call(gq, gk, gv, gate, sf0, sb0)

    k5 = k.reshape(b, DA_HEADS, vt5.shape[2], vt5.shape[4], 2 * DA_DH)
    ck5 = ck.reshape(b, DA_HEADS, cvt5.shape[2], cvt5.shape[4], 2 * DA_DH)
    lams = [a[layer].reshape(1, DA_DH) for a in (lambda_q1, lambda_k1, lambda_q2, lambda_k2)]
    y_da = _da_call(lams, da_head_norm[layer].reshape(1, DA_DV), qt, k5, vt5, ck5, cvt5, lam_init)

    x1 = _merge_call(x, gt1, y_da, o_f, o_b, gout, mrg,
                     gla_head_norm[layer].reshape(1, GLA_DV), post_norm1[layer].reshape(1, d),
                     w_branch_da[layer].astype(BF16), w_branch_gla[layer].astype(BF16),
                     w_out[layer].astype(BF16))
    return _mlp_call(x1, sh2, sc2, gt2, pre_norm2[layer].reshape(1, d), post_norm2[layer].reshape(1, d),
                     w_ff1[layer].astype(BF16), w_ff2[layer].astype(BF16))
```

```python
import functools
import math

import numpy as np
import jax
import jax.numpy as jnp
from jax import lax
from jax.experimental import pallas as pl
from jax.experimental.pallas import tpu as pltpu

F32 = jnp.float32
BF16 = jnp.bfloat16

GRID_W = 64
DA_HEADS = 8
DA_DH = 64
DA_DV = 2 * DA_DH
GLA_HEADS = 4
GLA_DK = 128
GLA_DV = 256
GLA_GATE_RANK = 16
GLA_GATE_NORM = 16.0
ROPE_THETA = 10000.0
ROPE_AXIS_DIM = DA_DH // 2
ROPE_HALF = ROPE_AXIS_DIM // 2
N_MOD = 6
EPS = 1e-6
NEG_BIG = -1e30

LANES = 128
SUBLANES = 8
LOG2_E = math.log2(math.e)
VMEM_LIMIT = 56 * 1024 * 1024

ROW_TILES = (512, 256, 128)
GLA_CHUNK = 128
GLA_BASES = (128, 16)
GLA_BASE_LIMIT = 100.0
GLA_HEAD_BLOCK = 4
DA_KEY_BLOCKS = (1024, 512, 256, 128)
DA_QUERY_BLOCKS = (1024, 512, 256, 128)
DA_BOUND_LIMIT = 50.0
DA_BOUND_SLACK = 1.01
LOW_PAD = LANES


def _nt(a, b):
    return lax.dot_general(a, b, (((1,), (1,)), ((), ())), preferred_element_type=F32)


def _tn(a, b):
    return lax.dot_general(a, b, (((0,), (0,)), ((), ())), preferred_element_type=F32)


def _mm(a, b):
    return jnp.dot(a, b, preferred_element_type=F32)


def _rms(x, gain):
    return x * lax.rsqrt(jnp.mean(x * x, axis=-1, keepdims=True) + EPS) * gain


def _params(*sem):
    return pltpu.CompilerParams(dimension_semantics=sem, vmem_limit_bytes=VMEM_LIMIT)


def _resident(shape):
    nd = len(shape)
    return pl.BlockSpec(shape, lambda *_: (0,) * nd, pipeline_mode=pl.Buffered(1))


def _largest_divisor(n, candidates):
    for cand in candidates:
        if n % cand == 0:
            return cand
    raise ValueError(f"no tile in {candidates} divides {n}")


def _mod_kernel(c_ref, w_ref, b_ref, o_ref):
    c = c_ref[...]
    s = (c * jax.nn.sigmoid(c)).astype(BF16)
    o_ref[...] = _mm(s, w_ref[...].astype(BF16)) + b_ref[...]


def _mod_call(cc, w_mod, b_mod):
    rows, d = cc.shape
    n = w_mod.shape[1]
    bn = _largest_divisor(n, (1024, 512, 256, 128))
    return pl.pallas_call(
        _mod_kernel,
        out_shape=jax.ShapeDtypeStruct((rows, n), F32),
        grid=(n // bn,),
        in_specs=[pl.BlockSpec((rows, d), lambda j: (0, 0)),
                  pl.BlockSpec((d, bn), lambda j: (0, j)),
                  pl.BlockSpec((1, bn), lambda j: (0, j))],
        out_specs=pl.BlockSpec((rows, bn), lambda j: (0, j)),
        compiler_params=_params("arbitrary"),
        name="mod",
    )(cc, w_mod, b_mod.reshape(1, n))


_OFF_Q = 0
_OFF_K = _OFF_Q + DA_HEADS * 2 * DA_DH
_OFF_V = _OFF_K + DA_HEADS * 2 * DA_DH
_OFF_GQ = _OFF_V + DA_HEADS * DA_DV
_OFF_GK = _OFF_GQ + GLA_HEADS * GLA_DK
_OFF_GV = _OFF_GK + GLA_HEADS * GLA_DK
_OFF_LOW = _OFF_GV + GLA_HEADS * GLA_DV
_OFF_GOUT = _OFF_LOW + LOW_PAD
_OFF_MERGE = _OFF_GOUT + GLA_HEADS * GLA_DV
_N_GATE = 2 * GLA_HEADS * GLA_DK


def _proj_kernel(x_ref, sh_ref, sc_ref, gain_ref, w_ref, wup_ref, bup_ref, cos_ref, sa_ref, sb_ref,
                 q_ref, k_ref, v_ref, gq_ref, gk_ref, gv_ref, gate_ref, gout_ref, mrg_ref):
    d = x_ref.shape[-1]
    h = (_rms(x_ref[...], gain_ref[...]) * (1.0 + sc_ref[...]) + sh_ref[...]).astype(BF16)
    cos, sa, sb = cos_ref[...], sa_ref[...], sb_ref[...]

    def roped(off, scale):
        acc = _mm(h, w_ref[:, off:off + d])
        for s in range(d // LANES):
            a = acc[:, s * LANES:(s + 1) * LANES]
            r = (a * cos + pltpu.roll(a, ROPE_HALF, 1) * sa
                 + pltpu.roll(a, LANES - ROPE_HALF, 1) * sb)
            yield s, r * scale

    low = _mm(h, w_ref[:, _OFF_LOW:_OFF_GOUT]).astype(BF16)
    z = _mm(low, wup_ref[...]) + bup_ref[...]
    log_sig = jnp.minimum(z, 0.0) - jnp.log1p(jnp.exp(-jnp.abs(z)))
    gate_ref[...] = log_sig / GLA_GATE_NORM
    mg = _mm(h, w_ref[:, _OFF_MERGE:_OFF_MERGE + 2 * d])
    mrg_ref[...] = jax.nn.sigmoid(mg).astype(mrg_ref.dtype)
    go = _mm(h, w_ref[:, _OFF_GOUT:_OFF_MERGE])
    gout_ref[...] = (go * jax.nn.sigmoid(go)).astype(gout_ref.dtype)
    for hd, slab in roped(_OFF_Q, DA_DH ** -0.5 * LOG2_E):
        q_ref[hd] = slab.T.astype(q_ref.dtype)
    for hd, slab in roped(_OFF_K, 1.0):
        k_ref[hd] = slab.astype(k_ref.dtype)
    v = _mm(h, w_ref[:, _OFF_V:_OFF_GQ])
    for hd in range(DA_HEADS):
        v_ref[hd] = v[:, hd * DA_DV:(hd + 1) * DA_DV].T.astype(v_ref.dtype)
    gq_ref[...] = (_mm(h, w_ref[:, _OFF_GQ:_OFF_GK]) * (GLA_DK ** -0.5)).astype(gq_ref.dtype)
    gk_ref[...] = _mm(h, w_ref[:, _OFF_GK:_OFF_GV]).astype(gk_ref.dtype)
    gv_ref[...] = _mm(h, w_ref[:, _OFF_GV:_OFF_LOW]).astype(gv_ref.dtype)


def _proj_call(x, shift, scale, gain, w_packed, w_up, b_up, cos, sa, sb):
    b, t, d = x.shape
    tm = _largest_divisor(t, ROW_TILES)
    tk = _da_key_block(t)
    per_block = tk // tm
    row = lambda width: pl.BlockSpec((None, tm, width), lambda bi, i: (bi, i, 0))
    vec = pl.BlockSpec((None, 1, d), lambda bi, i: (bi, 0, 0))
    tab = pl.BlockSpec((tm, LANES), lambda bi, i: (i, 0))
    widths = (GLA_HEADS * GLA_DK, GLA_HEADS * GLA_DK, GLA_HEADS * GLA_DV,
              _N_GATE, GLA_HEADS * GLA_DV, 2 * d)
    dtypes = (BF16, BF16, BF16, F32, BF16, BF16)
    dq = 2 * DA_DH
    head_shapes = [jax.ShapeDtypeStruct((b, DA_HEADS, dq, t), BF16),
                   jax.ShapeDtypeStruct((b, DA_HEADS, t, dq), BF16),
                   jax.ShapeDtypeStruct((b, DA_HEADS, t // tk, DA_DV, tk), BF16)]
    head_specs = [pl.BlockSpec((None, DA_HEADS, dq, tm), lambda bi, i: (bi, 0, 0, i)),
                  pl.BlockSpec((None, DA_HEADS, tm, dq), lambda bi, i: (bi, 0, i, 0)),
                  pl.BlockSpec((None, DA_HEADS, None, DA_DV, tm),
                               lambda bi, i: (bi, 0, i // per_block, 0, i % per_block))]
    return pl.pallas_call(
        _proj_kernel,
        out_shape=head_shapes + [jax.ShapeDtypeStruct((b, t, w), dt) for w, dt in zip(widths, dtypes)],
        grid=(b, t // tm),
        in_specs=[row(d), vec, vec, _resident((1, d)), _resident(w_packed.shape),
                  _resident(w_up.shape), _resident(b_up.shape), tab, tab, tab],
        out_specs=head_specs + [row(w) for w in widths],
        compiler_params=_params("parallel", "parallel"),
        name="proj",
    )(x, shift, scale, gain, w_packed, w_up, b_up, cos, sa, sb)


def _rope_tables(t):
    pos = jnp.arange(t)
    inv = ROPE_THETA ** (-jnp.arange(0, ROPE_AXIS_DIM, 2, dtype=F32) / ROPE_AXIS_DIM)
    ang_r = (pos // GRID_W).astype(F32)[:, None] * inv
    ang_c = (pos % GRID_W).astype(F32)[:, None] * inv
    zero = jnp.zeros_like(ang_r)
    cos = jnp.concatenate([jnp.cos(ang_r)] * 2 + [jnp.cos(ang_c)] * 2, axis=-1)
    sa = jnp.concatenate([zero, jnp.sin(ang_r), zero, jnp.sin(ang_c)], axis=-1)
    sb = jnp.concatenate([-jnp.sin(ang_r), zero, -jnp.sin(ang_c), zero], axis=-1)
    rep = LANES // DA_DH
    return tuple(jnp.tile(a, (1, rep)) for a in (cos, sa, sb))


def _identity_tables(t):
    return jnp.ones((t, LANES), F32), jnp.zeros((t, LANES), F32), jnp.zeros((t, LANES), F32)


def _level_ref(b, s, row, reverse):
    c = b.shape[0]
    if 2 * s >= SUBLANES:
        pieces = []
        for blk in range(c // (2 * s)):
            r = blk * 2 * s + s - 1 + (1 if reverse else 0)
            pieces.append(jnp.broadcast_to(b[r:r + 1, :], (2 * s, b.shape[1])))
        return jnp.concatenate(pieces, axis=0)
    pos = row % (2 * s)
    mid = s if reverse else s - 1
    ref = b
    for p in range(2 * s):
        if p != mid:
            ref = jnp.where(pos == p, pltpu.roll(b, (p - mid) % c, 0), ref)
    return ref


def _split3(x):
    hi = x.astype(BF16)
    r = x - hi.astype(F32)
    mid = r.astype(BF16)
    lo = (r - mid.astype(F32)).astype(BF16)
    return hi, mid, lo


def _chunk_cumsum(g, reverse):
    c = g.shape[0]
    ri = lax.broadcasted_iota(jnp.int32, (c, c), 0)
    ci = lax.broadcasted_iota(jnp.int32, (c, c), 1)
    tri = jnp.where((ci >= ri) if reverse else (ci <= ri), 1.0, 0.0).astype(BF16)
    hi, mid, lo = _split3(g)
    return _mm(tri, lo) + _mm(tri, mid) + _mm(tri, hi)


def _gla_tables(c):
    levels = int(math.log2(c))
    row = np.arange(c)
    side = np.zeros((levels, 2, 2, c, GLA_DK), np.float32)
    same = np.zeros((levels, c, c), np.float32)
    for lvl in range(levels):
        s = 1 << lvl
        upper = (row % (2 * s)) >= s
        for rev in range(2):
            q_side = ~upper if rev else upper
            side[lvl, rev, 0] = np.where(q_side, 0.0, NEG_BIG)[:, None]
            side[lvl, rev, 1] = np.where(q_side, NEG_BIG, 0.0)[:, None]
        same[lvl] = (row[:, None] // (2 * s)) == (row[None, :] // (2 * s))
    causal = np.zeros((len(GLA_BASES), 2, c, c), np.float32)
    for i, base in enumerate(GLA_BASES):
        blk = row // min(base, c)
        inside = blk[:, None] == blk[None, :]
        causal[i, 0] = inside & (row[None, :] <= row[:, None])
        causal[i, 1] = inside & (row[None, :] >= row[:, None])
    return jnp.asarray(side), jnp.asarray(same), jnp.asarray(causal)


def _base_offset(b, base, reverse):
    c, n = b.shape
    pieces = []
    for i in range(c // base):
        r = (i + 1) * base if reverse else i * base - 1
        edge = b[r:r + 1, :] if 0 <= r < c else jnp.zeros((1, n), F32)
        pieces.append(jnp.broadcast_to(edge, (base, n)))
    return b - jnp.concatenate(pieces, axis=0)


def _gla_direction(q, k, v, b, st_ref, side_ref, same_ref, causal_ref, reverse, tier, base_offset):
    c, dk = b.shape
    row = lax.broadcasted_iota(jnp.int32, (c, dk), 0)
    total = b[0:1, :] if reverse else b[c - 1:c, :]
    qf = q.astype(F32)
    kf = k.astype(F32)
    rev = 1 if reverse else 0

    st = st_ref[...]
    out = _nt((qf * jnp.exp2(b)).astype(BF16), st.astype(BF16))

    if tier is None:
        ri = lax.broadcasted_iota(jnp.int32, (c, c), 0)
        ci = lax.broadcasted_iota(jnp.int32, (c, c), 1)
        a = jnp.where(ri == ci, _nt(q, k), 0.0)
        s = 1
    else:
        qd = (qf * jnp.exp2(base_offset)).astype(BF16)
        kd = (kf * jnp.exp2(-base_offset)).astype(BF16)
        a = _nt(qd, kd) * causal_ref[tier, rev]
        s = min(GLA_BASES[tier], c)
    lvl = int(math.log2(s))
    while s < c:
        t = b - _level_ref(b, s, row, reverse)
        ql = (qf * jnp.exp2(t + side_ref[lvl, rev, 0])).astype(BF16)
        kl = (kf * jnp.exp2(side_ref[lvl, rev, 1] - t)).astype(BF16)
        al = _nt(ql, kl)
        a = a + (al if 2 * s == c else al * same_ref[lvl])
        s *= 2
        lvl += 1
    out = out + _mm(a.astype(BF16), v)

    kd = (kf * jnp.exp2(total - b)).astype(BF16)
    st_ref[...] = st * jnp.exp2(total) + _tn(v, kd)
    return out


def _gla_kernel(qf_ref, kf_ref, vf_ref, gf_ref, qb_ref, kb_ref, vb_ref, gb_ref, sf0_ref, sb0_ref,
                side_ref, same_ref, causal_ref, of_ref, ob_ref, sf1_ref, sb1_ref, sf_ref, sb_ref):
    i = pl.program_id(2)
    heads = sf_ref.shape[0]

    @pl.when(i == 0)
    def _():
        sf_ref[...] = sf0_ref[...]
        sb_ref[...] = sb0_ref[...]

    gf = gf_ref[...] * LOG2_E
    gb = gb_ref[...] * LOG2_E
    c = gf.shape[0]
    steepest = jnp.maximum(jnp.max(jnp.abs(gf)), jnp.max(jnp.abs(gb)))
    bf = _chunk_cumsum(gf, False)
    bb = _chunk_cumsum(gb, True)

    def run(tier):
        base = None if tier is None else min(GLA_BASES[tier], c)
        df = None if tier is None else _base_offset(bf, base, False)
        db = None if tier is None else _base_offset(bb, base, True)
        for h in range(heads):
            ks = slice(h * GLA_DK, (h + 1) * GLA_DK)
            vs = slice(h * GLA_DV, (h + 1) * GLA_DV)
            of_ref[:, vs] = _gla_direction(
                qf_ref[:, ks], kf_ref[:, ks], vf_ref[:, vs], bf[:, ks], sf_ref.at[h], side_ref, same_ref,
                causal_ref, False, tier, None if tier is None else df[:, ks]).astype(of_ref.dtype)
            ob_ref[:, vs] = _gla_direction(
                qb_ref[:, ks], kb_ref[:, ks], vb_ref[:, vs], bb[:, ks], sb_ref.at[h], side_ref, same_ref,
                causal_ref, True, tier, None if tier is None else db[:, ks]).astype(ob_ref.dtype)

    taken = None
    for tier, base in enumerate(GLA_BASES):
        fits = min(base, c) * steepest <= GLA_BASE_LIMIT
        pl.when(fits if taken is None else jnp.logical_and(fits, jnp.logical_not(taken)))(
            functools.partial(run, tier))
        taken = fits if taken is None else jnp.logical_or(taken, fits)
    pl.when(jnp.logical_not(taken))(functools.partial(run, None))

    @pl.when(i == pl.num_programs(2) - 1)
    def _():
        sf1_ref[...] = sf_ref[...]
        sb1_ref[...] = sb_ref[...]


def _gla_call(gq, gk, gv, gate, sf0, sb0):
    b, t, _ = gq.shape
    c = _largest_divisor(t, (GLA_CHUNK,))
    n = t // c
    hb = GLA_HEAD_BLOCK
    nhb = GLA_HEADS // hb
    fwd = lambda w, off=0: pl.BlockSpec((None, c, hb * w), lambda bi, h, i: (bi, i, h + off))
    bwd = lambda w, off=0: pl.BlockSpec((None, c, hb * w), lambda bi, h, i: (bi, n - 1 - i, h + off))
    state = pl.BlockSpec((None, hb, GLA_DV, GLA_DK), lambda bi, h, i: (bi, h, 0, 0))
    o_shape = jax.ShapeDtypeStruct((b, t, GLA_HEADS * GLA_DV), BF16)
    s_shape = jax.ShapeDtypeStruct((b, GLA_HEADS, GLA_DV, GLA_DK), F32)
    side, same, causal = _gla_tables(c)
    return pl.pallas_call(
        _gla_kernel,
        out_shape=[o_shape, o_shape, s_shape, s_shape],
        grid=(b, nhb, n),
        in_specs=[fwd(GLA_DK), fwd(GLA_DK), fwd(GLA_DV), fwd(GLA_DK),
                  bwd(GLA_DK), bwd(GLA_DK), bwd(GLA_DV), bwd(GLA_DK, nhb),
                  state, state, _resident(side.shape), _resident(same.shape), _resident(causal.shape)],
        out_specs=[fwd(GLA_DV), bwd(GLA_DV), state, state],
        scratch_shapes=[pltpu.VMEM((hb, GLA_DV, GLA_DK), F32), pltpu.VMEM((hb, GLA_DV, GLA_DK), F32)],
        compiler_params=_params("parallel", "parallel", "arbitrary"),
        name="gla",
    )(gq, gk, gv, gate, gq, gk, gv, gate, sf0, sb0, side, same, causal)


def _da_kernel(lq1_ref, lk1_ref, lq2_ref, lk2_ref, gain_ref, qt_ref, k_ref, vt_ref, ck_ref, cvt_ref,
               y_ref, kmax_ref, *, lam_init):
    tq = qt_ref.shape[-1]
    nk, tk, dq = k_ref.shape
    nck = ck_ref.shape[0]
    qt = qt_ref[...]
    sub = lax.broadcasted_iota(jnp.int32, qt.shape, 0)
    zero = jnp.zeros_like(qt)
    qm = jnp.concatenate([jnp.where(sub < DA_DH, qt, zero), jnp.where(sub < DA_DH, zero, qt)], axis=1)

    @pl.when(pl.program_id(2) == 0)
    def _():
        di = lax.broadcasted_iota(jnp.int32, (dq, dq), 0)
        ci = lax.broadcasted_iota(jnp.int32, (dq, dq), 1)
        same_comp = jnp.where((di < DA_DH) == (ci < DA_DH), 1.0, 0.0).astype(BF16)

        def block_max(kb, mx):
            nsq = _mm(kb * kb, same_comp)
            return jnp.maximum(mx, jnp.max(nsq.reshape(-1, SUBLANES, dq), axis=0))

        mx = lax.fori_loop(0, nk, lambda j, mx: block_max(k_ref[j], mx), jnp.zeros((SUBLANES, dq), F32))
        for j in range(nck):
            mx = block_max(ck_ref[j], mx)
        kmax_ref[...] = jnp.broadcast_to(jnp.max(mx, axis=0, keepdims=True), kmax_ref.shape)

    qf = qt.astype(F32)
    qsq = qf * qf
    kmsq = kmax_ref[...]
    bsq = jnp.concatenate([jnp.sum(qsq[:DA_DH], axis=0, keepdims=True) * kmsq[0:1, 0:1],
                           jnp.sum(qsq[DA_DH:], axis=0, keepdims=True) * kmsq[0:1, DA_DH:DA_DH + 1]],
                          axis=1)
    bound = jnp.sqrt(bsq) * DA_BOUND_SLACK
    in_range = jnp.max(bound) <= DA_BOUND_LIMIT

    def probs(kb, shift):
        s = _mm(kb, qm)
        p = jnp.exp2(s.reshape(-1, SUBLANES, s.shape[1]) - shift[None])
        return p.reshape(s.shape)

    def finish(acc, l):
        lam = (jnp.exp(jnp.sum(lq1_ref[...] * lk1_ref[...], keepdims=True))
               - jnp.exp(jnp.sum(lq2_ref[...] * lk2_ref[...], keepdims=True)) + lam_init)
        o = acc / l
        ot = o[:, :tq] - lam * o[:, tq:]
        y = _rms(ot.T, gain_ref[...]) * (1.0 - lam_init)
        y_ref[...] = y.astype(y_ref.dtype)

    @pl.when(in_range)
    def _():
        shift = jnp.broadcast_to(bound, (SUBLANES, 2 * tq))
        acc = jnp.zeros((DA_DV, 2 * tq), F32)
        l = jnp.zeros((1, 2 * tq), F32)
        blocks = [(k_ref[j], vt_ref[j]) for j in range(nk)] + [(ck_ref[j], cvt_ref[j]) for j in range(nck)]
        for kb, vb in blocks:
            p = probs(kb, shift)
            l = l + jnp.sum(p, axis=0, keepdims=True)
            acc = acc + _mm(vb, p.astype(BF16))
        finish(acc, l)

    @pl.when(jnp.logical_not(in_range))
    def _():
        def block(kb, vb, carry):
            m_prev, l, acc = carry
            s = _mm(kb, qm)
            m_new = jnp.maximum(m_prev, jnp.max(s, axis=0, keepdims=True))
            alpha = jnp.exp2(m_prev - m_new)
            p = jnp.exp2(s - m_new)
            return (m_new, alpha * l + jnp.sum(p, axis=0, keepdims=True),
                    alpha * acc + _mm(vb, p.astype(BF16)))

        carry = (jnp.full((1, 2 * tq), NEG_BIG, F32), jnp.zeros((1, 2 * tq), F32),
                 jnp.zeros((DA_DV, 2 * tq), F32))
        carry = lax.fori_loop(0, nk, lambda j, c: block(k_ref[j], vt_ref[j], c), carry)
        for j in range(nck):
            carry = block(ck_ref[j], cvt_ref[j], carry)
        finish(carry[2], carry[1])


def _da_key_block(t):
    return _largest_divisor(t, DA_KEY_BLOCKS)


def _da_call(lams, gain, qt, k5, vt5, ck5, cvt5, lam_init):
    b, h, dq, t = qt.shape
    tq = _largest_divisor(t, DA_QUERY_BLOCKS)
    small = pl.BlockSpec((1, DA_DH), lambda bi, hi, i: (0, 0))
    per_head = lambda a: pl.BlockSpec((None, None) + a.shape[2:], lambda bi, hi, i: (bi, hi, 0, 0, 0))
    return pl.pallas_call(
        functools.partial(_da_kernel, lam_init=lam_init),
        out_shape=jax.ShapeDtypeStruct((b, t, h * DA_DV), BF16),
        grid=(b, h, t // tq),
        in_specs=[small, small, small, small,
                  pl.BlockSpec((1, DA_DV), lambda bi, hi, i: (0, 0)),
                  pl.BlockSpec((None, None, dq, tq), lambda bi, hi, i: (bi, hi, 0, i)),
                  per_head(k5), per_head(vt5), per_head(ck5), per_head(cvt5)],
        out_specs=pl.BlockSpec((None, tq, DA_DV), lambda bi, hi, i: (bi, i, hi)),
        scratch_shapes=[pltpu.VMEM((SUBLANES, dq), F32)],
        compiler_params=_params("parallel", "parallel", "arbitrary"),
        name="da",
    )(*lams, gain, qt, k5, vt5, ck5, cvt5)


def _merge_kernel(x_ref, gt_ref, yda_ref, of_ref, ob_ref, gout_ref, mrg_ref, ggain_ref, pgain_ref,
                  wda_ref, wgla_ref, wout_ref, o_ref):
    d = x_ref.shape[-1]
    o_gla = of_ref[...].astype(F32) + ob_ref[...].astype(F32)
    ggain = ggain_ref[...]
    heads = [_rms(o_gla[:, h * GLA_DV:(h + 1) * GLA_DV], ggain) for h in range(GLA_HEADS)]
    y_gla = (jnp.concatenate(heads, axis=-1) * gout_ref[...].astype(F32)).astype(BF16)
    gates = mrg_ref[...].astype(F32)
    y = (gates[:, :d] * _mm(yda_ref[...], wda_ref[...])
         + gates[:, d:] * _mm(y_gla, wgla_ref[...]))
    z = _mm(y.astype(BF16), wout_ref[...])
    o_ref[...] = x_ref[...] + gt_ref[...] * _rms(z, pgain_ref[...])


def _merge_call(x, gt1, y_da, o_f, o_b, gout, mrg, ggain, pgain, w_da, w_gla, w_out):
    b, t, d = x.shape
    tm = _largest_divisor(t, ROW_TILES)
    row = lambda width: pl.BlockSpec((None, tm, width), lambda bi, i: (bi, i, 0))
    vec = pl.BlockSpec((None, 1, d), lambda bi, i: (bi, 0, 0))
    return pl.pallas_call(
        _merge_kernel,
        out_shape=jax.ShapeDtypeStruct((b, t, d), F32),
        grid=(b, t // tm),
        in_specs=[row(d), vec, row(d), row(d), row(d), row(d), row(2 * d),
                  _resident(ggain.shape), _resident(pgain.shape),
                  _resident(w_da.shape), _resident(w_gla.shape), _resident(w_out.shape)],
        out_specs=row(d),
        compiler_params=_params("parallel", "parallel"),
        name="merge",
    )(x, gt1, y_da, o_f, o_b, gout, mrg, ggain, pgain, w_da, w_gla, w_out)


def _mlp_kernel(x_ref, sh_ref, sc_ref, gt_ref, pre_ref, post_ref, w1_ref, w2_ref, o_ref):
    x = x_ref[...]
    h = (_rms(x, pre_ref[...]) * (1.0 + sc_ref[...]) + sh_ref[...]).astype(BF16)
    u = jnp.maximum(_mm(h, w1_ref[...]), 0.0)
    z = _mm((u * u).astype(BF16), w2_ref[...])
    o_ref[...] = x + gt_ref[...] * _rms(z, post_ref[...])


def _mlp_call(x, sh2, sc2, gt2, pre, post, w1, w2):
    b, t, d = x.shape
    tm = _largest_divisor(t, ROW_TILES)
    row = pl.BlockSpec((None, tm, d), lambda bi, i: (bi, i, 0))
    vec = pl.BlockSpec((None, 1, d), lambda bi, i: (bi, 0, 0))
    return pl.pallas_call(
        _mlp_kernel,
        out_shape=jax.ShapeDtypeStruct((b, t, d), F32),
        grid=(b, t // tm),
        in_specs=[row, vec, vec, vec, _resident(pre.shape), _resident(post.shape),
                  _resident(w1.shape), _resident(w2.shape)],
        out_specs=row,
        compiler_params=_params("parallel", "parallel"),
        name="mlp",
    )(x, sh2, sc2, gt2, pre, post, w1, w2)


def _pack_w_in(w):
    low0 = _OFF_LOW
    low1 = low0 + 2 * GLA_GATE_RANK
    pad = jnp.zeros((w.shape[0], LOW_PAD - 2 * GLA_GATE_RANK), w.dtype)
    return jnp.concatenate([w[:, :low0], w[:, low0:low1], pad, w[:, low1:]], axis=1).astype(BF16)


def _pack_gate_up(w_gate_up, b_gate_up):
    hk = GLA_HEADS * GLA_DK
    w = jnp.zeros((LOW_PAD, 2 * hk), F32)
    for z in range(2):
        w = w.at[z * GLA_GATE_RANK:(z + 1) * GLA_GATE_RANK, z * hk:(z + 1) * hk].set(w_gate_up[z])
    return w.astype(BF16), b_gate_up.reshape(1, 2 * hk).astype(F32)


def kernel(x, c, ctx, c_ctx, w_mod, b_mod, pre_norm1, w_in, w_gate_up, b_gate_up, lambda_q1, lambda_k1,
           lambda_q2, lambda_k2, da_head_norm, gla_head_norm, w_branch_da, w_branch_gla, w_out, post_norm1,
           pre_norm2, w_ff1, w_ff2, post_norm2):
    b, t, d = x.shape
    n_ctx = ctx.shape[1]
    assert w_in.shape[0] == 1, "single-layer block only"
    assert d == DA_HEADS * DA_DV == GLA_HEADS * GLA_DV and t % GRID_W == 0
    layer = 0
    lam_init = 0.8 - 0.6 * math.exp(-0.3 * layer)
    (w_mod, b_mod, pre_norm1, w_in, w_gate_up, b_gate_up, lambda_q1, lambda_k1, lambda_q2, lambda_k2,
     da_head_norm, gla_head_norm, w_branch_da, w_branch_gla, w_out, post_norm1, pre_norm2, w_ff1, w_ff2,
     post_norm2) = (a.reshape(a.shape[1:]) for a in (
         w_mod, b_mod, pre_norm1, w_in, w_gate_up, b_gate_up, lambda_q1, lambda_k1, lambda_q2, lambda_k2,
         da_head_norm, gla_head_norm, w_branch_da, w_branch_gla, w_out, post_norm1, pre_norm2, w_ff1,
         w_ff2, post_norm2))

    rows = -(-(b + 1) // SUBLANES) * SUBLANES
    cc = jnp.zeros((rows, d), F32).at[:b].set(c).at[b].set(c_ctx)
    mod = _mod_call(cc, w_mod, b_mod)
    lat = [mod[:b, i * d:(i + 1) * d].reshape(b, 1, d) for i in range(N_MOD)]
    cmod = [jnp.broadcast_to(mod[b, i * d:(i + 1) * d].reshape(1, 1, d), (b, 1, d)) for i in range(2)]
    sh1, sc1, gt1, sh2, sc2, gt2 = lat

    w_packed = _pack_w_in(w_in)
    w_up, b_up = _pack_gate_up(w_gate_up, b_gate_up)
    gain1 = pre_norm1.reshape(1, d)

    qt, k, vt5, gq, gk, gv, gate, gout, mrg = _proj_call(
        x, sh1, sc1, gain1, w_packed, w_up, b_up, *_rope_tables(t))
    _, ck, cvt5, _, cgk, cgv, cgate, _, _ = _proj_call(
        ctx, cmod[0], cmod[1], gain1, w_packed, w_up, b_up, *_identity_tables(n_ctx))

    zero_state = jnp.zeros((b, GLA_HEADS, GLA_DV, GLA_DK), F32)
    _, _, sf0, sb0 = _gla_call(jnp.zeros_like(cgk), cgk, cgv, cgate, zero_state, zero_state)
    o_f, o_b, _, _ = _gla_call(gq, gk, gv, gate, sf0, sb0)

    k5 = k.reshape(b, DA_HEADS, vt5.shape[2], vt5.shape[4], 2 * DA_DH)
    ck5 = ck.reshape(b, DA_HEADS, cvt5.shape[2], cvt5.shape[4], 2 * DA_DH)
    lams = [a.reshape(1, DA_DH) for a in (lambda_q1, lambda_k1, lambda_q2, lambda_k2)]
    y_da = _da_call(lams, da_head_norm.reshape(1, DA_DV), qt, k5, vt5, ck5, cvt5, lam_init)

    x1 = _merge_call(x, gt1, y_da, o_f, o_b, gout, mrg,
                     gla_head_norm.reshape(1, GLA_DV), post_norm1.reshape(1, d),
                     w_branch_da.astype(BF16), w_branch_gla.astype(BF16),
                     w_out.astype(BF16))
    return _mlp_call(x1, sh2, sc2, gt2, pre_norm2.reshape(1, d), post_norm2.reshape(1, d),
                     w_ff1.astype(BF16), w_ff2.astype(BF16))
```

```python
import functools
import math

import numpy as np
import jax
import jax.numpy as jnp
from jax import lax
from jax.experimental import pallas as pl
from jax.experimental.pallas import tpu as pltpu

F32 = jnp.float32
BF16 = jnp.bfloat16

GRID_W = 64
DA_HEADS = 8
DA_DH = 64
DA_DV = 2 * DA_DH
GLA_HEADS = 4
GLA_DK = 128
GLA_DV = 256
GLA_GATE_RANK = 16
GLA_GATE_NORM = 16.0
ROPE_THETA = 10000.0
ROPE_AXIS_DIM = DA_DH // 2
ROPE_HALF = ROPE_AXIS_DIM // 2
N_MOD = 6
EPS = 1e-6
NEG_BIG = -1e30

LANES = 128
SUBLANES = 8
LOG2_E = math.log2(math.e)
VMEM_LIMIT = 56 * 1024 * 1024

ROW_TILES = (512, 256, 128)
GLA_CHUNK = 128
GLA_BASES = (128, 16)
GLA_BASE_LIMIT = 100.0
GLA_HEAD_BLOCK = 4
DA_KEY_BLOCKS = (1024, 512, 256, 128)
DA_QUERY_BLOCKS = (1024, 512, 256, 128)
DA_BOUND_LIMIT = 50.0
DA_BOUND_SLACK = 1.01
LOW_PAD = LANES


def _nt(a, b):
    return lax.dot_general(a, b, (((1,), (1,)), ((), ())), preferred_element_type=F32)


def _tn(a, b):
    return lax.dot_general(a, b, (((0,), (0,)), ((), ())), preferred_element_type=F32)


def _mm(a, b):
    return jnp.dot(a, b, preferred_element_type=F32)


def _rms(x, gain):
    return x * lax.rsqrt(jnp.mean(x * x, axis=-1, keepdims=True) + EPS) * gain


def _params(*sem):
    return pltpu.CompilerParams(dimension_semantics=sem, vmem_limit_bytes=VMEM_LIMIT)


def _resident(shape):
    nd = len(shape)
    return pl.BlockSpec(shape, lambda *_: (0,) * nd, pipeline_mode=pl.Buffered(1))


def _largest_divisor(n, candidates):
    for cand in candidates:
        if n % cand == 0:
            return cand
    raise ValueError(f"no tile in {candidates} divides {n}")


def _mod_kernel(c_ref, w_ref, b_ref, o_ref):
    c = c_ref[...]
    s = (c * jax.nn.sigmoid(c)).astype(BF16)
    o_ref[...] = _mm(s, w_ref[...].astype(BF16)) + b_ref[...]


def _mod_call(cc, w_mod, b_mod):
    rows, d = cc.shape
    n = w_mod.shape[1]
    bn = _largest_divisor(n, (1024, 512, 256, 128))
    return pl.pallas_call(
        _mod_kernel,
        out_shape=jax.ShapeDtypeStruct((rows, n), F32),
        grid=(n // bn,),
        in_specs=[pl.BlockSpec((rows, d), lambda j: (0, 0)),
                  pl.BlockSpec((d, bn), lambda j: (0, j)),
                  pl.BlockSpec((1, bn), lambda j: (0, j))],
        out_specs=pl.BlockSpec((rows, bn), lambda j: (0, j)),
        compiler_params=_params("arbitrary"),
        name="mod",
    )(cc, w_mod, b_mod.reshape(1, n))


_OFF_Q = 0
_OFF_K = _OFF_Q + DA_HEADS * 2 * DA_DH
_OFF_V = _OFF_K + DA_HEADS * 2 * DA_DH
_OFF_GQ = _OFF_V + DA_HEADS * DA_DV
_OFF_GK = _OFF_GQ + GLA_HEADS * GLA_DK
_OFF_GV = _OFF_GK + GLA_HEADS * GLA_DK
_OFF_LOW = _OFF_GV + GLA_HEADS * GLA_DV
_TAIL_GOUT = 0
_TAIL_MERGE = _TAIL_GOUT + GLA_HEADS * GLA_DV
_N_GATE = 2 * GLA_HEADS * GLA_DK


def _proj_kernel(x_ref, sh_ref, sc_ref, gain_ref, w_ref, wlow_ref, wtail_ref, wup_ref, bup_ref,
                 cos_ref, sa_ref, sb_ref,
                 q_ref, k_ref, v_ref, gq_ref, gk_ref, gv_ref, gate_ref, gout_ref, mrg_ref):
    d = x_ref.shape[-1]
    h = (_rms(x_ref[...], gain_ref[...]) * (1.0 + sc_ref[...]) + sh_ref[...]).astype(BF16)
    cos, sa, sb = cos_ref[...], sa_ref[...], sb_ref[...]

    def roped(off, scale):
        acc = _mm(h, w_ref[:, off:off + d])
        for s in range(d // LANES):
            a = acc[:, s * LANES:(s + 1) * LANES]
            r = (a * cos + pltpu.roll(a, ROPE_HALF, 1) * sa
                 + pltpu.roll(a, LANES - ROPE_HALF, 1) * sb)
            yield s, r * scale

    low = _mm(h, wlow_ref[...]).astype(BF16)
    z = _mm(low, wup_ref[...]) + bup_ref[...]
    log_sig = jnp.minimum(z, 0.0) - jnp.log1p(jnp.exp(-jnp.abs(z)))
    gate_ref[...] = log_sig / GLA_GATE_NORM
    mg = _mm(h, wtail_ref[:, _TAIL_MERGE:_TAIL_MERGE + 2 * d])
    mrg_ref[...] = jax.nn.sigmoid(mg).astype(mrg_ref.dtype)
    go = _mm(h, wtail_ref[:, _TAIL_GOUT:_TAIL_MERGE])
    gout_ref[...] = (go * jax.nn.sigmoid(go)).astype(gout_ref.dtype)
    for hd, slab in roped(_OFF_Q, DA_DH ** -0.5 * LOG2_E):
        q_ref[hd] = slab.T.astype(q_ref.dtype)
    for hd, slab in roped(_OFF_K, 1.0):
        k_ref[hd] = slab.astype(k_ref.dtype)
    v = _mm(h, w_ref[:, _OFF_V:_OFF_GQ])
    for hd in range(DA_HEADS):
        v_ref[hd] = v[:, hd * DA_DV:(hd + 1) * DA_DV].T.astype(v_ref.dtype)
    gq_ref[...] = (_mm(h, w_ref[:, _OFF_GQ:_OFF_GK]) * (GLA_DK ** -0.5)).astype(gq_ref.dtype)
    gk_ref[...] = _mm(h, w_ref[:, _OFF_GK:_OFF_GV]).astype(gk_ref.dtype)
    gv_ref[...] = _mm(h, w_ref[:, _OFF_GV:_OFF_LOW]).astype(gv_ref.dtype)


def _proj_call(x, shift, scale, gain, w_pieces, w_up, b_up, cos, sa, sb):
    b, t, d = x.shape
    tm = _largest_divisor(t, ROW_TILES)
    tk = _da_key_block(t)
    per_block = tk // tm
    row = lambda width: pl.BlockSpec((None, tm, width), lambda bi, i: (bi, i, 0))
    vec = pl.BlockSpec((None, 1, d), lambda bi, i: (bi, 0, 0))
    tab = pl.BlockSpec((tm, LANES), lambda bi, i: (i, 0))
    widths = (GLA_HEADS * GLA_DK, GLA_HEADS * GLA_DK, GLA_HEADS * GLA_DV,
              _N_GATE, GLA_HEADS * GLA_DV, 2 * d)
    dtypes = (BF16, BF16, BF16, F32, BF16, BF16)
    dq = 2 * DA_DH
    head_shapes = [jax.ShapeDtypeStruct((b, DA_HEADS, dq, t), BF16),
                   jax.ShapeDtypeStruct((b, DA_HEADS, t, dq), BF16),
                   jax.ShapeDtypeStruct((b, DA_HEADS, t // tk, DA_DV, tk), BF16)]
    head_specs = [pl.BlockSpec((None, DA_HEADS, dq, tm), lambda bi, i: (bi, 0, 0, i)),
                  pl.BlockSpec((None, DA_HEADS, tm, dq), lambda bi, i: (bi, 0, i, 0)),
                  pl.BlockSpec((None, DA_HEADS, None, DA_DV, tm),
                               lambda bi, i: (bi, 0, i // per_block, 0, i % per_block))]
    return pl.pallas_call(
        _proj_kernel,
        out_shape=head_shapes + [jax.ShapeDtypeStruct((b, t, w), dt) for w, dt in zip(widths, dtypes)],
        grid=(b, t // tm),
        in_specs=[row(d), vec, vec, _resident((1, d))] + [_resident(w.shape) for w in w_pieces]
        + [_resident(w_up.shape), _resident(b_up.shape), tab, tab, tab],
        out_specs=head_specs + [row(w) for w in widths],
        compiler_params=_params("parallel", "parallel"),
        name="proj",
    )(x, shift, scale, gain, *w_pieces, w_up, b_up, cos, sa, sb)


def _rope_tables(t):
    pos = jnp.arange(t)
    inv = ROPE_THETA ** (-jnp.arange(0, ROPE_AXIS_DIM, 2, dtype=F32) / ROPE_AXIS_DIM)
    ang_r = (pos // GRID_W).astype(F32)[:, None] * inv
    ang_c = (pos % GRID_W).astype(F32)[:, None] * inv
    zero = jnp.zeros_like(ang_r)
    cos = jnp.concatenate([jnp.cos(ang_r)] * 2 + [jnp.cos(ang_c)] * 2, axis=-1)
    sa = jnp.concatenate([zero, jnp.sin(ang_r), zero, jnp.sin(ang_c)], axis=-1)
    sb = jnp.concatenate([-jnp.sin(ang_r), zero, -jnp.sin(ang_c), zero], axis=-1)
    rep = LANES // DA_DH
    return tuple(jnp.tile(a, (1, rep)) for a in (cos, sa, sb))


def _identity_tables(t):
    return jnp.ones((t, LANES), F32), jnp.zeros((t, LANES), F32), jnp.zeros((t, LANES), F32)


def _level_ref(b, s, row, reverse):
    c = b.shape[0]
    if 2 * s >= SUBLANES:
        pieces = []
        for blk in range(c // (2 * s)):
            r = blk * 2 * s + s - 1 + (1 if reverse else 0)
            pieces.append(jnp.broadcast_to(b[r:r + 1, :], (2 * s, b.shape[1])))
        return jnp.concatenate(pieces, axis=0)
    pos = row % (2 * s)
    mid = s if reverse else s - 1
    ref = b
    for p in range(2 * s):
        if p != mid:
            ref = jnp.where(pos == p, pltpu.roll(b, (p - mid) % c, 0), ref)
    return ref


def _split3(x):
    hi = x.astype(BF16)
    r = x - hi.astype(F32)
    mid = r.astype(BF16)
    lo = (r - mid.astype(F32)).astype(BF16)
    return hi, mid, lo


def _chunk_cumsum(g, reverse):
    c = g.shape[0]
    ri = lax.broadcasted_iota(jnp.int32, (c, c), 0)
    ci = lax.broadcasted_iota(jnp.int32, (c, c), 1)
    tri = jnp.where((ci >= ri) if reverse else (ci <= ri), 1.0, 0.0).astype(BF16)
    hi, mid, lo = _split3(g)
    return _mm(tri, lo) + _mm(tri, mid) + _mm(tri, hi)


def _gla_tables(c):
    levels = int(math.log2(c))
    row = np.arange(c)
    side = np.zeros((levels, 2, 2, c, GLA_DK), np.float32)
    same = np.zeros((levels, c, c), np.float32)
    for lvl in range(levels):
        s = 1 << lvl
        upper = (row % (2 * s)) >= s
        for rev in range(2):
            q_side = ~upper if rev else upper
            side[lvl, rev, 0] = np.where(q_side, 0.0, NEG_BIG)[:, None]
            side[lvl, rev, 1] = np.where(q_side, NEG_BIG, 0.0)[:, None]
        same[lvl] = (row[:, None] // (2 * s)) == (row[None, :] // (2 * s))
    causal = np.zeros((len(GLA_BASES), 2, c, c), np.float32)
    for i, base in enumerate(GLA_BASES):
        blk = row // min(base, c)
        inside = blk[:, None] == blk[None, :]
        causal[i, 0] = inside & (row[None, :] <= row[:, None])
        causal[i, 1] = inside & (row[None, :] >= row[:, None])
    return jnp.asarray(side), jnp.asarray(same), jnp.asarray(causal)


def _base_offset(b, base, reverse):
    c, n = b.shape
    pieces = []
    for i in range(c // base):
        r = (i + 1) * base if reverse else i * base - 1
        edge = b[r:r + 1, :] if 0 <= r < c else jnp.zeros((1, n), F32)
        pieces.append(jnp.broadcast_to(edge, (base, n)))
    return b - jnp.concatenate(pieces, axis=0)


def _gla_direction(q, k, v, b, st_ref, side_ref, same_ref, causal_ref, reverse, tier, base_offset):
    c, dk = b.shape
    row = lax.broadcasted_iota(jnp.int32, (c, dk), 0)
    total = b[0:1, :] if reverse else b[c - 1:c, :]
    qf = q.astype(F32)
    kf = k.astype(F32)
    rev = 1 if reverse else 0

    st = st_ref[...]
    out = _nt((qf * jnp.exp2(b)).astype(BF16), st.astype(BF16))

    if tier is None:
        ri = lax.broadcasted_iota(jnp.int32, (c, c), 0)
        ci = lax.broadcasted_iota(jnp.int32, (c, c), 1)
        a = jnp.where(ri == ci, _nt(q, k), 0.0)
        s = 1
    else:
        qd = (qf * jnp.exp2(base_offset)).astype(BF16)
        kd = (kf * jnp.exp2(-base_offset)).astype(BF16)
        a = _nt(qd, kd) * causal_ref[tier, rev]
        s = min(GLA_BASES[tier], c)
    lvl = int(math.log2(s))
    while s < c:
        t = b - _level_ref(b, s, row, reverse)
        ql = (qf * jnp.exp2(t + side_ref[lvl, rev, 0])).astype(BF16)
        kl = (kf * jnp.exp2(side_ref[lvl, rev, 1] - t)).astype(BF16)
        al = _nt(ql, kl)
        a = a + (al if 2 * s == c else al * same_ref[lvl])
        s *= 2
        lvl += 1
    out = out + _mm(a.astype(BF16), v)

    kd = (kf * jnp.exp2(total - b)).astype(BF16)
    st_ref[...] = st * jnp.exp2(total) + _tn(v, kd)
    return out


def _gla_kernel(qf_ref, kf_ref, vf_ref, gf_ref, qb_ref, kb_ref, vb_ref, gb_ref, sf0_ref, sb0_ref,
                side_ref, same_ref, causal_ref, of_ref, ob_ref, sf1_ref, sb1_ref, sf_ref, sb_ref):
    i = pl.program_id(2)
    heads = sf_ref.shape[0]

    @pl.when(i == 0)
    def _():
        sf_ref[...] = sf0_ref[...]
        sb_ref[...] = sb0_ref[...]

    gf = gf_ref[...] * LOG2_E
    gb = gb_ref[...] * LOG2_E
    c = gf.shape[0]
    steepest = jnp.maximum(jnp.max(jnp.abs(gf)), jnp.max(jnp.abs(gb)))
    bf = _chunk_cumsum(gf, False)
    bb = _chunk_cumsum(gb, True)

    def run(tier):
        base = None if tier is None else min(GLA_BASES[tier], c)
        df = None if tier is None else _base_offset(bf, base, False)
        db = None if tier is None else _base_offset(bb, base, True)
        for h in range(heads):
            ks = slice(h * GLA_DK, (h + 1) * GLA_DK)
            vs = slice(h * GLA_DV, (h + 1) * GLA_DV)
            of_ref[:, vs] = _gla_direction(
                qf_ref[:, ks], kf_ref[:, ks], vf_ref[:, vs], bf[:, ks], sf_ref.at[h], side_ref, same_ref,
                causal_ref, False, tier, None if tier is None else df[:, ks]).astype(of_ref.dtype)
            ob_ref[:, vs] = _gla_direction(
                qb_ref[:, ks], kb_ref[:, ks], vb_ref[:, vs], bb[:, ks], sb_ref.at[h], side_ref, same_ref,
                causal_ref, True, tier, None if tier is None else db[:, ks]).astype(ob_ref.dtype)

    taken = None
    for tier, base in enumerate(GLA_BASES):
        fits = min(base, c) * steepest <= GLA_BASE_LIMIT
        pl.when(fits if taken is None else jnp.logical_and(fits, jnp.logical_not(taken)))(
            functools.partial(run, tier))
        taken = fits if taken is None else jnp.logical_or(taken, fits)
    pl.when(jnp.logical_not(taken))(functools.partial(run, None))

    @pl.when(i == pl.num_programs(2) - 1)
    def _():
        sf1_ref[...] = sf_ref[...]
        sb1_ref[...] = sb_ref[...]


def _gla_call(gq, gk, gv, gate, sf0, sb0):
    b, t, _ = gq.shape
    c = _largest_divisor(t, (GLA_CHUNK,))
    n = t // c
    hb = GLA_HEAD_BLOCK
    nhb = GLA_HEADS // hb
    fwd = lambda w, off=0: pl.BlockSpec((None, c, hb * w), lambda bi, h, i: (bi, i, h + off))
    bwd = lambda w, off=0: pl.BlockSpec((None, c, hb * w), lambda bi, h, i: (bi, n - 1 - i, h + off))
    state = pl.BlockSpec((None, hb, GLA_DV, GLA_DK), lambda bi, h, i: (bi, h, 0, 0))
    o_shape = jax.ShapeDtypeStruct((b, t, GLA_HEADS * GLA_DV), BF16)
    s_shape = jax.ShapeDtypeStruct((b, GLA_HEADS, GLA_DV, GLA_DK), F32)
    side, same, causal = _gla_tables(c)
    return pl.pallas_call(
        _gla_kernel,
        out_shape=[o_shape, o_shape, s_shape, s_shape],
        grid=(b, nhb, n),
        in_specs=[fwd(GLA_DK), fwd(GLA_DK), fwd(GLA_DV), fwd(GLA_DK),
                  bwd(GLA_DK), bwd(GLA_DK), bwd(GLA_DV), bwd(GLA_DK, nhb),
                  state, state, _resident(side.shape), _resident(same.shape), _resident(causal.shape)],
        out_specs=[fwd(GLA_DV), bwd(GLA_DV), state, state],
        scratch_shapes=[pltpu.VMEM((hb, GLA_DV, GLA_DK), F32), pltpu.VMEM((hb, GLA_DV, GLA_DK), F32)],
        compiler_params=_params("parallel", "parallel", "arbitrary"),
        name="gla",
    )(gq, gk, gv, gate, gq, gk, gv, gate, sf0, sb0, side, same, causal)


def _da_kernel(lq1_ref, lk1_ref, lq2_ref, lk2_ref, gain_ref, qt_ref, k_ref, vt_ref, ck_ref, cvt_ref,
               y_ref, kmax_ref, *, lam_init):
    tq = qt_ref.shape[-1]
    nk, tk, dq = k_ref.shape
    nck = ck_ref.shape[0]
    qt = qt_ref[...]
    sub = lax.broadcasted_iota(jnp.int32, qt.shape, 0)
    zero = jnp.zeros_like(qt)
    qm = jnp.concatenate([jnp.where(sub < DA_DH, qt, zero), jnp.where(sub < DA_DH, zero, qt)], axis=1)

    @pl.when(pl.program_id(2) == 0)
    def _():
        di = lax.broadcasted_iota(jnp.int32, (dq, dq), 0)
        ci = lax.broadcasted_iota(jnp.int32, (dq, dq), 1)
        same_comp = jnp.where((di < DA_DH) == (ci < DA_DH), 1.0, 0.0).astype(BF16)

        def block_max(kb, mx):
            nsq = _mm(kb * kb, same_comp)
            return jnp.maximum(mx, jnp.max(nsq.reshape(-1, SUBLANES, dq), axis=0))

        mx = lax.fori_loop(0, nk, lambda j, mx: block_max(k_ref[j], mx), jnp.zeros((SUBLANES, dq), F32))
        for j in range(nck):
            mx = block_max(ck_ref[j], mx)
        kmax_ref[...] = jnp.broadcast_to(jnp.max(mx, axis=0, keepdims=True), kmax_ref.shape)

    qf = qt.astype(F32)
    qsq = qf * qf
    kmsq = kmax_ref[...]
    bsq = jnp.concatenate([jnp.sum(qsq[:DA_DH], axis=0, keepdims=True) * kmsq[0:1, 0:1],
                           jnp.sum(qsq[DA_DH:], axis=0, keepdims=True) * kmsq[0:1, DA_DH:DA_DH + 1]],
                          axis=1)
    bound = jnp.sqrt(bsq) * DA_BOUND_SLACK
    in_range = jnp.max(bound) <= DA_BOUND_LIMIT

    def probs(kb, shift):
        s = _mm(kb, qm)
        p = jnp.exp2(s.reshape(-1, SUBLANES, s.shape[1]) - shift[None])
        return p.reshape(s.shape)

    def finish(acc, l):
        lam = (jnp.exp(jnp.sum(lq1_ref[...] * lk1_ref[...], keepdims=True))
               - jnp.exp(jnp.sum(lq2_ref[...] * lk2_ref[...], keepdims=True)) + lam_init)
        o = acc / l
        ot = o[:, :tq] - lam * o[:, tq:]
        y = _rms(ot.T, gain_ref[...]) * (1.0 - lam_init)
        y_ref[...] = y.astype(y_ref.dtype)

    @pl.when(in_range)
    def _():
        shift = jnp.broadcast_to(bound, (SUBLANES, 2 * tq))
        acc = jnp.zeros((DA_DV, 2 * tq), F32)
        l = jnp.zeros((1, 2 * tq), F32)
        blocks = [(k_ref[j], vt_ref[j]) for j in range(nk)] + [(ck_ref[j], cvt_ref[j]) for j in range(nck)]
        for kb, vb in blocks:
            p = probs(kb, shift)
            l = l + jnp.sum(p, axis=0, keepdims=True)
            acc = acc + _mm(vb, p.astype(BF16))
        finish(acc, l)

    @pl.when(jnp.logical_not(in_range))
    def _():
        def block(kb, vb, carry):
            m_prev, l, acc = carry
            s = _mm(kb, qm)
            m_new = jnp.maximum(m_prev, jnp.max(s, axis=0, keepdims=True))
            alpha = jnp.exp2(m_prev - m_new)
            p = jnp.exp2(s - m_new)
            return (m_new, alpha * l + jnp.sum(p, axis=0, keepdims=True),
                    alpha * acc + _mm(vb, p.astype(BF16)))

        carry = (jnp.full((1, 2 * tq), NEG_BIG, F32), jnp.zeros((1, 2 * tq), F32),
                 jnp.zeros((DA_DV, 2 * tq), F32))
        carry = lax.fori_loop(0, nk, lambda j, c: block(k_ref[j], vt_ref[j], c), carry)
        for j in range(nck):
            carry = block(ck_ref[j], cvt_ref[j], carry)
        finish(carry[2], carry[1])


def _da_key_block(t):
    return _largest_divisor(t, DA_KEY_BLOCKS)


def _da_call(lams, gain, qt, k5, vt5, ck5, cvt5, lam_init):
    b, h, dq, t = qt.shape
    tq = _largest_divisor(t, DA_QUERY_BLOCKS)
    small = pl.BlockSpec((1, DA_DH), lambda bi, hi, i: (0, 0))
    per_head = lambda a: pl.BlockSpec((None, None) + a.shape[2:], lambda bi, hi, i: (bi, hi, 0, 0, 0))
    return pl.pallas_call(
        functools.partial(_da_kernel, lam_init=lam_init),
        out_shape=jax.ShapeDtypeStruct((b, t, h * DA_DV), BF16),
        grid=(b, h, t // tq),
        in_specs=[small, small, small, small,
                  pl.BlockSpec((1, DA_DV), lambda bi, hi, i: (0, 0)),
                  pl.BlockSpec((None, None, dq, tq), lambda bi, hi, i: (bi, hi, 0, i)),
                  per_head(k5), per_head(vt5), per_head(ck5), per_head(cvt5)],
        out_specs=pl.BlockSpec((None, tq, DA_DV), lambda bi, hi, i: (bi, i, hi)),
        scratch_shapes=[pltpu.VMEM((SUBLANES, dq), F32)],
        compiler_params=_params("parallel", "parallel", "arbitrary"),
        name="da",
    )(*lams, gain, qt, k5, vt5, ck5, cvt5)


def _merge_kernel(x_ref, gt_ref, yda_ref, of_ref, ob_ref, gout_ref, mrg_ref, ggain_ref, pgain_ref,
                  wda_ref, wgla_ref, wout_ref, o_ref):
    d = x_ref.shape[-1]
    o_gla = of_ref[...].astype(F32) + ob_ref[...].astype(F32)
    ggain = ggain_ref[...]
    heads = [_rms(o_gla[:, h * GLA_DV:(h + 1) * GLA_DV], ggain) for h in range(GLA_HEADS)]
    y_gla = (jnp.concatenate(heads, axis=-1) * gout_ref[...].astype(F32)).astype(BF16)
    gates = mrg_ref[...].astype(F32)
    y = (gates[:, :d] * _mm(yda_ref[...], wda_ref[...])
         + gates[:, d:] * _mm(y_gla, wgla_ref[...]))
    z = _mm(y.astype(BF16), wout_ref[...])
    o_ref[...] = x_ref[...] + gt_ref[...] * _rms(z, pgain_ref[...])


def _merge_call(x, gt1, y_da, o_f, o_b, gout, mrg, ggain, pgain, w_da, w_gla, w_out):
    b, t, d = x.shape
    tm = _largest_divisor(t, ROW_TILES)
    row = lambda width: pl.BlockSpec((None, tm, width), lambda bi, i: (bi, i, 0))
    vec = pl.BlockSpec((None, 1, d), lambda bi, i: (bi, 0, 0))
    return pl.pallas_call(
        _merge_kernel,
        out_shape=jax.ShapeDtypeStruct((b, t, d), F32),
        grid=(b, t // tm),
        in_specs=[row(d), vec, row(d), row(d), row(d), row(d), row(2 * d),
                  _resident(ggain.shape), _resident(pgain.shape),
                  _resident(w_da.shape), _resident(w_gla.shape), _resident(w_out.shape)],
        out_specs=row(d),
        compiler_params=_params("parallel", "parallel"),
        name="merge",
    )(x, gt1, y_da, o_f, o_b, gout, mrg, ggain, pgain, w_da, w_gla, w_out)


def _mlp_kernel(x_ref, sh_ref, sc_ref, gt_ref, pre_ref, post_ref, w1_ref, w2_ref, o_ref):
    x = x_ref[...]
    h = (_rms(x, pre_ref[...]) * (1.0 + sc_ref[...]) + sh_ref[...]).astype(BF16)
    u = jnp.maximum(_mm(h, w1_ref[...]), 0.0)
    z = _mm((u * u).astype(BF16), w2_ref[...])
    o_ref[...] = x + gt_ref[...] * _rms(z, post_ref[...])


def _mlp_call(x, sh2, sc2, gt2, pre, post, w1, w2):
    b, t, d = x.shape
    tm = _largest_divisor(t, ROW_TILES)
    row = pl.BlockSpec((None, tm, d), lambda bi, i: (bi, i, 0))
    vec = pl.BlockSpec((None, 1, d), lambda bi, i: (bi, 0, 0))
    return pl.pallas_call(
        _mlp_kernel,
        out_shape=jax.ShapeDtypeStruct((b, t, d), F32),
        grid=(b, t // tm),
        in_specs=[row, vec, vec, vec, _resident(pre.shape), _resident(post.shape),
                  _resident(w1.shape), _resident(w2.shape)],
        out_specs=row,
        compiler_params=_params("parallel", "parallel"),
        name="mlp",
    )(x, sh2, sc2, gt2, pre, post, w1, w2)


def _split_w_in(w):
    low0 = _OFF_LOW
    low1 = low0 + 2 * GLA_GATE_RANK
    low = jnp.pad(w[:, low0:low1], ((0, 0), (0, LOW_PAD - 2 * GLA_GATE_RANK)))
    return w[:, :low0].astype(BF16), low.astype(BF16), w[:, low1:].astype(BF16)


def _pack_gate_up(w_gate_up, b_gate_up):
    hk = GLA_HEADS * GLA_DK
    w = jnp.zeros((LOW_PAD, 2 * hk), F32)
    for z in range(2):
        w = w.at[z * GLA_GATE_RANK:(z + 1) * GLA_GATE_RANK, z * hk:(z + 1) * hk].set(w_gate_up[z])
    return w.astype(BF16), b_gate_up.reshape(1, 2 * hk).astype(F32)


def kernel(x, c, ctx, c_ctx, w_mod, b_mod, pre_norm1, w_in, w_gate_up, b_gate_up, lambda_q1, lambda_k1,
           lambda_q2, lambda_k2, da_head_norm, gla_head_norm, w_branch_da, w_branch_gla, w_out, post_norm1,
           pre_norm2, w_ff1, w_ff2, post_norm2):
    b, t, d = x.shape
    n_ctx = ctx.shape[1]
    assert w_in.shape[0] == 1, "single-layer block only"
    assert d == DA_HEADS * DA_DV == GLA_HEADS * GLA_DV and t % GRID_W == 0
    layer = 0
    lam_init = 0.8 - 0.6 * math.exp(-0.3 * layer)
    (w_mod, b_mod, pre_norm1, w_in, w_gate_up, b_gate_up, lambda_q1, lambda_k1, lambda_q2, lambda_k2,
     da_head_norm, gla_head_norm, w_branch_da, w_branch_gla, w_out, post_norm1, pre_norm2, w_ff1, w_ff2,
     post_norm2) = (a.reshape(a.shape[1:]) for a in (
         w_mod, b_mod, pre_norm1, w_in, w_gate_up, b_gate_up, lambda_q1, lambda_k1, lambda_q2, lambda_k2,
         da_head_norm, gla_head_norm, w_branch_da, w_branch_gla, w_out, post_norm1, pre_norm2, w_ff1,
         w_ff2, post_norm2))

    rows = -(-(b + 1) // SUBLANES) * SUBLANES
    cc = jnp.zeros((rows, d), F32).at[:b].set(c).at[b].set(c_ctx)
    mod = _mod_call(cc, w_mod, b_mod)
    lat = [mod[:b, i * d:(i + 1) * d].reshape(b, 1, d) for i in range(N_MOD)]
    cmod = [jnp.broadcast_to(mod[b, i * d:(i + 1) * d].reshape(1, 1, d), (b, 1, d)) for i in range(2)]
    sh1, sc1, gt1, sh2, sc2, gt2 = lat

    w_pieces = _split_w_in(w_in)
    w_up, b_up = _pack_gate_up(w_gate_up, b_gate_up)
    gain1 = pre_norm1.reshape(1, d)

    qt, k, vt5, gq, gk, gv, gate, gout, mrg = _proj_call(
        x, sh1, sc1, gain1, w_pieces, w_up, b_up, *_rope_tables(t))
    _, ck, cvt5, _, cgk, cgv, cgate, _, _ = _proj_call(
        ctx, cmod[0], cmod[1], gain1, w_pieces, w_up, b_up, *_identity_tables(n_ctx))

    zero_state = jnp.zeros((b, GLA_HEADS, GLA_DV, GLA_DK), F32)
    _, _, sf0, sb0 = _gla_call(jnp.zeros_like(cgk), cgk, cgv, cgate, zero_state, zero_state)
    o_f, o_b, _, _ = _gla_call(gq, gk, gv, gate, sf0, sb0)

    k5 = k.reshape(b, DA_HEADS, vt5.shape[2], vt5.shape[4], 2 * DA_DH)
    ck5 = ck.reshape(b, DA_HEADS, cvt5.shape[2], cvt5.shape[4], 2 * DA_DH)
    lams = [a.reshape(1, DA_DH) for a in (lambda_q1, lambda_k1, lambda_q2, lambda_k2)]
    y_da = _da_call(lams, da_head_norm.reshape(1, DA_DV), qt, k5, vt5, ck5, cvt5, lam_init)

    x1 = _merge_call(x, gt1, y_da, o_f, o_b, gout, mrg,
                     gla_head_norm.reshape(1, GLA_DV), post_norm1.reshape(1, d),
                     w_branch_da.astype(BF16), w_branch_gla.astype(BF16),
                     w_out.astype(BF16))
    return _mlp_call(x1, sh2, sc2, gt2, pre_norm2.reshape(1, d), post_norm2.reshape(1, d),
                     w_ff1.astype(BF16), w_ff2.astype(BF16))
```

```python
import functools
import math

import numpy as np
import jax
import jax.numpy as jnp
from jax import lax
from jax.experimental import pallas as pl
from jax.experimental.pallas import tpu as pltpu

F32 = jnp.float32
BF16 = jnp.bfloat16

GRID_W = 64
DA_HEADS = 8
DA_DH = 64
DA_DV = 2 * DA_DH
GLA_HEADS = 4
GLA_DK = 128
GLA_DV = 256
GLA_GATE_RANK = 16
GLA_GATE_NORM = 16.0
ROPE_THETA = 10000.0
ROPE_AXIS_DIM = DA_DH // 2
ROPE_HALF = ROPE_AXIS_DIM // 2
N_MOD = 6
EPS = 1e-6
NEG_BIG = -1e30

LANES = 128
SUBLANES = 8
LOG2_E = math.log2(math.e)
VMEM_LIMIT = 56 * 1024 * 1024

ROW_TILES = (512, 256, 128)
GLA_CHUNK = 128
GLA_BASES = (128, 16)
GLA_BASE_LIMIT = 100.0
GLA_HEAD_BLOCK = 4
DA_KEY_BLOCKS = (1024, 512, 256, 128)
DA_QUERY_BLOCKS = (1024, 512, 256, 128)
DA_BOUND_LIMIT = 50.0
DA_BOUND_SLACK = 1.01
LOW_PAD = LANES


def _nt(a, b):
    return lax.dot_general(a, b, (((1,), (1,)), ((), ())), preferred_element_type=F32)


def _tn(a, b):
    return lax.dot_general(a, b, (((0,), (0,)), ((), ())), preferred_element_type=F32)


def _mm(a, b):
    return jnp.dot(a, b, preferred_element_type=F32)


def _rms(x, gain):
    return x * lax.rsqrt(jnp.mean(x * x, axis=-1, keepdims=True) + EPS) * gain


def _params(*sem):
    return pltpu.CompilerParams(dimension_semantics=sem, vmem_limit_bytes=VMEM_LIMIT)


def _resident(shape):
    nd = len(shape)
    return pl.BlockSpec(shape, lambda *_: (0,) * nd, pipeline_mode=pl.Buffered(1))


def _largest_divisor(n, candidates):
    for cand in candidates:
        if n % cand == 0:
            return cand
    raise ValueError(f"no tile in {candidates} divides {n}")


def _mod_kernel(c_ref, w_ref, b_ref, o_ref):
    c = c_ref[...]
    s = (c * jax.nn.sigmoid(c)).astype(BF16)
    o_ref[...] = _mm(s, w_ref[...].astype(BF16)) + b_ref[...]


def _mod_call(cc, w_mod, b_mod):
    rows, d = cc.shape
    n = w_mod.shape[1]
    bn = _largest_divisor(n, (1024, 512, 256, 128))
    return pl.pallas_call(
        _mod_kernel,
        out_shape=jax.ShapeDtypeStruct((rows, n), F32),
        grid=(n // bn,),
        in_specs=[pl.BlockSpec((rows, d), lambda j: (0, 0)),
                  pl.BlockSpec((d, bn), lambda j: (0, j)),
                  pl.BlockSpec((1, bn), lambda j: (0, j))],
        out_specs=pl.BlockSpec((rows, bn), lambda j: (0, j)),
        compiler_params=_params("arbitrary"),
        name="mod",
    )(cc, w_mod, b_mod.reshape(1, n))


_OFF_Q = 0
_OFF_K = _OFF_Q + DA_HEADS * 2 * DA_DH
_OFF_V = _OFF_K + DA_HEADS * 2 * DA_DH
_OFF_GQ = _OFF_V + DA_HEADS * DA_DV
_OFF_GK = _OFF_GQ + GLA_HEADS * GLA_DK
_OFF_GV = _OFF_GK + GLA_HEADS * GLA_DK
_OFF_LOW = _OFF_GV + GLA_HEADS * GLA_DV
_TAIL_GOUT = 0
_TAIL_MERGE = _TAIL_GOUT + GLA_HEADS * GLA_DV
_N_GATE = 2 * GLA_HEADS * GLA_DK


def _proj_kernel(x_ref, sh_ref, sc_ref, gain_ref, w_ref, wlow_ref, wtail_ref, wup_ref, bup_ref,
                 cos_ref, sa_ref, sb_ref,
                 q_ref, k_ref, v_ref, gq_ref, gk_ref, gv_ref, gate_ref, gout_ref, mrg_ref):
    d = x_ref.shape[-1]
    h = (_rms(x_ref[...], gain_ref[...]) * (1.0 + sc_ref[...]) + sh_ref[...]).astype(BF16)
    cos, sa, sb = cos_ref[...], sa_ref[...], sb_ref[...]

    def roped(off, scale):
        acc = _mm(h, w_ref[:, off:off + d])
        for s in range(d // LANES):
            a = acc[:, s * LANES:(s + 1) * LANES]
            r = (a * cos + pltpu.roll(a, ROPE_HALF, 1) * sa
                 + pltpu.roll(a, LANES - ROPE_HALF, 1) * sb)
            yield s, r * scale

    low = _mm(h, wlow_ref[...]).astype(BF16)
    z = _mm(low, wup_ref[...]) + bup_ref[...]
    log_sig = jnp.minimum(z, 0.0) - jnp.log1p(jnp.exp(-jnp.abs(z)))
    gate_ref[...] = log_sig / GLA_GATE_NORM
    mg = _mm(h, wtail_ref[:, _TAIL_MERGE:_TAIL_MERGE + 2 * d])
    mrg_ref[...] = jax.nn.sigmoid(mg).astype(mrg_ref.dtype)
    go = _mm(h, wtail_ref[:, _TAIL_GOUT:_TAIL_MERGE])
    gout_ref[...] = (go * jax.nn.sigmoid(go)).astype(gout_ref.dtype)
    for hd, slab in roped(_OFF_Q, DA_DH ** -0.5 * LOG2_E):
        q_ref[hd] = slab.T.astype(q_ref.dtype)
    for hd, slab in roped(_OFF_K, 1.0):
        k_ref[hd] = slab.astype(k_ref.dtype)
    v = _mm(h, w_ref[:, _OFF_V:_OFF_GQ])
    for hd in range(DA_HEADS):
        v_ref[hd] = v[:, hd * DA_DV:(hd + 1) * DA_DV].T.astype(v_ref.dtype)
    gq_ref[...] = (_mm(h, w_ref[:, _OFF_GQ:_OFF_GK]) * (GLA_DK ** -0.5)).astype(gq_ref.dtype)
    gk_ref[...] = _mm(h, w_ref[:, _OFF_GK:_OFF_GV]).astype(gk_ref.dtype)
    gv_ref[...] = _mm(h, w_ref[:, _OFF_GV:_OFF_LOW]).astype(gv_ref.dtype)


def _proj_call(x, shift, scale, gain, w_pieces, w_up, b_up, cos, sa, sb):
    b, t, d = x.shape
    tm = _largest_divisor(t, ROW_TILES)
    tk = _da_key_block(t)
    per_block = tk // tm
    row = lambda width: pl.BlockSpec((None, tm, width), lambda bi, i: (bi, i, 0))
    vec = pl.BlockSpec((None, 1, d), lambda bi, i: (bi, 0, 0))
    tab = pl.BlockSpec((tm, LANES), lambda bi, i: (i, 0))
    widths = (GLA_HEADS * GLA_DK, GLA_HEADS * GLA_DK, GLA_HEADS * GLA_DV,
              _N_GATE, GLA_HEADS * GLA_DV, 2 * d)
    dtypes = (BF16, BF16, BF16, F32, BF16, BF16)
    dq = 2 * DA_DH
    head_shapes = [jax.ShapeDtypeStruct((b, DA_HEADS, dq, t), BF16),
                   jax.ShapeDtypeStruct((b, DA_HEADS, t, dq), BF16),
                   jax.ShapeDtypeStruct((b, DA_HEADS, t // tk, DA_DV, tk), BF16)]
    head_specs = [pl.BlockSpec((None, DA_HEADS, dq, tm), lambda bi, i: (bi, 0, 0, i)),
                  pl.BlockSpec((None, DA_HEADS, tm, dq), lambda bi, i: (bi, 0, i, 0)),
                  pl.BlockSpec((None, DA_HEADS, None, DA_DV, tm),
                               lambda bi, i: (bi, 0, i // per_block, 0, i % per_block))]
    return pl.pallas_call(
        _proj_kernel,
        out_shape=head_shapes + [jax.ShapeDtypeStruct((b, t, w), dt) for w, dt in zip(widths, dtypes)],
        grid=(b, t // tm),
        in_specs=[row(d), vec, vec, _resident((1, d))] + [_resident(w.shape) for w in w_pieces]
        + [_resident(w_up.shape), _resident(b_up.shape), tab, tab, tab],
        out_specs=head_specs + [row(w) for w in widths],
        compiler_params=_params("parallel", "parallel"),
        name="proj",
    )(x, shift, scale, gain, *w_pieces, w_up, b_up, cos, sa, sb)


def _rope_tables(t):
    pos = jnp.arange(t)
    inv = ROPE_THETA ** (-jnp.arange(0, ROPE_AXIS_DIM, 2, dtype=F32) / ROPE_AXIS_DIM)
    ang_r = (pos // GRID_W).astype(F32)[:, None] * inv
    ang_c = (pos % GRID_W).astype(F32)[:, None] * inv
    zero = jnp.zeros_like(ang_r)
    cos = jnp.concatenate([jnp.cos(ang_r)] * 2 + [jnp.cos(ang_c)] * 2, axis=-1)
    sa = jnp.concatenate([zero, jnp.sin(ang_r), zero, jnp.sin(ang_c)], axis=-1)
    sb = jnp.concatenate([-jnp.sin(ang_r), zero, -jnp.sin(ang_c), zero], axis=-1)
    rep = LANES // DA_DH
    return tuple(jnp.tile(a, (1, rep)) for a in (cos, sa, sb))


def _identity_tables(t):
    return jnp.ones((t, LANES), F32), jnp.zeros((t, LANES), F32), jnp.zeros((t, LANES), F32)


def _level_ref(b, s, row, reverse):
    c = b.shape[0]
    if 2 * s >= SUBLANES:
        pieces = []
        for blk in range(c // (2 * s)):
            r = blk * 2 * s + s - 1 + (1 if reverse else 0)
            pieces.append(jnp.broadcast_to(b[r:r + 1, :], (2 * s, b.shape[1])))
        return jnp.concatenate(pieces, axis=0)
    pos = row % (2 * s)
    mid = s if reverse else s - 1
    ref = b
    for p in range(2 * s):
        if p != mid:
            ref = jnp.where(pos == p, pltpu.roll(b, (p - mid) % c, 0), ref)
    return ref


def _split3(x):
    hi = x.astype(BF16)
    r = x - hi.astype(F32)
    mid = r.astype(BF16)
    lo = (r - mid.astype(F32)).astype(BF16)
    return hi, mid, lo


def _chunk_cumsum(g, reverse):
    c = g.shape[0]
    ri = lax.broadcasted_iota(jnp.int32, (c, c), 0)
    ci = lax.broadcasted_iota(jnp.int32, (c, c), 1)
    tri = jnp.where((ci >= ri) if reverse else (ci <= ri), 1.0, 0.0).astype(BF16)
    hi, mid, lo = _split3(g)
    return _mm(tri, lo) + _mm(tri, mid) + _mm(tri, hi)


def _gla_tables(c):
    levels = int(math.log2(c))
    row = np.arange(c)
    side = np.zeros((levels, 2, 2, c, GLA_DK), np.float32)
    same = np.zeros((levels, c, c), np.float32)
    for lvl in range(levels):
        s = 1 << lvl
        upper = (row % (2 * s)) >= s
        for rev in range(2):
            q_side = ~upper if rev else upper
            side[lvl, rev, 0] = np.where(q_side, 0.0, NEG_BIG)[:, None]
            side[lvl, rev, 1] = np.where(q_side, NEG_BIG, 0.0)[:, None]
        same[lvl] = (row[:, None] // (2 * s)) == (row[None, :] // (2 * s))
    causal = np.zeros((len(GLA_BASES), 2, c, c), np.float32)
    for i, base in enumerate(GLA_BASES):
        blk = row // min(base, c)
        inside = blk[:, None] == blk[None, :]
        causal[i, 0] = inside & (row[None, :] <= row[:, None])
        causal[i, 1] = inside & (row[None, :] >= row[:, None])
    return jnp.asarray(side), jnp.asarray(same), jnp.asarray(causal)


def _base_offset(b, base, reverse):
    c, n = b.shape
    pieces = []
    for i in range(c // base):
        r = (i + 1) * base if reverse else i * base - 1
        edge = b[r:r + 1, :] if 0 <= r < c else jnp.zeros((1, n), F32)
        pieces.append(jnp.broadcast_to(edge, (base, n)))
    return b - jnp.concatenate(pieces, axis=0)


def _gla_direction(q, k, v, b, st_ref, side_ref, same_ref, causal_ref, reverse, tier, base_offset):
    c, dk = b.shape
    row = lax.broadcasted_iota(jnp.int32, (c, dk), 0)
    total = b[0:1, :] if reverse else b[c - 1:c, :]
    qf = q.astype(F32)
    kf = k.astype(F32)
    rev = 1 if reverse else 0

    st = st_ref[...]
    out = _nt((qf * jnp.exp2(b)).astype(BF16), st.astype(BF16))

    if tier is None:
        ri = lax.broadcasted_iota(jnp.int32, (c, c), 0)
        ci = lax.broadcasted_iota(jnp.int32, (c, c), 1)
        a = jnp.where(ri == ci, _nt(q, k), 0.0)
        s = 1
    else:
        qd = (qf * jnp.exp2(base_offset)).astype(BF16)
        kd = (kf * jnp.exp2(-base_offset)).astype(BF16)
        a = _nt(qd, kd) * causal_ref[tier, rev]
        s = min(GLA_BASES[tier], c)
    lvl = int(math.log2(s))
    while s < c:
        t = b - _level_ref(b, s, row, reverse)
        ql = (qf * jnp.exp2(t + side_ref[lvl, rev, 0])).astype(BF16)
        kl = (kf * jnp.exp2(side_ref[lvl, rev, 1] - t)).astype(BF16)
        al = _nt(ql, kl)
        a = a + (al if 2 * s == c else al * same_ref[lvl])
        s *= 2
        lvl += 1
    out = out + _mm(a.astype(BF16), v)

    kd = (kf * jnp.exp2(total - b)).astype(BF16)
    st_ref[...] = st * jnp.exp2(total) + _tn(v, kd)
    return out


def _gla_kernel(qf_ref, kf_ref, vf_ref, gf_ref, qb_ref, kb_ref, vb_ref, gb_ref, sf0_ref, sb0_ref,
                side_ref, same_ref, causal_ref, of_ref, ob_ref, sf1_ref, sb1_ref, sf_ref, sb_ref):
    i = pl.program_id(2)
    heads = sf_ref.shape[0]

    @pl.when(i == 0)
    def _():
        sf_ref[...] = sf0_ref[...]
        sb_ref[...] = sb0_ref[...]

    gf = gf_ref[...] * LOG2_E
    gb = gb_ref[...] * LOG2_E
    c = gf.shape[0]
    steepest = jnp.maximum(jnp.max(jnp.abs(gf)), jnp.max(jnp.abs(gb)))
    bf = _chunk_cumsum(gf, False)
    bb = _chunk_cumsum(gb, True)

    def run(tier):
        base = None if tier is None else min(GLA_BASES[tier], c)
        df = None if tier is None else _base_offset(bf, base, False)
        db = None if tier is None else _base_offset(bb, base, True)
        for h in range(heads):
            ks = slice(h * GLA_DK, (h + 1) * GLA_DK)
            vs = slice(h * GLA_DV, (h + 1) * GLA_DV)
            of_ref[:, vs] = _gla_direction(
                qf_ref[:, ks], kf_ref[:, ks], vf_ref[:, vs], bf[:, ks], sf_ref.at[h], side_ref, same_ref,
                causal_ref, False, tier, None if tier is None else df[:, ks]).astype(of_ref.dtype)
            ob_ref[:, vs] = _gla_direction(
                qb_ref[:, ks], kb_ref[:, ks], vb_ref[:, vs], bb[:, ks], sb_ref.at[h], side_ref, same_ref,
                causal_ref, True, tier, None if tier is None else db[:, ks]).astype(ob_ref.dtype)

    taken = None
    for tier, base in enumerate(GLA_BASES):
        fits = min(base, c) * steepest <= GLA_BASE_LIMIT
        pl.when(fits if taken is None else jnp.logical_and(fits, jnp.logical_not(taken)))(
            functools.partial(run, tier))
        taken = fits if taken is None else jnp.logical_or(taken, fits)
    pl.when(jnp.logical_not(taken))(functools.partial(run, None))

    @pl.when(i == pl.num_programs(2) - 1)
    def _():
        sf1_ref[...] = sf_ref[...]
        sb1_ref[...] = sb_ref[...]


def _gla_call(gq, gk, gv, gate, sf0, sb0):
    b, t, _ = gq.shape
    c = _largest_divisor(t, (GLA_CHUNK,))
    n = t // c
    hb = GLA_HEAD_BLOCK
    nhb = GLA_HEADS // hb
    fwd = lambda w, off=0: pl.BlockSpec((None, c, hb * w), lambda bi, h, i: (bi, i, h + off))
    bwd = lambda w, off=0: pl.BlockSpec((None, c, hb * w), lambda bi, h, i: (bi, n - 1 - i, h + off))
    state = pl.BlockSpec((None, hb, GLA_DV, GLA_DK), lambda bi, h, i: (bi, h, 0, 0))
    o_shape = jax.ShapeDtypeStruct((b, t, GLA_HEADS * GLA_DV), BF16)
    s_shape = jax.ShapeDtypeStruct((b, GLA_HEADS, GLA_DV, GLA_DK), F32)
    side, same, causal = _gla_tables(c)
    return pl.pallas_call(
        _gla_kernel,
        out_shape=[o_shape, o_shape, s_shape, s_shape],
        grid=(b, nhb, n),
        in_specs=[fwd(GLA_DK), fwd(GLA_DK), fwd(GLA_DV), fwd(GLA_DK),
                  bwd(GLA_DK), bwd(GLA_DK), bwd(GLA_DV), bwd(GLA_DK, nhb),
                  state, state, _resident(side.shape), _resident(same.shape), _resident(causal.shape)],
        out_specs=[fwd(GLA_DV), bwd(GLA_DV), state, state],
        scratch_shapes=[pltpu.VMEM((hb, GLA_DV, GLA_DK), F32), pltpu.VMEM((hb, GLA_DV, GLA_DK), F32)],
        compiler_params=_params("parallel", "parallel", "arbitrary"),
        name="gla",
    )(gq, gk, gv, gate, gq, gk, gv, gate, sf0, sb0, side, same, causal)


def _da_kernel(lq1_ref, lk1_ref, lq2_ref, lk2_ref, gain_ref, qt_ref, k_ref, vt_ref, ck_ref, cvt_ref,
               y_ref, kmax_ref, *, lam_init):
    tq = qt_ref.shape[-1]
    nk, tk, dq = k_ref.shape
    nck = ck_ref.shape[0]
    qt = qt_ref[...]
    sub = lax.broadcasted_iota(jnp.int32, qt.shape, 0)
    zero = jnp.zeros_like(qt)
    qm = jnp.concatenate([jnp.where(sub < DA_DH, qt, zero), jnp.where(sub < DA_DH, zero, qt)], axis=1)

    @pl.when(pl.program_id(2) == 0)
    def _():
        di = lax.broadcasted_iota(jnp.int32, (dq, dq), 0)
        ci = lax.broadcasted_iota(jnp.int32, (dq, dq), 1)
        same_comp = jnp.where((di < DA_DH) == (ci < DA_DH), 1.0, 0.0).astype(BF16)

        def block_max(kb, mx):
            nsq = _mm(kb * kb, same_comp)
            return jnp.maximum(mx, jnp.max(nsq.reshape(-1, SUBLANES, dq), axis=0))

        mx = jnp.zeros((SUBLANES, dq), F32)
        for kb in [k_ref[j] for j in range(nk)] + [ck_ref[j] for j in range(nck)]:
            mx = block_max(kb, mx)
        kmax_ref[...] = jnp.broadcast_to(jnp.max(mx, axis=0, keepdims=True), kmax_ref.shape)

    qf = qt.astype(F32)
    qsq = qf * qf
    kmsq = kmax_ref[...]
    bsq = jnp.concatenate([jnp.sum(qsq[:DA_DH], axis=0, keepdims=True) * kmsq[0:1, 0:1],
                           jnp.sum(qsq[DA_DH:], axis=0, keepdims=True) * kmsq[0:1, DA_DH:DA_DH + 1]],
                          axis=1)
    bound = jnp.sqrt(bsq) * DA_BOUND_SLACK
    in_range = jnp.max(bound) <= DA_BOUND_LIMIT

    def probs(kb, shift):
        s = _mm(kb, qm)
        p = jnp.exp2(s.reshape(-1, SUBLANES, s.shape[1]) - shift[None])
        return p.reshape(s.shape)

    def finish(acc, l):
        lam = (jnp.exp(jnp.sum(lq1_ref[...] * lk1_ref[...], keepdims=True))
               - jnp.exp(jnp.sum(lq2_ref[...] * lk2_ref[...], keepdims=True)) + lam_init)
        o = acc / l
        ot = o[:, :tq] - lam * o[:, tq:]
        y = _rms(ot.T, gain_ref[...]) * (1.0 - lam_init)
        y_ref[...] = y.astype(y_ref.dtype)

    @pl.when(in_range)
    def _():
        shift = jnp.broadcast_to(bound, (SUBLANES, 2 * tq))
        acc = jnp.zeros((DA_DV, 2 * tq), F32)
        l = jnp.zeros((1, 2 * tq), F32)
        blocks = [(k_ref[j], vt_ref[j]) for j in range(nk)] + [(ck_ref[j], cvt_ref[j]) for j in range(nck)]
        for kb, vb in blocks:
            p = probs(kb, shift)
            l = l + jnp.sum(p, axis=0, keepdims=True)
            acc = acc + _mm(vb, p.astype(BF16))
        finish(acc, l)

    @pl.when(jnp.logical_not(in_range))
    def _():
        def block(kb, vb, carry):
            m_prev, l, acc = carry
            s = _mm(kb, qm)
            m_new = jnp.maximum(m_prev, jnp.max(s, axis=0, keepdims=True))
            alpha = jnp.exp2(m_prev - m_new)
            p = jnp.exp2(s - m_new)
            return (m_new, alpha * l + jnp.sum(p, axis=0, keepdims=True),
                    alpha * acc + _mm(vb, p.astype(BF16)))

        carry = (jnp.full((1, 2 * tq), NEG_BIG, F32), jnp.zeros((1, 2 * tq), F32),
                 jnp.zeros((DA_DV, 2 * tq), F32))
        carry = lax.fori_loop(0, nk, lambda j, c: block(k_ref[j], vt_ref[j], c), carry)
        for j in range(nck):
            carry = block(ck_ref[j], cvt_ref[j], carry)
        finish(carry[2], carry[1])


def _da_key_block(t):
    return _largest_divisor(t, DA_KEY_BLOCKS)


def _da_call(lams, gain, qt, k5, vt5, ck5, cvt5, lam_init):
    b, h, dq, t = qt.shape
    tq = _largest_divisor(t, DA_QUERY_BLOCKS)
    small = pl.BlockSpec((1, DA_DH), lambda bi, hi, i: (0, 0))
    per_head = lambda a: pl.BlockSpec((None, None) + a.shape[2:], lambda bi, hi, i: (bi, hi, 0, 0, 0))
    return pl.pallas_call(
        functools.partial(_da_kernel, lam_init=lam_init),
        out_shape=jax.ShapeDtypeStruct((b, t, h * DA_DV), BF16),
        grid=(b, h, t // tq),
        in_specs=[small, small, small, small,
                  pl.BlockSpec((1, DA_DV), lambda bi, hi, i: (0, 0)),
                  pl.BlockSpec((None, None, dq, tq), lambda bi, hi, i: (bi, hi, 0, i)),
                  per_head(k5), per_head(vt5), per_head(ck5), per_head(cvt5)],
        out_specs=pl.BlockSpec((None, tq, DA_DV), lambda bi, hi, i: (bi, i, hi)),
        scratch_shapes=[pltpu.VMEM((SUBLANES, dq), F32)],
        compiler_params=_params("parallel", "parallel", "arbitrary"),
        name="da",
    )(*lams, gain, qt, k5, vt5, ck5, cvt5)


def _merge_kernel(x_ref, gt_ref, yda_ref, of_ref, ob_ref, gout_ref, mrg_ref, ggain_ref, pgain_ref,
                  wda_ref, wgla_ref, wout_ref, o_ref):
    d = x_ref.shape[-1]
    o_gla = of_ref[...].astype(F32) + ob_ref[...].astype(F32)
    ggain = ggain_ref[...]
    heads = [_rms(o_gla[:, h * GLA_DV:(h + 1) * GLA_DV], ggain) for h in range(GLA_HEADS)]
    y_gla = (jnp.concatenate(heads, axis=-1) * gout_ref[...].astype(F32)).astype(BF16)
    gates = mrg_ref[...].astype(F32)
    y = (gates[:, :d] * _mm(yda_ref[...], wda_ref[...])
         + gates[:, d:] * _mm(y_gla, wgla_ref[...]))
    z = _mm(y.astype(BF16), wout_ref[...])
    o_ref[...] = x_ref[...] + gt_ref[...] * _rms(z, pgain_ref[...])


def _merge_call(x, gt1, y_da, o_f, o_b, gout, mrg, ggain, pgain, w_da, w_gla, w_out):
    b, t, d = x.shape
    tm = _largest_divisor(t, ROW_TILES)
    row = lambda width: pl.BlockSpec((None, tm, width), lambda bi, i: (bi, i, 0))
    vec = pl.BlockSpec((None, 1, d), lambda bi, i: (bi, 0, 0))
    return pl.pallas_call(
        _merge_kernel,
        out_shape=jax.ShapeDtypeStruct((b, t, d), F32),
        grid=(b, t // tm),
        in_specs=[row(d), vec, row(d), row(d), row(d), row(d), row(2 * d),
                  _resident(ggain.shape), _resident(pgain.shape),
                  _resident(w_da.shape), _resident(w_gla.shape), _resident(w_out.shape)],
        out_specs=row(d),
        compiler_params=_params("parallel", "parallel"),
        name="merge",
    )(x, gt1, y_da, o_f, o_b, gout, mrg, ggain, pgain, w_da, w_gla, w_out)


def _mlp_kernel(x_ref, sh_ref, sc_ref, gt_ref, pre_ref, post_ref, w1_ref, w2_ref, o_ref):
    x = x_ref[...]
    h = (_rms(x, pre_ref[...]) * (1.0 + sc_ref[...]) + sh_ref[...]).astype(BF16)
    u = jnp.maximum(_mm(h, w1_ref[...]), 0.0)
    z = _mm((u * u).astype(BF16), w2_ref[...])
    o_ref[...] = x + gt_ref[...] * _rms(z, post_ref[...])


def _mlp_call(x, sh2, sc2, gt2, pre, post, w1, w2):
    b, t, d = x.shape
    tm = _largest_divisor(t, ROW_TILES)
    row = pl.BlockSpec((None, tm, d), lambda bi, i: (bi, i, 0))
    vec = pl.BlockSpec((None, 1, d), lambda bi, i: (bi, 0, 0))
    return pl.pallas_call(
        _mlp_kernel,
        out_shape=jax.ShapeDtypeStruct((b, t, d), F32),
        grid=(b, t // tm),
        in_specs=[row, vec, vec, vec, _resident(pre.shape), _resident(post.shape),
                  _resident(w1.shape), _resident(w2.shape)],
        out_specs=row,
        compiler_params=_params("parallel", "parallel"),
        name="mlp",
    )(x, sh2, sc2, gt2, pre, post, w1, w2)


def _split_w_in(w):
    low0 = _OFF_LOW
    low1 = low0 + 2 * GLA_GATE_RANK
    low = jnp.pad(w[:, low0:low1], ((0, 0), (0, LOW_PAD - 2 * GLA_GATE_RANK)))
    return w[:, :low0].astype(BF16), low.astype(BF16), w[:, low1:].astype(BF16)


def _pack_gate_up(w_gate_up, b_gate_up):
    hk = GLA_HEADS * GLA_DK
    w = jnp.zeros((LOW_PAD, 2 * hk), F32)
    for z in range(2):
        w = w.at[z * GLA_GATE_RANK:(z + 1) * GLA_GATE_RANK, z * hk:(z + 1) * hk].set(w_gate_up[z])
    return w.astype(BF16), b_gate_up.reshape(1, 2 * hk).astype(F32)


def kernel(x, c, ctx, c_ctx, w_mod, b_mod, pre_norm1, w_in, w_gate_up, b_gate_up, lambda_q1, lambda_k1,
           lambda_q2, lambda_k2, da_head_norm, gla_head_norm, w_branch_da, w_branch_gla, w_out, post_norm1,
           pre_norm2, w_ff1, w_ff2, post_norm2):
    b, t, d = x.shape
    n_ctx = ctx.shape[1]
    assert w_in.shape[0] == 1, "single-layer block only"
    assert d == DA_HEADS * DA_DV == GLA_HEADS * GLA_DV and t % GRID_W == 0
    layer = 0
    lam_init = 0.8 - 0.6 * math.exp(-0.3 * layer)
    (w_mod, b_mod, pre_norm1, w_in, w_gate_up, b_gate_up, lambda_q1, lambda_k1, lambda_q2, lambda_k2,
     da_head_norm, gla_head_norm, w_branch_da, w_branch_gla, w_out, post_norm1, pre_norm2, w_ff1, w_ff2,
     post_norm2) = (a.reshape(a.shape[1:]) for a in (
         w_mod, b_mod, pre_norm1, w_in, w_gate_up, b_gate_up, lambda_q1, lambda_k1, lambda_q2, lambda_k2,
         da_head_norm, gla_head_norm, w_branch_da, w_branch_gla, w_out, post_norm1, pre_norm2, w_ff1,
         w_ff2, post_norm2))

    rows = -(-(b + 1) // SUBLANES) * SUBLANES
    cc = jnp.zeros((rows, d), F32).at[:b].set(c).at[b].set(c_ctx)
    mod = _mod_call(cc, w_mod, b_mod)
    lat = [mod[:b, i * d:(i + 1) * d].reshape(b, 1, d) for i in range(N_MOD)]
    cmod = [jnp.broadcast_to(mod[b, i * d:(i + 1) * d].reshape(1, 1, d), (b, 1, d)) for i in range(2)]
    sh1, sc1, gt1, sh2, sc2, gt2 = lat

    w_pieces = _split_w_in(w_in)
    w_up, b_up = _pack_gate_up(w_gate_up, b_gate_up)
    gain1 = pre_norm1.reshape(1, d)

    qt, k, vt5, gq, gk, gv, gate, gout, mrg = _proj_call(
        x, sh1, sc1, gain1, w_pieces, w_up, b_up, *_rope_tables(t))
    _, ck, cvt5, _, cgk, cgv, cgate, _, _ = _proj_call(
        ctx, cmod[0], cmod[1], gain1, w_pieces, w_up, b_up, *_identity_tables(n_ctx))

    zero_state = jnp.zeros((b, GLA_HEADS, GLA_DV, GLA_DK), F32)
    _, _, sf0, sb0 = _gla_call(jnp.zeros_like(cgk), cgk, cgv, cgate, zero_state, zero_state)
    o_f, o_b, _, _ = _gla_call(gq, gk, gv, gate, sf0, sb0)

    k5 = k.reshape(b, DA_HEADS, vt5.shape[2], vt5.shape[4], 2 * DA_DH)
    ck5 = ck.reshape(b, DA_HEADS, cvt5.shape[2], cvt5.shape[4], 2 * DA_DH)
    lams = [a.reshape(1, DA_DH) for a in (lambda_q1, lambda_k1, lambda_q2, lambda_k2)]
    y_da = _da_call(lams, da_head_norm.reshape(1, DA_DV), qt, k5, vt5, ck5, cvt5, lam_init)

    x1 = _merge_call(x, gt1, y_da, o_f, o_b, gout, mrg,
                     gla_head_norm.reshape(1, GLA_DV), post_norm1.reshape(1, d),
                     w_branch_da.astype(BF16), w_branch_gla.astype(BF16),
                     w_out.astype(BF16))
    return _mlp_call(x1, sh2, sc2, gt2, pre_norm2.reshape(1, d), post_norm2.reshape(1, d),
                     w_ff1.astype(BF16), w_ff2.astype(BF16))
```

```python
import functools
import math

import numpy as np
import jax
import jax.numpy as jnp
from jax import lax
from jax.experimental import pallas as pl
from jax.experimental.pallas import tpu as pltpu

F32 = jnp.float32
BF16 = jnp.bfloat16

GRID_W = 64
DA_HEADS = 8
DA_DH = 64
DA_DV = 2 * DA_DH
GLA_HEADS = 4
GLA_DK = 128
GLA_DV = 256
GLA_GATE_RANK = 16
GLA_GATE_NORM = 16.0
ROPE_THETA = 10000.0
ROPE_AXIS_DIM = DA_DH // 2
ROPE_HALF = ROPE_AXIS_DIM // 2
N_MOD = 6
EPS = 1e-6
NEG_BIG = -1e30

LANES = 128
SUBLANES = 8
LOG2_E = math.log2(math.e)
VMEM_LIMIT = 56 * 1024 * 1024

ROW_TILES = (512, 256, 128)
GLA_CHUNK = 128
GLA_BASES = (128, 16)
GLA_BASE_LIMIT = 100.0
GLA_HEAD_BLOCK = 4
DA_KEY_BLOCKS = (1024, 512, 256, 128)
DA_QUERY_BLOCKS = (1024, 512, 256, 128)
DA_BOUND_LIMIT = 50.0
DA_BOUND_SLACK = 1.01
LOW_PAD = LANES


def _nt(a, b):
    return lax.dot_general(a, b, (((1,), (1,)), ((), ())), preferred_element_type=F32)


def _tn(a, b):
    return lax.dot_general(a, b, (((0,), (0,)), ((), ())), preferred_element_type=F32)


def _mm(a, b):
    return jnp.dot(a, b, preferred_element_type=F32)


def _rms(x, gain):
    return x * lax.rsqrt(jnp.mean(x * x, axis=-1, keepdims=True) + EPS) * gain


def _params(*sem):
    return pltpu.CompilerParams(dimension_semantics=sem, vmem_limit_bytes=VMEM_LIMIT)


def _resident(shape):
    nd = len(shape)
    return pl.BlockSpec(shape, lambda *_: (0,) * nd, pipeline_mode=pl.Buffered(1))


def _largest_divisor(n, candidates):
    for cand in candidates:
        if n % cand == 0:
            return cand
    raise ValueError(f"no tile in {candidates} divides {n}")


def _mod_kernel(c_ref, w_ref, b_ref, o_ref):
    c = c_ref[...]
    s = (c * jax.nn.sigmoid(c)).astype(BF16)
    o_ref[...] = _mm(s, w_ref[...].astype(BF16)) + b_ref[...]


def _mod_call(cc, w_mod, b_mod):
    rows, d = cc.shape
    n = w_mod.shape[1]
    bn = _largest_divisor(n, (1024, 512, 256, 128))
    return pl.pallas_call(
        _mod_kernel,
        out_shape=jax.ShapeDtypeStruct((rows, n), F32),
        grid=(n // bn,),
        in_specs=[pl.BlockSpec((rows, d), lambda j: (0, 0)),
                  pl.BlockSpec((d, bn), lambda j: (0, j)),
                  pl.BlockSpec((1, bn), lambda j: (0, j))],
        out_specs=pl.BlockSpec((rows, bn), lambda j: (0, j)),
        compiler_params=_params("arbitrary"),
        name="mod",
    )(cc, w_mod, b_mod.reshape(1, n))


_OFF_Q = 0
_OFF_K = _OFF_Q + DA_HEADS * 2 * DA_DH
_OFF_V = _OFF_K + DA_HEADS * 2 * DA_DH
_OFF_GQ = _OFF_V + DA_HEADS * DA_DV
_OFF_GK = _OFF_GQ + GLA_HEADS * GLA_DK
_OFF_GV = _OFF_GK + GLA_HEADS * GLA_DK
_OFF_LOW = _OFF_GV + GLA_HEADS * GLA_DV
_TAIL_GOUT = 0
_TAIL_MERGE = _TAIL_GOUT + GLA_HEADS * GLA_DV
_N_GATE = 2 * GLA_HEADS * GLA_DK


def _proj_kernel(x_ref, sh_ref, sc_ref, gain_ref, w_ref, wlow_ref, wtail_ref, wup_ref, bup_ref,
                 cos_ref, sa_ref, sb_ref,
                 q_ref, k_ref, v_ref, gq_ref, gk_ref, gv_ref, gate_ref, gout_ref, mrg_ref):
    d = x_ref.shape[-1]
    h = (_rms(x_ref[...], gain_ref[...]) * (1.0 + sc_ref[...]) + sh_ref[...]).astype(BF16)
    cos, sa, sb = cos_ref[...], sa_ref[...], sb_ref[...]

    def roped(off, scale):
        acc = _mm(h, w_ref[:, off:off + d])
        for s in range(d // LANES):
            a = acc[:, s * LANES:(s + 1) * LANES]
            r = (a * cos + pltpu.roll(a, ROPE_HALF, 1) * sa
                 + pltpu.roll(a, LANES - ROPE_HALF, 1) * sb)
            yield s, r * scale

    low = _mm(h, wlow_ref[...]).astype(BF16)
    z = _mm(low, wup_ref[...]) + bup_ref[...]
    log_sig = jnp.minimum(z, 0.0) - jnp.log1p(jnp.exp(-jnp.abs(z)))
    gate_ref[...] = log_sig / GLA_GATE_NORM
    mg = _mm(h, wtail_ref[:, _TAIL_MERGE:_TAIL_MERGE + 2 * d])
    mrg_ref[...] = jax.nn.sigmoid(mg).astype(mrg_ref.dtype)
    go = _mm(h, wtail_ref[:, _TAIL_GOUT:_TAIL_MERGE])
    gout_ref[...] = (go * jax.nn.sigmoid(go)).astype(gout_ref.dtype)
    for hd, slab in roped(_OFF_Q, DA_DH ** -0.5 * LOG2_E):
        q_ref[hd] = slab.T.astype(q_ref.dtype)
    for hd, slab in roped(_OFF_K, 1.0):
        k_ref[hd] = slab.astype(k_ref.dtype)
    v = _mm(h, w_ref[:, _OFF_V:_OFF_GQ])
    for hd in range(DA_HEADS):
        v_ref[hd] = v[:, hd * DA_DV:(hd + 1) * DA_DV].T.astype(v_ref.dtype)
    gq_ref[...] = (_mm(h, w_ref[:, _OFF_GQ:_OFF_GK]) * (GLA_DK ** -0.5)).astype(gq_ref.dtype)
    gk_ref[...] = _mm(h, w_ref[:, _OFF_GK:_OFF_GV]).astype(gk_ref.dtype)
    gv_ref[...] = _mm(h, w_ref[:, _OFF_GV:_OFF_LOW]).astype(gv_ref.dtype)


def _proj_call(x, shift, scale, gain, w_pieces, w_up, b_up, cos, sa, sb):
    b, t, d = x.shape
    tm = _largest_divisor(t, ROW_TILES)
    tk = _da_key_block(t)
    per_block = tk // tm
    row = lambda width: pl.BlockSpec((None, tm, width), lambda bi, i: (bi, i, 0))
    vec = pl.BlockSpec((None, 1, d), lambda bi, i: (bi, 0, 0))
    tab = pl.BlockSpec((tm, LANES), lambda bi, i: (i, 0))
    widths = (GLA_HEADS * GLA_DK, GLA_HEADS * GLA_DK, GLA_HEADS * GLA_DV,
              _N_GATE, GLA_HEADS * GLA_DV, 2 * d)
    dtypes = (BF16, BF16, BF16, F32, BF16, BF16)
    dq = 2 * DA_DH
    head_shapes = [jax.ShapeDtypeStruct((b, DA_HEADS, dq, t), BF16),
                   jax.ShapeDtypeStruct((b, DA_HEADS, t, dq), BF16),
                   jax.ShapeDtypeStruct((b, DA_HEADS, t // tk, DA_DV, tk), BF16)]
    head_specs = [pl.BlockSpec((None, DA_HEADS, dq, tm), lambda bi, i: (bi, 0, 0, i)),
                  pl.BlockSpec((None, DA_HEADS, tm, dq), lambda bi, i: (bi, 0, i, 0)),
                  pl.BlockSpec((None, DA_HEADS, None, DA_DV, tm),
                               lambda bi, i: (bi, 0, i // per_block, 0, i % per_block))]
    return pl.pallas_call(
        _proj_kernel,
        out_shape=head_shapes + [jax.ShapeDtypeStruct((b, t, w), dt) for w, dt in zip(widths, dtypes)],
        grid=(b, t // tm),
        in_specs=[row(d), vec, vec, _resident((1, d))] + [_resident(w.shape) for w in w_pieces]
        + [_resident(w_up.shape), _resident(b_up.shape), tab, tab, tab],
        out_specs=head_specs + [row(w) for w in widths],
        compiler_params=_params("parallel", "parallel"),
        name="proj",
    )(x, shift, scale, gain, *w_pieces, w_up, b_up, cos, sa, sb)


def _rope_tables(t):
    pos = jnp.arange(t)
    inv = ROPE_THETA ** (-jnp.arange(0, ROPE_AXIS_DIM, 2, dtype=F32) / ROPE_AXIS_DIM)
    ang_r = (pos // GRID_W).astype(F32)[:, None] * inv
    ang_c = (pos % GRID_W).astype(F32)[:, None] * inv
    zero = jnp.zeros_like(ang_r)
    cos = jnp.concatenate([jnp.cos(ang_r)] * 2 + [jnp.cos(ang_c)] * 2, axis=-1)
    sa = jnp.concatenate([zero, jnp.sin(ang_r), zero, jnp.sin(ang_c)], axis=-1)
    sb = jnp.concatenate([-jnp.sin(ang_r), zero, -jnp.sin(ang_c), zero], axis=-1)
    rep = LANES // DA_DH
    return tuple(jnp.tile(a, (1, rep)) for a in (cos, sa, sb))


def _identity_tables(t):
    return jnp.ones((t, LANES), F32), jnp.zeros((t, LANES), F32), jnp.zeros((t, LANES), F32)


def _level_ref(b, s, row, reverse):
    c = b.shape[0]
    if 2 * s >= SUBLANES:
        pieces = []
        for blk in range(c // (2 * s)):
            r = blk * 2 * s + s - 1 + (1 if reverse else 0)
            pieces.append(jnp.broadcast_to(b[r:r + 1, :], (2 * s, b.shape[1])))
        return jnp.concatenate(pieces, axis=0)
    pos = row % (2 * s)
    mid = s if reverse else s - 1
    ref = b
    for p in range(2 * s):
        if p != mid:
            ref = jnp.where(pos == p, pltpu.roll(b, (p - mid) % c, 0), ref)
    return ref


def _split3(x):
    hi = x.astype(BF16)
    r = x - hi.astype(F32)
    mid = r.astype(BF16)
    lo = (r - mid.astype(F32)).astype(BF16)
    return hi, mid, lo


def _chunk_cumsum(g, reverse):
    c = g.shape[0]
    ri = lax.broadcasted_iota(jnp.int32, (c, c), 0)
    ci = lax.broadcasted_iota(jnp.int32, (c, c), 1)
    tri = jnp.where((ci >= ri) if reverse else (ci <= ri), 1.0, 0.0).astype(BF16)
    hi, mid, lo = _split3(g)
    return _mm(tri, lo) + _mm(tri, mid) + _mm(tri, hi)


def _gla_tables(c):
    levels = int(math.log2(c))
    row = np.arange(c)
    side = np.zeros((levels, 2, 2, c, GLA_DK), np.float32)
    same = np.zeros((levels, c, c), np.float32)
    for lvl in range(levels):
        s = 1 << lvl
        upper = (row % (2 * s)) >= s
        for rev in range(2):
            q_side = ~upper if rev else upper
            side[lvl, rev, 0] = np.where(q_side, 0.0, NEG_BIG)[:, None]
            side[lvl, rev, 1] = np.where(q_side, NEG_BIG, 0.0)[:, None]
        same[lvl] = (row[:, None] // (2 * s)) == (row[None, :] // (2 * s))
    causal = np.zeros((len(GLA_BASES), 2, c, c), np.float32)
    for i, base in enumerate(GLA_BASES):
        blk = row // min(base, c)
        inside = blk[:, None] == blk[None, :]
        causal[i, 0] = inside & (row[None, :] <= row[:, None])
        causal[i, 1] = inside & (row[None, :] >= row[:, None])
    return jnp.asarray(side), jnp.asarray(same), jnp.asarray(causal)


def _base_offset(b, base, reverse):
    c, n = b.shape
    pieces = []
    for i in range(c // base):
        r = (i + 1) * base if reverse else i * base - 1
        edge = b[r:r + 1, :] if 0 <= r < c else jnp.zeros((1, n), F32)
        pieces.append(jnp.broadcast_to(edge, (base, n)))
    return b - jnp.concatenate(pieces, axis=0)


def _gla_direction(q, k, v, b, st_ref, side_ref, same_ref, causal_ref, reverse, tier, base_offset):
    c, dk = b.shape
    row = lax.broadcasted_iota(jnp.int32, (c, dk), 0)
    total = b[0:1, :] if reverse else b[c - 1:c, :]
    qf = q.astype(F32)
    kf = k.astype(F32)
    rev = 1 if reverse else 0

    st = st_ref[...]
    out = _nt((qf * jnp.exp2(b)).astype(BF16), st.astype(BF16))

    if tier is None:
        ri = lax.broadcasted_iota(jnp.int32, (c, c), 0)
        ci = lax.broadcasted_iota(jnp.int32, (c, c), 1)
        a = jnp.where(ri == ci, _nt(q, k), 0.0)
        s = 1
    else:
        qd = (qf * jnp.exp2(base_offset)).astype(BF16)
        kd = (kf * jnp.exp2(-base_offset)).astype(BF16)
        a = _nt(qd, kd) * causal_ref[tier, rev]
        s = min(GLA_BASES[tier], c)
    lvl = int(math.log2(s))
    while s < c:
        t = b - _level_ref(b, s, row, reverse)
        ql = (qf * jnp.exp2(t + side_ref[lvl, rev, 0])).astype(BF16)
        kl = (kf * jnp.exp2(side_ref[lvl, rev, 1] - t)).astype(BF16)
        al = _nt(ql, kl)
        a = a + (al if 2 * s == c else al * same_ref[lvl])
        s *= 2
        lvl += 1
    out = out + _mm(a.astype(BF16), v)

    kd = (kf * jnp.exp2(total - b)).astype(BF16)
    st_ref[...] = st * jnp.exp2(total) + _tn(v, kd)
    return out


def _gla_kernel(qf_ref, kf_ref, vf_ref, gf_ref, qb_ref, kb_ref, vb_ref, gb_ref, sf0_ref, sb0_ref,
                side_ref, same_ref, causal_ref, of_ref, ob_ref, sf1_ref, sb1_ref, sf_ref, sb_ref):
    i = pl.program_id(2)
    heads = sf_ref.shape[0]

    @pl.when(i == 0)
    def _():
        sf_ref[...] = sf0_ref[...]
        sb_ref[...] = sb0_ref[...]

    gf = gf_ref[...] * LOG2_E
    gb = gb_ref[...] * LOG2_E
    c = gf.shape[0]
    steepest = jnp.maximum(jnp.max(jnp.abs(gf)), jnp.max(jnp.abs(gb)))
    bf = _chunk_cumsum(gf, False)
    bb = _chunk_cumsum(gb, True)

    def run(tier):
        base = None if tier is None else min(GLA_BASES[tier], c)
        df = None if tier is None else _base_offset(bf, base, False)
        db = None if tier is None else _base_offset(bb, base, True)
        for h in range(heads):
            ks = slice(h * GLA_DK, (h + 1) * GLA_DK)
            vs = slice(h * GLA_DV, (h + 1) * GLA_DV)
            of_ref[:, vs] = _gla_direction(
                qf_ref[:, ks], kf_ref[:, ks], vf_ref[:, vs], bf[:, ks], sf_ref.at[h], side_ref, same_ref,
                causal_ref, False, tier, None if tier is None else df[:, ks]).astype(of_ref.dtype)
            ob_ref[:, vs] = _gla_direction(
                qb_ref[:, ks], kb_ref[:, ks], vb_ref[:, vs], bb[:, ks], sb_ref.at[h], side_ref, same_ref,
                causal_ref, True, tier, None if tier is None else db[:, ks]).astype(ob_ref.dtype)

    taken = None
    for tier, base in enumerate(GLA_BASES):
        fits = min(base, c) * steepest <= GLA_BASE_LIMIT
        pl.when(fits if taken is None else jnp.logical_and(fits, jnp.logical_not(taken)))(
            functools.partial(run, tier))
        taken = fits if taken is None else jnp.logical_or(taken, fits)
    pl.when(jnp.logical_not(taken))(functools.partial(run, None))

    @pl.when(i == pl.num_programs(2) - 1)
    def _():
        sf1_ref[...] = sf_ref[...]
        sb1_ref[...] = sb_ref[...]


def _gla_call(gq, gk, gv, gate, sf0, sb0):
    b, t, _ = gq.shape
    c = _largest_divisor(t, (GLA_CHUNK,))
    n = t // c
    hb = GLA_HEAD_BLOCK
    nhb = GLA_HEADS // hb
    fwd = lambda w, off=0: pl.BlockSpec((None, c, hb * w), lambda bi, h, i: (bi, i, h + off))
    bwd = lambda w, off=0: pl.BlockSpec((None, c, hb * w), lambda bi, h, i: (bi, n - 1 - i, h + off))
    state = pl.BlockSpec((None, hb, GLA_DV, GLA_DK), lambda bi, h, i: (bi, h, 0, 0))
    o_shape = jax.ShapeDtypeStruct((b, t, GLA_HEADS * GLA_DV), BF16)
    s_shape = jax.ShapeDtypeStruct((b, GLA_HEADS, GLA_DV, GLA_DK), F32)
    side, same, causal = _gla_tables(c)
    return pl.pallas_call(
        _gla_kernel,
        out_shape=[o_shape, o_shape, s_shape, s_shape],
        grid=(b, nhb, n),
        in_specs=[fwd(GLA_DK), fwd(GLA_DK), fwd(GLA_DV), fwd(GLA_DK),
                  bwd(GLA_DK), bwd(GLA_DK), bwd(GLA_DV), bwd(GLA_DK, nhb),
                  state, state, _resident(side.shape), _resident(same.shape), _resident(causal.shape)],
        out_specs=[fwd(GLA_DV), bwd(GLA_DV), state, state],
        scratch_shapes=[pltpu.VMEM((hb, GLA_DV, GLA_DK), F32), pltpu.VMEM((hb, GLA_DV, GLA_DK), F32)],
        compiler_params=_params("parallel", "parallel", "arbitrary"),
        name="gla",
    )(gq, gk, gv, gate, gq, gk, gv, gate, sf0, sb0, side, same, causal)


def _da_kernel(lq1_ref, lk1_ref, lq2_ref, lk2_ref, gain_ref, qt_ref, k_ref, vt_ref, ck_ref, cvt_ref,
               y_ref, kmax_ref, *, lam_init):
    tq = qt_ref.shape[-1]
    nk, tk, dq = k_ref.shape
    nck = ck_ref.shape[0]
    qt = qt_ref[...]
    sub = lax.broadcasted_iota(jnp.int32, qt.shape, 0)
    zero = jnp.zeros_like(qt)
    qm = jnp.concatenate([jnp.where(sub < DA_DH, qt, zero), jnp.where(sub < DA_DH, zero, qt)], axis=1)

    @pl.when(pl.program_id(2) == 0)
    def _():
        di = lax.broadcasted_iota(jnp.int32, (dq, dq), 0)
        ci = lax.broadcasted_iota(jnp.int32, (dq, dq), 1)
        same_comp = jnp.where((di < DA_DH) == (ci < DA_DH), 1.0, 0.0).astype(BF16)

        def block_max(kb, mx):
            nsq = _mm(kb * kb, same_comp)
            return jnp.maximum(mx, jnp.max(nsq.reshape(-1, SUBLANES, dq), axis=0))

        mx = jnp.zeros((SUBLANES, dq), F32)
        for kb in [k_ref[j] for j in range(nk)] + [ck_ref[j] for j in range(nck)]:
            mx = block_max(kb, mx)
        kmax_ref[...] = jnp.broadcast_to(jnp.max(mx, axis=0, keepdims=True), kmax_ref.shape)

    qf = qt.astype(F32)
    qsq = qf * qf
    kmsq = kmax_ref[...]
    bsq = jnp.concatenate([jnp.sum(qsq[:DA_DH], axis=0, keepdims=True) * kmsq[0:1, 0:1],
                           jnp.sum(qsq[DA_DH:], axis=0, keepdims=True) * kmsq[0:1, DA_DH:DA_DH + 1]],
                          axis=1)
    bound = jnp.sqrt(bsq) * DA_BOUND_SLACK
    in_range = jnp.max(bound) <= DA_BOUND_LIMIT

    def probs(kb, shift):
        s = _mm(kb, qm)
        p = jnp.exp2(s.reshape(-1, SUBLANES, s.shape[1]) - shift[None])
        return p.reshape(s.shape)

    def finish(acc, l):
        lam = (jnp.exp(jnp.sum(lq1_ref[...] * lk1_ref[...], keepdims=True))
               - jnp.exp(jnp.sum(lq2_ref[...] * lk2_ref[...], keepdims=True)) + lam_init)
        o = acc / l
        ot = o[:, :tq] - lam * o[:, tq:]
        ms = jnp.mean(ot * ot, axis=0, keepdims=True)
        yt = ot * lax.rsqrt(ms + EPS) * (gain_ref[...] * (1.0 - lam_init))
        y_ref[...] = yt.T.astype(y_ref.dtype)

    @pl.when(in_range)
    def _():
        shift = jnp.broadcast_to(bound, (SUBLANES, 2 * tq))
        acc = jnp.zeros((DA_DV, 2 * tq), F32)
        l = jnp.zeros((1, 2 * tq), F32)
        blocks = [(k_ref[j], vt_ref[j]) for j in range(nk)] + [(ck_ref[j], cvt_ref[j]) for j in range(nck)]
        for kb, vb in blocks:
            p = probs(kb, shift)
            l = l + jnp.sum(p, axis=0, keepdims=True)
            acc = acc + _mm(vb, p.astype(BF16))
        finish(acc, l)

    @pl.when(jnp.logical_not(in_range))
    def _():
        def block(kb, vb, carry):
            m_prev, l, acc = carry
            s = _mm(kb, qm)
            m_new = jnp.maximum(m_prev, jnp.max(s, axis=0, keepdims=True))
            alpha = jnp.exp2(m_prev - m_new)
            p = jnp.exp2(s - m_new)
            return (m_new, alpha * l + jnp.sum(p, axis=0, keepdims=True),
                    alpha * acc + _mm(vb, p.astype(BF16)))

        carry = (jnp.full((1, 2 * tq), NEG_BIG, F32), jnp.zeros((1, 2 * tq), F32),
                 jnp.zeros((DA_DV, 2 * tq), F32))
        carry = lax.fori_loop(0, nk, lambda j, c: block(k_ref[j], vt_ref[j], c), carry)
        for j in range(nck):
            carry = block(ck_ref[j], cvt_ref[j], carry)
        finish(carry[2], carry[1])


def _da_key_block(t):
    return _largest_divisor(t, DA_KEY_BLOCKS)


def _da_call(lams, gain, qt, k5, vt5, ck5, cvt5, lam_init):
    b, h, dq, t = qt.shape
    tq = _largest_divisor(t, DA_QUERY_BLOCKS)
    small = pl.BlockSpec((1, DA_DH), lambda bi, hi, i: (0, 0))
    per_head = lambda a: pl.BlockSpec((None, None) + a.shape[2:], lambda bi, hi, i: (bi, hi, 0, 0, 0))
    return pl.pallas_call(
        functools.partial(_da_kernel, lam_init=lam_init),
        out_shape=jax.ShapeDtypeStruct((b, t, h * DA_DV), BF16),
        grid=(b, h, t // tq),
        in_specs=[small, small, small, small,
                  pl.BlockSpec((DA_DV, 1), lambda bi, hi, i: (0, 0)),
                  pl.BlockSpec((None, None, dq, tq), lambda bi, hi, i: (bi, hi, 0, i)),
                  per_head(k5), per_head(vt5), per_head(ck5), per_head(cvt5)],
        out_specs=pl.BlockSpec((None, tq, DA_DV), lambda bi, hi, i: (bi, i, hi)),
        scratch_shapes=[pltpu.VMEM((SUBLANES, dq), F32)],
        compiler_params=_params("parallel", "parallel", "arbitrary"),
        name="da",
    )(*lams, gain, qt, k5, vt5, ck5, cvt5)


def _merge_kernel(x_ref, gt_ref, yda_ref, of_ref, ob_ref, gout_ref, mrg_ref, ggain_ref, pgain_ref,
                  wda_ref, wgla_ref, wout_ref, o_ref):
    d = x_ref.shape[-1]
    o_gla = of_ref[...].astype(F32) + ob_ref[...].astype(F32)
    ggain = ggain_ref[...]
    heads = [_rms(o_gla[:, h * GLA_DV:(h + 1) * GLA_DV], ggain) for h in range(GLA_HEADS)]
    y_gla = (jnp.concatenate(heads, axis=-1) * gout_ref[...].astype(F32)).astype(BF16)
    gates = mrg_ref[...].astype(F32)
    y = (gates[:, :d] * _mm(yda_ref[...], wda_ref[...])
         + gates[:, d:] * _mm(y_gla, wgla_ref[...]))
    z = _mm(y.astype(BF16), wout_ref[...])
    o_ref[...] = x_ref[...] + gt_ref[...] * _rms(z, pgain_ref[...])


def _merge_call(x, gt1, y_da, o_f, o_b, gout, mrg, ggain, pgain, w_da, w_gla, w_out):
    b, t, d = x.shape
    tm = _largest_divisor(t, ROW_TILES)
    row = lambda width: pl.BlockSpec((None, tm, width), lambda bi, i: (bi, i, 0))
    vec = pl.BlockSpec((None, 1, d), lambda bi, i: (bi, 0, 0))
    return pl.pallas_call(
        _merge_kernel,
        out_shape=jax.ShapeDtypeStruct((b, t, d), F32),
        grid=(b, t // tm),
        in_specs=[row(d), vec, row(d), row(d), row(d), row(d), row(2 * d),
                  _resident(ggain.shape), _resident(pgain.shape),
                  _resident(w_da.shape), _resident(w_gla.shape), _resident(w_out.shape)],
        out_specs=row(d),
        compiler_params=_params("parallel", "parallel"),
        name="merge",
    )(x, gt1, y_da, o_f, o_b, gout, mrg, ggain, pgain, w_da, w_gla, w_out)


def _mlp_kernel(x_ref, sh_ref, sc_ref, gt_ref, pre_ref, post_ref, w1_ref, w2_ref, o_ref):
    x = x_ref[...]
    h = (_rms(x, pre_ref[...]) * (1.0 + sc_ref[...]) + sh_ref[...]).astype(BF16)
    u = jnp.maximum(_mm(h, w1_ref[...]), 0.0)
    z = _mm((u * u).astype(BF16), w2_ref[...])
    o_ref[...] = x + gt_ref[...] * _rms(z, post_ref[...])


def _mlp_call(x, sh2, sc2, gt2, pre, post, w1, w2):
    b, t, d = x.shape
    tm = _largest_divisor(t, ROW_TILES)
    row = pl.BlockSpec((None, tm, d), lambda bi, i: (bi, i, 0))
    vec = pl.BlockSpec((None, 1, d), lambda bi, i: (bi, 0, 0))
    return pl.pallas_call(
        _mlp_kernel,
        out_shape=jax.ShapeDtypeStruct((b, t, d), F32),
        grid=(b, t // tm),
        in_specs=[row, vec, vec, vec, _resident(pre.shape), _resident(post.shape),
                  _resident(w1.shape), _resident(w2.shape)],
        out_specs=row,
        compiler_params=_params("parallel", "parallel"),
        name="mlp",
    )(x, sh2, sc2, gt2, pre, post, w1, w2)


def _split_w_in(w):
    low0 = _OFF_LOW
    low1 = low0 + 2 * GLA_GATE_RANK
    low = jnp.pad(w[:, low0:low1], ((0, 0), (0, LOW_PAD - 2 * GLA_GATE_RANK)))
    return w[:, :low0].astype(BF16), low.astype(BF16), w[:, low1:].astype(BF16)


def _pack_gate_up(w_gate_up, b_gate_up):
    hk = GLA_HEADS * GLA_DK
    w = jnp.zeros((LOW_PAD, 2 * hk), F32)
    for z in range(2):
        w = w.at[z * GLA_GATE_RANK:(z + 1) * GLA_GATE_RANK, z * hk:(z + 1) * hk].set(w_gate_up[z])
    return w.astype(BF16), b_gate_up.reshape(1, 2 * hk).astype(F32)


def kernel(x, c, ctx, c_ctx, w_mod, b_mod, pre_norm1, w_in, w_gate_up, b_gate_up, lambda_q1, lambda_k1,
           lambda_q2, lambda_k2, da_head_norm, gla_head_norm, w_branch_da, w_branch_gla, w_out, post_norm1,
           pre_norm2, w_ff1, w_ff2, post_norm2):
    b, t, d = x.shape
    n_ctx = ctx.shape[1]
    assert w_in.shape[0] == 1, "single-layer block only"
    assert d == DA_HEADS * DA_DV == GLA_HEADS * GLA_DV and t % GRID_W == 0
    layer = 0
    lam_init = 0.8 - 0.6 * math.exp(-0.3 * layer)
    (w_mod, b_mod, pre_norm1, w_in, w_gate_up, b_gate_up, lambda_q1, lambda_k1, lambda_q2, lambda_k2,
     da_head_norm, gla_head_norm, w_branch_da, w_branch_gla, w_out, post_norm1, pre_norm2, w_ff1, w_ff2,
     post_norm2) = (a.reshape(a.shape[1:]) for a in (
         w_mod, b_mod, pre_norm1, w_in, w_gate_up, b_gate_up, lambda_q1, lambda_k1, lambda_q2, lambda_k2,
         da_head_norm, gla_head_norm, w_branch_da, w_branch_gla, w_out, post_norm1, pre_norm2, w_ff1,
         w_ff2, post_norm2))

    rows = -(-(b + 1) // SUBLANES) * SUBLANES
    cc = jnp.zeros((rows, d), F32).at[:b].set(c).at[b].set(c_ctx)
    mod = _mod_call(cc, w_mod, b_mod)
    lat = [mod[:b, i * d:(i + 1) * d].reshape(b, 1, d) for i in range(N_MOD)]
    cmod = [jnp.broadcast_to(mod[b, i * d:(i + 1) * d].reshape(1, 1, d), (b, 1, d)) for i in range(2)]
    sh1, sc1, gt1, sh2, sc2, gt2 = lat

    w_pieces = _split_w_in(w_in)
    w_up, b_up = _pack_gate_up(w_gate_up, b_gate_up)
    gain1 = pre_norm1.reshape(1, d)

    qt, k, vt5, gq, gk, gv, gate, gout, mrg = _proj_call(
        x, sh1, sc1, gain1, w_pieces, w_up, b_up, *_rope_tables(t))
    _, ck, cvt5, _, cgk, cgv, cgate, _, _ = _proj_call(
        ctx, cmod[0], cmod[1], gain1, w_pieces, w_up, b_up, *_identity_tables(n_ctx))

    zero_state = jnp.zeros((b, GLA_HEADS, GLA_DV, GLA_DK), F32)
    _, _, sf0, sb0 = _gla_call(jnp.zeros_like(cgk), cgk, cgv, cgate, zero_state, zero_state)
    o_f, o_b, _, _ = _gla_call(gq, gk, gv, gate, sf0, sb0)

    k5 = k.reshape(b, DA_HEADS, vt5.shape[2], vt5.shape[4], 2 * DA_DH)
    ck5 = ck.reshape(b, DA_HEADS, cvt5.shape[2], cvt5.shape[4], 2 * DA_DH)
    lams = [a.reshape(1, DA_DH) for a in (lambda_q1, lambda_k1, lambda_q2, lambda_k2)]
    y_da = _da_call(lams, da_head_norm.reshape(DA_DV, 1), qt, k5, vt5, ck5, cvt5, lam_init)

    x1 = _merge_call(x, gt1, y_da, o_f, o_b, gout, mrg,
                     gla_head_norm.reshape(1, GLA_DV), post_norm1.reshape(1, d),
                     w_branch_da.astype(BF16), w_branch_gla.astype(BF16),
                     w_out.astype(BF16))
    return _mlp_call(x1, sh2, sc2, gt2, pre_norm2.reshape(1, d), post_norm2.reshape(1, d),
                     w_ff1.astype(BF16), w_ff2.astype(BF16))
```

```python
import functools
import math

import numpy as np
import jax
import jax.numpy as jnp
from jax import lax
from jax.experimental import pallas as pl
from jax.experimental.pallas import tpu as pltpu

F32 = jnp.float32
BF16 = jnp.bfloat16

GRID_W = 64
DA_HEADS = 8
DA_DH = 64
DA_DV = 2 * DA_DH
GLA_HEADS = 4
GLA_DK = 128
GLA_DV = 256
GLA_GATE_RANK = 16
GLA_GATE_NORM = 16.0
ROPE_THETA = 10000.0
ROPE_AXIS_DIM = DA_DH // 2
ROPE_HALF = ROPE_AXIS_DIM // 2
N_MOD = 6
EPS = 1e-6
NEG_BIG = -1e30

LANES = 128
SUBLANES = 8
LOG2_E = math.log2(math.e)
VMEM_LIMIT = 56 * 1024 * 1024

ROW_TILES = (512, 256, 128)
GLA_CHUNK = 128
GLA_STEP_CHUNKS = 2
GLA_BASES = (128, 16)
GLA_BASE_LIMIT = 100.0
GLA_HEAD_BLOCK = 4
DA_KEY_BLOCKS = (1024, 512, 256, 128)
DA_QUERY_BLOCKS = (1024, 512, 256, 128)
DA_BOUND_LIMIT = 50.0
DA_BOUND_SLACK = 1.01
LOW_PAD = LANES


def _nt(a, b):
    return lax.dot_general(a, b, (((1,), (1,)), ((), ())), preferred_element_type=F32)


def _tn(a, b):
    return lax.dot_general(a, b, (((0,), (0,)), ((), ())), preferred_element_type=F32)


def _mm(a, b):
    return jnp.dot(a, b, preferred_element_type=F32)


def _rms(x, gain):
    return x * lax.rsqrt(jnp.mean(x * x, axis=-1, keepdims=True) + EPS) * gain


def _params(*sem):
    return pltpu.CompilerParams(dimension_semantics=sem, vmem_limit_bytes=VMEM_LIMIT)


def _resident(shape):
    nd = len(shape)
    return pl.BlockSpec(shape, lambda *_: (0,) * nd, pipeline_mode=pl.Buffered(1))


def _largest_divisor(n, candidates):
    for cand in candidates:
        if n % cand == 0:
            return cand
    raise ValueError(f"no tile in {candidates} divides {n}")


def _mod_kernel(c_ref, w_ref, b_ref, o_ref):
    c = c_ref[...]
    s = (c * jax.nn.sigmoid(c)).astype(BF16)
    o_ref[...] = _mm(s, w_ref[...].astype(BF16)) + b_ref[...]


def _mod_call(cc, w_mod, b_mod):
    rows, d = cc.shape
    n = w_mod.shape[1]
    bn = _largest_divisor(n, (1024, 512, 256, 128))
    return pl.pallas_call(
        _mod_kernel,
        out_shape=jax.ShapeDtypeStruct((rows, n), F32),
        grid=(n // bn,),
        in_specs=[pl.BlockSpec((rows, d), lambda j: (0, 0)),
                  pl.BlockSpec((d, bn), lambda j: (0, j)),
                  pl.BlockSpec((1, bn), lambda j: (0, j))],
        out_specs=pl.BlockSpec((rows, bn), lambda j: (0, j)),
        compiler_params=_params("arbitrary"),
        name="mod",
    )(cc, w_mod, b_mod.reshape(1, n))


_OFF_Q = 0
_OFF_K = _OFF_Q + DA_HEADS * 2 * DA_DH
_OFF_V = _OFF_K + DA_HEADS * 2 * DA_DH
_OFF_GQ = _OFF_V + DA_HEADS * DA_DV
_OFF_GK = _OFF_GQ + GLA_HEADS * GLA_DK
_OFF_GV = _OFF_GK + GLA_HEADS * GLA_DK
_OFF_LOW = _OFF_GV + GLA_HEADS * GLA_DV
_TAIL_GOUT = 0
_TAIL_MERGE = _TAIL_GOUT + GLA_HEADS * GLA_DV
_N_GATE = 2 * GLA_HEADS * GLA_DK


def _proj_kernel(x_ref, sh_ref, sc_ref, gain_ref, w_ref, wlow_ref, wtail_ref, wup_ref, bup_ref,
                 cos_ref, sa_ref, sb_ref,
                 q_ref, k_ref, v_ref, gq_ref, gk_ref, gv_ref, gate_ref, gout_ref, mrg_ref):
    d = x_ref.shape[-1]
    h = (_rms(x_ref[...], gain_ref[...]) * (1.0 + sc_ref[...]) + sh_ref[...]).astype(BF16)
    cos, sa, sb = cos_ref[...], sa_ref[...], sb_ref[...]

    def roped(off, scale):
        acc = _mm(h, w_ref[:, off:off + d])
        for s in range(d // LANES):
            a = acc[:, s * LANES:(s + 1) * LANES]
            r = (a * cos + pltpu.roll(a, ROPE_HALF, 1) * sa
                 + pltpu.roll(a, LANES - ROPE_HALF, 1) * sb)
            yield s, r * scale

    low = _mm(h, wlow_ref[...]).astype(BF16)
    z = _mm(low, wup_ref[...]) + bup_ref[...]
    log_sig = jnp.minimum(z, 0.0) - jnp.log1p(jnp.exp(-jnp.abs(z)))
    gate_ref[...] = log_sig / GLA_GATE_NORM
    mg = _mm(h, wtail_ref[:, _TAIL_MERGE:_TAIL_MERGE + 2 * d])
    mrg_ref[...] = jax.nn.sigmoid(mg).astype(mrg_ref.dtype)
    go = _mm(h, wtail_ref[:, _TAIL_GOUT:_TAIL_MERGE])
    gout_ref[...] = (go * jax.nn.sigmoid(go)).astype(gout_ref.dtype)
    for hd, slab in roped(_OFF_Q, DA_DH ** -0.5 * LOG2_E):
        q_ref[hd] = slab.T.astype(q_ref.dtype)
    for hd, slab in roped(_OFF_K, 1.0):
        k_ref[hd] = slab.astype(k_ref.dtype)
    v = _mm(h, w_ref[:, _OFF_V:_OFF_GQ])
    for hd in range(DA_HEADS):
        v_ref[hd] = v[:, hd * DA_DV:(hd + 1) * DA_DV].T.astype(v_ref.dtype)
    gq_ref[...] = (_mm(h, w_ref[:, _OFF_GQ:_OFF_GK]) * (GLA_DK ** -0.5)).astype(gq_ref.dtype)
    gk_ref[...] = _mm(h, w_ref[:, _OFF_GK:_OFF_GV]).astype(gk_ref.dtype)
    gv_ref[...] = _mm(h, w_ref[:, _OFF_GV:_OFF_LOW]).astype(gv_ref.dtype)


def _proj_call(x, shift, scale, gain, w_pieces, w_up, b_up, cos, sa, sb):
    b, t, d = x.shape
    tm = _largest_divisor(t, ROW_TILES)
    tk = _da_key_block(t)
    per_block = tk // tm
    row = lambda width: pl.BlockSpec((None, tm, width), lambda bi, i: (bi, i, 0))
    vec = pl.BlockSpec((None, 1, d), lambda bi, i: (bi, 0, 0))
    tab = pl.BlockSpec((tm, LANES), lambda bi, i: (i, 0))
    widths = (GLA_HEADS * GLA_DK, GLA_HEADS * GLA_DK, GLA_HEADS * GLA_DV,
              _N_GATE, GLA_HEADS * GLA_DV, 2 * d)
    dtypes = (BF16, BF16, BF16, F32, BF16, BF16)
    dq = 2 * DA_DH
    head_shapes = [jax.ShapeDtypeStruct((b, DA_HEADS, dq, t), BF16),
                   jax.ShapeDtypeStruct((b, DA_HEADS, t, dq), BF16),
                   jax.ShapeDtypeStruct((b, DA_HEADS, t // tk, DA_DV, tk), BF16)]
    head_specs = [pl.BlockSpec((None, DA_HEADS, dq, tm), lambda bi, i: (bi, 0, 0, i)),
                  pl.BlockSpec((None, DA_HEADS, tm, dq), lambda bi, i: (bi, 0, i, 0)),
                  pl.BlockSpec((None, DA_HEADS, None, DA_DV, tm),
                               lambda bi, i: (bi, 0, i // per_block, 0, i % per_block))]
    return pl.pallas_call(
        _proj_kernel,
        out_shape=head_shapes + [jax.ShapeDtypeStruct((b, t, w), dt) for w, dt in zip(widths, dtypes)],
        grid=(b, t // tm),
        in_specs=[row(d), vec, vec, _resident((1, d))] + [_resident(w.shape) for w in w_pieces]
        + [_resident(w_up.shape), _resident(b_up.shape), tab, tab, tab],
        out_specs=head_specs + [row(w) for w in widths],
        compiler_params=_params("parallel", "parallel"),
        name="proj",
    )(x, shift, scale, gain, *w_pieces, w_up, b_up, cos, sa, sb)


def _rope_tables(t):
    pos = jnp.arange(t)
    inv = ROPE_THETA ** (-jnp.arange(0, ROPE_AXIS_DIM, 2, dtype=F32) / ROPE_AXIS_DIM)
    ang_r = (pos // GRID_W).astype(F32)[:, None] * inv
    ang_c = (pos % GRID_W).astype(F32)[:, None] * inv
    zero = jnp.zeros_like(ang_r)
    cos = jnp.concatenate([jnp.cos(ang_r)] * 2 + [jnp.cos(ang_c)] * 2, axis=-1)
    sa = jnp.concatenate([zero, jnp.sin(ang_r), zero, jnp.sin(ang_c)], axis=-1)
    sb = jnp.concatenate([-jnp.sin(ang_r), zero, -jnp.sin(ang_c), zero], axis=-1)
    rep = LANES // DA_DH
    return tuple(jnp.tile(a, (1, rep)) for a in (cos, sa, sb))


def _identity_tables(t):
    return jnp.ones((t, LANES), F32), jnp.zeros((t, LANES), F32), jnp.zeros((t, LANES), F32)


def _level_ref(b, s, row, reverse):
    c = b.shape[0]
    if 2 * s >= SUBLANES:
        pieces = []
        for blk in range(c // (2 * s)):
            r = blk * 2 * s + s - 1 + (1 if reverse else 0)
            pieces.append(jnp.broadcast_to(b[r:r + 1, :], (2 * s, b.shape[1])))
        return jnp.concatenate(pieces, axis=0)
    pos = row % (2 * s)
    mid = s if reverse else s - 1
    ref = b
    for p in range(2 * s):
        if p != mid:
            ref = jnp.where(pos == p, pltpu.roll(b, (p - mid) % c, 0), ref)
    return ref


def _split3(x):
    hi = x.astype(BF16)
    r = x - hi.astype(F32)
    mid = r.astype(BF16)
    lo = (r - mid.astype(F32)).astype(BF16)
    return hi, mid, lo


def _chunk_cumsum(g, reverse):
    c = g.shape[0]
    ri = lax.broadcasted_iota(jnp.int32, (c, c), 0)
    ci = lax.broadcasted_iota(jnp.int32, (c, c), 1)
    tri = jnp.where((ci >= ri) if reverse else (ci <= ri), 1.0, 0.0).astype(BF16)
    hi, mid, lo = _split3(g)
    return _mm(tri, lo) + _mm(tri, mid) + _mm(tri, hi)


def _gla_tables(c):
    levels = int(math.log2(c))
    row = np.arange(c)
    side = np.zeros((levels, 2, 2, c, GLA_DK), np.float32)
    same = np.zeros((levels, c, c), np.float32)
    for lvl in range(levels):
        s = 1 << lvl
        upper = (row % (2 * s)) >= s
        for rev in range(2):
            q_side = ~upper if rev else upper
            side[lvl, rev, 0] = np.where(q_side, 0.0, NEG_BIG)[:, None]
            side[lvl, rev, 1] = np.where(q_side, NEG_BIG, 0.0)[:, None]
        same[lvl] = (row[:, None] // (2 * s)) == (row[None, :] // (2 * s))
    causal = np.zeros((len(GLA_BASES), 2, c, c), np.float32)
    for i, base in enumerate(GLA_BASES):
        blk = row // min(base, c)
        inside = blk[:, None] == blk[None, :]
        causal[i, 0] = inside & (row[None, :] <= row[:, None])
        causal[i, 1] = inside & (row[None, :] >= row[:, None])
    return jnp.asarray(side), jnp.asarray(same), jnp.asarray(causal)


def _base_offset(b, base, reverse):
    c, n = b.shape
    pieces = []
    for i in range(c // base):
        r = (i + 1) * base if reverse else i * base - 1
        edge = b[r:r + 1, :] if 0 <= r < c else jnp.zeros((1, n), F32)
        pieces.append(jnp.broadcast_to(edge, (base, n)))
    return b - jnp.concatenate(pieces, axis=0)


def _gla_direction(q, k, v, b, st_ref, side_ref, same_ref, causal_ref, reverse, tier, base_offset):
    c, dk = b.shape
    row = lax.broadcasted_iota(jnp.int32, (c, dk), 0)
    total = b[0:1, :] if reverse else b[c - 1:c, :]
    qf = q.astype(F32)
    kf = k.astype(F32)
    rev = 1 if reverse else 0

    st = st_ref[...]
    out = _nt((qf * jnp.exp2(b)).astype(BF16), st.astype(BF16))

    if tier is None:
        ri = lax.broadcasted_iota(jnp.int32, (c, c), 0)
        ci = lax.broadcasted_iota(jnp.int32, (c, c), 1)
        a = jnp.where(ri == ci, _nt(q, k), 0.0)
        s = 1
    else:
        qd = (qf * jnp.exp2(base_offset)).astype(BF16)
        kd = (kf * jnp.exp2(-base_offset)).astype(BF16)
        a = _nt(qd, kd) * causal_ref[tier, rev]
        s = min(GLA_BASES[tier], c)
    lvl = int(math.log2(s))
    while s < c:
        t = b - _level_ref(b, s, row, reverse)
        ql = (qf * jnp.exp2(t + side_ref[lvl, rev, 0])).astype(BF16)
        kl = (kf * jnp.exp2(side_ref[lvl, rev, 1] - t)).astype(BF16)
        al = _nt(ql, kl)
        a = a + (al if 2 * s == c else al * same_ref[lvl])
        s *= 2
        lvl += 1
    out = out + _mm(a.astype(BF16), v)

    kd = (kf * jnp.exp2(total - b)).astype(BF16)
    st_ref[...] = st * jnp.exp2(total) + _tn(v, kd)
    return out


def _gla_kernel(qf_ref, kf_ref, vf_ref, gf_ref, qb_ref, kb_ref, vb_ref, gb_ref, sf0_ref, sb0_ref,
                side_ref, same_ref, causal_ref, of_ref, ob_ref, sf1_ref, sb1_ref, sf_ref, sb_ref):
    i = pl.program_id(2)
    heads = sf_ref.shape[0]

    @pl.when(i == 0)
    def _():
        sf_ref[...] = sf0_ref[...]
        sb_ref[...] = sb0_ref[...]

    gf = gf_ref[...] * LOG2_E
    gb = gb_ref[...] * LOG2_E
    c = side_ref.shape[-2]
    subs = gf.shape[0] // c
    steepest = jnp.maximum(jnp.max(jnp.abs(gf)), jnp.max(jnp.abs(gb)))
    rows_of = lambda s: slice(s * c, (s + 1) * c)
    bf = [_chunk_cumsum(gf[rows_of(s)], False) for s in range(subs)]
    bb = [_chunk_cumsum(gb[rows_of(s)], True) for s in range(subs)]

    def run(tier):
        base = None if tier is None else min(GLA_BASES[tier], c)
        for n in range(subs):
            sf_, sb_ = n, subs - 1 - n
            rf, rb = rows_of(sf_), rows_of(sb_)
            df = None if tier is None else _base_offset(bf[sf_], base, False)
            db = None if tier is None else _base_offset(bb[sb_], base, True)
            for h in range(heads):
                ks = slice(h * GLA_DK, (h + 1) * GLA_DK)
                vs = slice(h * GLA_DV, (h + 1) * GLA_DV)
                of_ref[rf, vs] = _gla_direction(
                    qf_ref[rf, ks], kf_ref[rf, ks], vf_ref[rf, vs], bf[sf_][:, ks], sf_ref.at[h], side_ref,
                    same_ref, causal_ref, False, tier,
                    None if tier is None else df[:, ks]).astype(of_ref.dtype)
                ob_ref[rb, vs] = _gla_direction(
                    qb_ref[rb, ks], kb_ref[rb, ks], vb_ref[rb, vs], bb[sb_][:, ks], sb_ref.at[h], side_ref,
                    same_ref, causal_ref, True, tier,
                    None if tier is None else db[:, ks]).astype(ob_ref.dtype)

    taken = None
    for tier, base in enumerate(GLA_BASES):
        fits = min(base, c) * steepest <= GLA_BASE_LIMIT
        pl.when(fits if taken is None else jnp.logical_and(fits, jnp.logical_not(taken)))(
            functools.partial(run, tier))
        taken = fits if taken is None else jnp.logical_or(taken, fits)
    pl.when(jnp.logical_not(taken))(functools.partial(run, None))

    @pl.when(i == pl.num_programs(2) - 1)
    def _():
        sf1_ref[...] = sf_ref[...]
        sb1_ref[...] = sb_ref[...]


def _gla_call(gq, gk, gv, gate, sf0, sb0):
    b, t, _ = gq.shape
    c = _largest_divisor(t, (GLA_CHUNK,))
    rows = c * _largest_divisor(t // c, (GLA_STEP_CHUNKS, 1))
    n = t // rows
    hb = GLA_HEAD_BLOCK
    nhb = GLA_HEADS // hb
    fwd = lambda w, off=0: pl.BlockSpec((None, rows, hb * w), lambda bi, h, i: (bi, i, h + off))
    bwd = lambda w, off=0: pl.BlockSpec((None, rows, hb * w), lambda bi, h, i: (bi, n - 1 - i, h + off))
    state = pl.BlockSpec((None, hb, GLA_DV, GLA_DK), lambda bi, h, i: (bi, h, 0, 0))
    o_shape = jax.ShapeDtypeStruct((b, t, GLA_HEADS * GLA_DV), BF16)
    s_shape = jax.ShapeDtypeStruct((b, GLA_HEADS, GLA_DV, GLA_DK), F32)
    side, same, causal = _gla_tables(c)
    return pl.pallas_call(
        _gla_kernel,
        out_shape=[o_shape, o_shape, s_shape, s_shape],
        grid=(b, nhb, n),
        in_specs=[fwd(GLA_DK), fwd(GLA_DK), fwd(GLA_DV), fwd(GLA_DK),
                  bwd(GLA_DK), bwd(GLA_DK), bwd(GLA_DV), bwd(GLA_DK, nhb),
                  state, state, _resident(side.shape), _resident(same.shape), _resident(causal.shape)],
        out_specs=[fwd(GLA_DV), bwd(GLA_DV), state, state],
        scratch_shapes=[pltpu.VMEM((hb, GLA_DV, GLA_DK), F32), pltpu.VMEM((hb, GLA_DV, GLA_DK), F32)],
        compiler_params=_params("parallel", "parallel", "arbitrary"),
        name="gla",
    )(gq, gk, gv, gate, gq, gk, gv, gate, sf0, sb0, side, same, causal)


def _da_kernel(lq1_ref, lk1_ref, lq2_ref, lk2_ref, gain_ref, qt_ref, k_ref, vt_ref, ck_ref, cvt_ref,
               y_ref, kmax_ref, *, lam_init):
    tq = qt_ref.shape[-1]
    nk, tk, dq = k_ref.shape
    nck = ck_ref.shape[0]
    qt = qt_ref[...]
    sub = lax.broadcasted_iota(jnp.int32, qt.shape, 0)
    zero = jnp.zeros_like(qt)
    qm = jnp.concatenate([jnp.where(sub < DA_DH, qt, zero), jnp.where(sub < DA_DH, zero, qt)], axis=1)

    @pl.when(pl.program_id(2) == 0)
    def _():
        di = lax.broadcasted_iota(jnp.int32, (dq, dq), 0)
        ci = lax.broadcasted_iota(jnp.int32, (dq, dq), 1)
        same_comp = jnp.where((di < DA_DH) == (ci < DA_DH), 1.0, 0.0).astype(BF16)

        def block_max(kb, mx):
            nsq = _mm(kb * kb, same_comp)
            return jnp.maximum(mx, jnp.max(nsq.reshape(-1, SUBLANES, dq), axis=0))

        mx = jnp.zeros((SUBLANES, dq), F32)
        for kb in [k_ref[j] for j in range(nk)] + [ck_ref[j] for j in range(nck)]:
            mx = block_max(kb, mx)
        kmax_ref[...] = jnp.broadcast_to(jnp.max(mx, axis=0, keepdims=True), kmax_ref.shape)

    qf = qt.astype(F32)
    qsq = qf * qf
    kmsq = kmax_ref[...]
    bsq = jnp.concatenate([jnp.sum(qsq[:DA_DH], axis=0, keepdims=True) * kmsq[0:1, 0:1],
                           jnp.sum(qsq[DA_DH:], axis=0, keepdims=True) * kmsq[0:1, DA_DH:DA_DH + 1]],
                          axis=1)
    bound = jnp.sqrt(bsq) * DA_BOUND_SLACK
    in_range = jnp.max(bound) <= DA_BOUND_LIMIT

    def probs(kb, shift):
        s = _mm(kb, qm)
        p = jnp.exp2(s.reshape(-1, SUBLANES, s.shape[1]) - shift[None])
        return p.reshape(s.shape)

    def finish(acc, l):
        lam = (jnp.exp(jnp.sum(lq1_ref[...] * lk1_ref[...], keepdims=True))
               - jnp.exp(jnp.sum(lq2_ref[...] * lk2_ref[...], keepdims=True)) + lam_init)
        o = acc / l
        ot = o[:, :tq] - lam * o[:, tq:]
        ms = jnp.mean(ot * ot, axis=0, keepdims=True)
        yt = ot * lax.rsqrt(ms + EPS) * (gain_ref[...] * (1.0 - lam_init))
        y_ref[...] = yt.T.astype(y_ref.dtype)

    @pl.when(in_range)
    def _():
        shift = jnp.broadcast_to(bound, (SUBLANES, 2 * tq))
        acc = jnp.zeros((DA_DV, 2 * tq), F32)
        l = jnp.zeros((1, 2 * tq), F32)
        blocks = [(k_ref[j], vt_ref[j]) for j in range(nk)] + [(ck_ref[j], cvt_ref[j]) for j in range(nck)]
        for kb, vb in blocks:
            p = probs(kb, shift)
            l = l + jnp.sum(p, axis=0, keepdims=True)
            acc = acc + _mm(vb, p.astype(BF16))
        finish(acc, l)

    @pl.when(jnp.logical_not(in_range))
    def _():
        def block(kb, vb, carry):
            m_prev, l, acc = carry
            s = _mm(kb, qm)
            m_new = jnp.maximum(m_prev, jnp.max(s, axis=0, keepdims=True))
            alpha = jnp.exp2(m_prev - m_new)
            p = jnp.exp2(s - m_new)
            return (m_new, alpha * l + jnp.sum(p, axis=0, keepdims=True),
                    alpha * acc + _mm(vb, p.astype(BF16)))

        carry = (jnp.full((1, 2 * tq), NEG_BIG, F32), jnp.zeros((1, 2 * tq), F32),
                 jnp.zeros((DA_DV, 2 * tq), F32))
        carry = lax.fori_loop(0, nk, lambda j, c: block(k_ref[j], vt_ref[j], c), carry)
        for j in range(nck):
            carry = block(ck_ref[j], cvt_ref[j], carry)
        finish(carry[2], carry[1])


def _da_key_block(t):
    return _largest_divisor(t, DA_KEY_BLOCKS)


def _da_call(lams, gain, qt, k5, vt5, ck5, cvt5, lam_init):
    b, h, dq, t = qt.shape
    tq = _largest_divisor(t, DA_QUERY_BLOCKS)
    small = pl.BlockSpec((1, DA_DH), lambda bi, hi, i: (0, 0))
    per_head = lambda a: pl.BlockSpec((None, None) + a.shape[2:], lambda bi, hi, i: (bi, hi, 0, 0, 0))
    return pl.pallas_call(
        functools.partial(_da_kernel, lam_init=lam_init),
        out_shape=jax.ShapeDtypeStruct((b, t, h * DA_DV), BF16),
        grid=(b, h, t // tq),
        in_specs=[small, small, small, small,
                  pl.BlockSpec((DA_DV, 1), lambda bi, hi, i: (0, 0)),
                  pl.BlockSpec((None, None, dq, tq), lambda bi, hi, i: (bi, hi, 0, i)),
                  per_head(k5), per_head(vt5), per_head(ck5), per_head(cvt5)],
        out_specs=pl.BlockSpec((None, tq, DA_DV), lambda bi, hi, i: (bi, i, hi)),
        scratch_shapes=[pltpu.VMEM((SUBLANES, dq), F32)],
        compiler_params=_params("parallel", "parallel", "arbitrary"),
        name="da",
    )(*lams, gain, qt, k5, vt5, ck5, cvt5)


def _merge_kernel(x_ref, gt_ref, yda_ref, of_ref, ob_ref, gout_ref, mrg_ref, ggain_ref, pgain_ref,
                  wda_ref, wgla_ref, wout_ref, o_ref):
    d = x_ref.shape[-1]
    o_gla = of_ref[...].astype(F32) + ob_ref[...].astype(F32)
    ggain = ggain_ref[...]
    heads = [_rms(o_gla[:, h * GLA_DV:(h + 1) * GLA_DV], ggain) for h in range(GLA_HEADS)]
    y_gla = (jnp.concatenate(heads, axis=-1) * gout_ref[...].astype(F32)).astype(BF16)
    gates = mrg_ref[...].astype(F32)
    y = (gates[:, :d] * _mm(yda_ref[...], wda_ref[...])
         + gates[:, d:] * _mm(y_gla, wgla_ref[...]))
    z = _mm(y.astype(BF16), wout_ref[...])
    o_ref[...] = x_ref[...] + gt_ref[...] * _rms(z, pgain_ref[...])


def _merge_call(x, gt1, y_da, o_f, o_b, gout, mrg, ggain, pgain, w_da, w_gla, w_out):
    b, t, d = x.shape
    tm = _largest_divisor(t, ROW_TILES)
    row = lambda width: pl.BlockSpec((None, tm, width), lambda bi, i: (bi, i, 0))
    vec = pl.BlockSpec((None, 1, d), lambda bi, i: (bi, 0, 0))
    return pl.pallas_call(
        _merge_kernel,
        out_shape=jax.ShapeDtypeStruct((b, t, d), F32),
        grid=(b, t // tm),
        in_specs=[row(d), vec, row(d), row(d), row(d), row(d), row(2 * d),
                  _resident(ggain.shape), _resident(pgain.shape),
                  _resident(w_da.shape), _resident(w_gla.shape), _resident(w_out.shape)],
        out_specs=row(d),
        compiler_params=_params("parallel", "parallel"),
        name="merge",
    )(x, gt1, y_da, o_f, o_b, gout, mrg, ggain, pgain, w_da, w_gla, w_out)


def _mlp_kernel(x_ref, sh_ref, sc_ref, gt_ref, pre_ref, post_ref, w1_ref, w2_ref, o_ref):
    x = x_ref[...]
    h = (_rms(x, pre_ref[...]) * (1.0 + sc_ref[...]) + sh_ref[...]).astype(BF16)
    u = jnp.maximum(_mm(h, w1_ref[...]), 0.0)
    z = _mm((u * u).astype(BF16), w2_ref[...])
    o_ref[...] = x + gt_ref[...] * _rms(z, post_ref[...])


def _mlp_call(x, sh2, sc2, gt2, pre, post, w1, w2):
    b, t, d = x.shape
    tm = _largest_divisor(t, ROW_TILES)
    row = pl.BlockSpec((None, tm, d), lambda bi, i: (bi, i, 0))
    vec = pl.BlockSpec((None, 1, d), lambda bi, i: (bi, 0, 0))
    return pl.pallas_call(
        _mlp_kernel,
        out_shape=jax.ShapeDtypeStruct((b, t, d), F32),
        grid=(b, t // tm),
        in_specs=[row, vec, vec, vec, _resident(pre.shape), _resident(post.shape),
                  _resident(w1.shape), _resident(w2.shape)],
        out_specs=row,
        compiler_params=_params("parallel", "parallel"),
        name="mlp",
    )(x, sh2, sc2, gt2, pre, post, w1, w2)


def _split_w_in(w):
    low0 = _OFF_LOW
    low1 = low0 + 2 * GLA_GATE_RANK
    low = jnp.pad(w[:, low0:low1], ((0, 0), (0, LOW_PAD - 2 * GLA_GATE_RANK)))
    return w[:, :low0].astype(BF16), low.astype(BF16), w[:, low1:].astype(BF16)


def _pack_gate_up(w_gate_up, b_gate_up):
    hk = GLA_HEADS * GLA_DK
    w = jnp.zeros((LOW_PAD, 2 * hk), F32)
    for z in range(2):
        w = w.at[z * GLA_GATE_RANK:(z + 1) * GLA_GATE_RANK, z * hk:(z + 1) * hk].set(w_gate_up[z])
    return w.astype(BF16), b_gate_up.reshape(1, 2 * hk).astype(F32)


def kernel(x, c, ctx, c_ctx, w_mod, b_mod, pre_norm1, w_in, w_gate_up, b_gate_up, lambda_q1, lambda_k1,
           lambda_q2, lambda_k2, da_head_norm, gla_head_norm, w_branch_da, w_branch_gla, w_out, post_norm1,
           pre_norm2, w_ff1, w_ff2, post_norm2):
    b, t, d = x.shape
    n_ctx = ctx.shape[1]
    assert w_in.shape[0] == 1, "single-layer block only"
    assert d == DA_HEADS * DA_DV == GLA_HEADS * GLA_DV and t % GRID_W == 0
    layer = 0
    lam_init = 0.8 - 0.6 * math.exp(-0.3 * layer)
    (w_mod, b_mod, pre_norm1, w_in, w_gate_up, b_gate_up, lambda_q1, lambda_k1, lambda_q2, lambda_k2,
     da_head_norm, gla_head_norm, w_branch_da, w_branch_gla, w_out, post_norm1, pre_norm2, w_ff1, w_ff2,
     post_norm2) = (a.reshape(a.shape[1:]) for a in (
         w_mod, b_mod, pre_norm1, w_in, w_gate_up, b_gate_up, lambda_q1, lambda_k1, lambda_q2, lambda_k2,
         da_head_norm, gla_head_norm, w_branch_da, w_branch_gla, w_out, post_norm1, pre_norm2, w_ff1,
         w_ff2, post_norm2))

    rows = -(-(b + 1) // SUBLANES) * SUBLANES
    cc = jnp.zeros((rows, d), F32).at[:b].set(c).at[b].set(c_ctx)
    mod = _mod_call(cc, w_mod, b_mod)
    lat = [mod[:b, i * d:(i + 1) * d].reshape(b, 1, d) for i in range(N_MOD)]
    cmod = [jnp.broadcast_to(mod[b, i * d:(i + 1) * d].reshape(1, 1, d), (b, 1, d)) for i in range(2)]
    sh1, sc1, gt1, sh2, sc2, gt2 = lat

    w_pieces = _split_w_in(w_in)
    w_up, b_up = _pack_gate_up(w_gate_up, b_gate_up)
    gain1 = pre_norm1.reshape(1, d)

    qt, k, vt5, gq, gk, gv, gate, gout, mrg = _proj_call(
        x, sh1, sc1, gain1, w_pieces, w_up, b_up, *_rope_tables(t))
    _, ck, cvt5, _, cgk, cgv, cgate, _, _ = _proj_call(
        ctx, cmod[0], cmod[1], gain1, w_pieces, w_up, b_up, *_identity_tables(n_ctx))

    zero_state = jnp.zeros((b, GLA_HEADS, GLA_DV, GLA_DK), F32)
    _, _, sf0, sb0 = _gla_call(jnp.zeros_like(cgk), cgk, cgv, cgate, zero_state, zero_state)
    o_f, o_b, _, _ = _gla_call(gq, gk, gv, gate, sf0, sb0)

    k5 = k.reshape(b, DA_HEADS, vt5.shape[2], vt5.shape[4], 2 * DA_DH)
    ck5 = ck.reshape(b, DA_HEADS, cvt5.shape[2], cvt5.shape[4], 2 * DA_DH)
    lams = [a.reshape(1, DA_DH) for a in (lambda_q1, lambda_k1, lambda_q2, lambda_k2)]
    y_da = _da_call(lams, da_head_norm.reshape(DA_DV, 1), qt, k5, vt5, ck5, cvt5, lam_init)

    x1 = _merge_call(x, gt1, y_da, o_f, o_b, gout, mrg,
                     gla_head_norm.reshape(1, GLA_DV), post_norm1.reshape(1, d),
                     w_branch_da.astype(BF16), w_branch_gla.astype(BF16),
                     w_out.astype(BF16))
    return _mlp_call(x1, sh2, sc2, gt2, pre_norm2.reshape(1, d), post_norm2.reshape(1, d),
                     w_ff1.astype(BF16), w_ff2.astype(BF16))
```

```python
import functools
import math

import numpy as np
import jax
import jax.numpy as jnp
from jax import lax
from jax.experimental import pallas as pl
from jax.experimental.pallas import tpu as pltpu

F32 = jnp.float32
BF16 = jnp.bfloat16

GRID_W = 64
DA_HEADS = 8
DA_DH = 64
DA_DV = 2 * DA_DH
GLA_HEADS = 4
GLA_DK = 128
GLA_DV = 256
GLA_GATE_RANK = 16
GLA_GATE_NORM = 16.0
ROPE_THETA = 10000.0
ROPE_AXIS_DIM = DA_DH // 2
ROPE_HALF = ROPE_AXIS_DIM // 2
N_MOD = 6
EPS = 1e-6
NEG_BIG = -1e30

LANES = 128
SUBLANES = 8
LOG2_E = math.log2(math.e)
VMEM_LIMIT = 56 * 1024 * 1024

ROW_TILES = (512, 256, 128)
GLA_CHUNK = 128
GLA_STEP_CHUNKS = 4
GLA_BASES = (128, 16)
GLA_BASE_LIMIT = 100.0
GLA_HEAD_BLOCK = 4
DA_KEY_BLOCKS = (1024, 512, 256, 128)
DA_QUERY_BLOCKS = (1024, 512, 256, 128)
DA_BOUND_LIMIT = 50.0
DA_BOUND_SLACK = 1.01
LOW_PAD = LANES


def _nt(a, b):
    return lax.dot_general(a, b, (((1,), (1,)), ((), ())), preferred_element_type=F32)


def _tn(a, b):
    return lax.dot_general(a, b, (((0,), (0,)), ((), ())), preferred_element_type=F32)


def _mm(a, b):
    return jnp.dot(a, b, preferred_element_type=F32)


def _rms(x, gain):
    return x * lax.rsqrt(jnp.mean(x * x, axis=-1, keepdims=True) + EPS) * gain


def _params(*sem):
    return pltpu.CompilerParams(dimension_semantics=sem, vmem_limit_bytes=VMEM_LIMIT)


def _resident(shape):
    nd = len(shape)
    return pl.BlockSpec(shape, lambda *_: (0,) * nd, pipeline_mode=pl.Buffered(1))


def _largest_divisor(n, candidates):
    for cand in candidates:
        if n % cand == 0:
            return cand
    raise ValueError(f"no tile in {candidates} divides {n}")


def _mod_kernel(c_ref, w_ref, b_ref, o_ref):
    c = c_ref[...]
    s = (c * jax.nn.sigmoid(c)).astype(BF16)
    o_ref[...] = _mm(s, w_ref[...].astype(BF16)) + b_ref[...]


def _mod_call(cc, w_mod, b_mod):
    rows, d = cc.shape
    n = w_mod.shape[1]
    bn = _largest_divisor(n, (1024, 512, 256, 128))
    return pl.pallas_call(
        _mod_kernel,
        out_shape=jax.ShapeDtypeStruct((rows, n), F32),
        grid=(n // bn,),
        in_specs=[pl.BlockSpec((rows, d), lambda j: (0, 0)),
                  pl.BlockSpec((d, bn), lambda j: (0, j)),
                  pl.BlockSpec((1, bn), lambda j: (0, j))],
        out_specs=pl.BlockSpec((rows, bn), lambda j: (0, j)),
        compiler_params=_params("arbitrary"),
        name="mod",
    )(cc, w_mod, b_mod.reshape(1, n))


_OFF_Q = 0
_OFF_K = _OFF_Q + DA_HEADS * 2 * DA_DH
_OFF_V = _OFF_K + DA_HEADS * 2 * DA_DH
_OFF_GQ = _OFF_V + DA_HEADS * DA_DV
_OFF_GK = _OFF_GQ + GLA_HEADS * GLA_DK
_OFF_GV = _OFF_GK + GLA_HEADS * GLA_DK
_OFF_LOW = _OFF_GV + GLA_HEADS * GLA_DV
_TAIL_GOUT = 0
_TAIL_MERGE = _TAIL_GOUT + GLA_HEADS * GLA_DV
_N_GATE = 2 * GLA_HEADS * GLA_DK


def _proj_kernel(x_ref, sh_ref, sc_ref, gain_ref, w_ref, wlow_ref, wtail_ref, wup_ref, bup_ref,
                 cos_ref, sa_ref, sb_ref,
                 q_ref, k_ref, v_ref, gq_ref, gk_ref, gv_ref, gate_ref, gout_ref, mrg_ref):
    d = x_ref.shape[-1]
    h = (_rms(x_ref[...], gain_ref[...]) * (1.0 + sc_ref[...]) + sh_ref[...]).astype(BF16)
    cos, sa, sb = cos_ref[...], sa_ref[...], sb_ref[...]

    def roped(off, scale):
        acc = _mm(h, w_ref[:, off:off + d])
        for s in range(d // LANES):
            a = acc[:, s * LANES:(s + 1) * LANES]
            r = (a * cos + pltpu.roll(a, ROPE_HALF, 1) * sa
                 + pltpu.roll(a, LANES - ROPE_HALF, 1) * sb)
            yield s, r * scale

    low = _mm(h, wlow_ref[...]).astype(BF16)
    z = _mm(low, wup_ref[...]) + bup_ref[...]
    log_sig = jnp.minimum(z, 0.0) - jnp.log1p(jnp.exp(-jnp.abs(z)))
    gate_ref[...] = log_sig / GLA_GATE_NORM
    mg = _mm(h, wtail_ref[:, _TAIL_MERGE:_TAIL_MERGE + 2 * d])
    mrg_ref[...] = jax.nn.sigmoid(mg).astype(mrg_ref.dtype)
    go = _mm(h, wtail_ref[:, _TAIL_GOUT:_TAIL_MERGE])
    gout_ref[...] = (go * jax.nn.sigmoid(go)).astype(gout_ref.dtype)
    for hd, slab in roped(_OFF_Q, DA_DH ** -0.5 * LOG2_E):
        q_ref[hd] = slab.T.astype(q_ref.dtype)
    for hd, slab in roped(_OFF_K, 1.0):
        k_ref[hd] = slab.astype(k_ref.dtype)
    v = _mm(h, w_ref[:, _OFF_V:_OFF_GQ])
    for hd in range(DA_HEADS):
        v_ref[hd] = v[:, hd * DA_DV:(hd + 1) * DA_DV].T.astype(v_ref.dtype)
    gq_ref[...] = (_mm(h, w_ref[:, _OFF_GQ:_OFF_GK]) * (GLA_DK ** -0.5)).astype(gq_ref.dtype)
    gk_ref[...] = _mm(h, w_ref[:, _OFF_GK:_OFF_GV]).astype(gk_ref.dtype)
    gv_ref[...] = _mm(h, w_ref[:, _OFF_GV:_OFF_LOW]).astype(gv_ref.dtype)


def _proj_call(x, shift, scale, gain, w_pieces, w_up, b_up, cos, sa, sb):
    b, t, d = x.shape
    tm = _largest_divisor(t, ROW_TILES)
    tk = _da_key_block(t)
    per_block = tk // tm
    row = lambda width: pl.BlockSpec((None, tm, width), lambda bi, i: (bi, i, 0))
    vec = pl.BlockSpec((None, 1, d), lambda bi, i: (bi, 0, 0))
    tab = pl.BlockSpec((tm, LANES), lambda bi, i: (i, 0))
    widths = (GLA_HEADS * GLA_DK, GLA_HEADS * GLA_DK, GLA_HEADS * GLA_DV,
              _N_GATE, GLA_HEADS * GLA_DV, 2 * d)
    dtypes = (BF16, BF16, BF16, F32, BF16, BF16)
    dq = 2 * DA_DH
    head_shapes = [jax.ShapeDtypeStruct((b, DA_HEADS, dq, t), BF16),
                   jax.ShapeDtypeStruct((b, DA_HEADS, t, dq), BF16),
                   jax.ShapeDtypeStruct((b, DA_HEADS, t // tk, DA_DV, tk), BF16)]
    head_specs = [pl.BlockSpec((None, DA_HEADS, dq, tm), lambda bi, i: (bi, 0, 0, i)),
                  pl.BlockSpec((None, DA_HEADS, tm, dq), lambda bi, i: (bi, 0, i, 0)),
                  pl.BlockSpec((None, DA_HEADS, None, DA_DV, tm),
                               lambda bi, i: (bi, 0, i // per_block, 0, i % per_block))]
    return pl.pallas_call(
        _proj_kernel,
        out_shape=head_shapes + [jax.ShapeDtypeStruct((b, t, w), dt) for w, dt in zip(widths, dtypes)],
        grid=(b, t // tm),
        in_specs=[row(d), vec, vec, _resident((1, d))] + [_resident(w.shape) for w in w_pieces]
        + [_resident(w_up.shape), _resident(b_up.shape), tab, tab, tab],
        out_specs=head_specs + [row(w) for w in widths],
        compiler_params=_params("parallel", "parallel"),
        name="proj",
    )(x, shift, scale, gain, *w_pieces, w_up, b_up, cos, sa, sb)


def _rope_tables(t):
    pos = jnp.arange(t)
    inv = ROPE_THETA ** (-jnp.arange(0, ROPE_AXIS_DIM, 2, dtype=F32) / ROPE_AXIS_DIM)
    ang_r = (pos // GRID_W).astype(F32)[:, None] * inv
    ang_c = (pos % GRID_W).astype(F32)[:, None] * inv
    zero = jnp.zeros_like(ang_r)
    cos = jnp.concatenate([jnp.cos(ang_r)] * 2 + [jnp.cos(ang_c)] * 2, axis=-1)
    sa = jnp.concatenate([zero, jnp.sin(ang_r), zero, jnp.sin(ang_c)], axis=-1)
    sb = jnp.concatenate([-jnp.sin(ang_r), zero, -jnp.sin(ang_c), zero], axis=-1)
    rep = LANES // DA_DH
    return tuple(jnp.tile(a, (1, rep)) for a in (cos, sa, sb))


def _identity_tables(t):
    return jnp.ones((t, LANES), F32), jnp.zeros((t, LANES), F32), jnp.zeros((t, LANES), F32)


def _level_ref(b, s, row, reverse):
    c = b.shape[0]
    if 2 * s >= SUBLANES:
        pieces = []
        for blk in range(c // (2 * s)):
            r = blk * 2 * s + s - 1 + (1 if reverse else 0)
            pieces.append(jnp.broadcast_to(b[r:r + 1, :], (2 * s, b.shape[1])))
        return jnp.concatenate(pieces, axis=0)
    pos = row % (2 * s)
    mid = s if reverse else s - 1
    ref = b
    for p in range(2 * s):
        if p != mid:
            ref = jnp.where(pos == p, pltpu.roll(b, (p - mid) % c, 0), ref)
    return ref


def _split3(x):
    hi = x.astype(BF16)
    r = x - hi.astype(F32)
    mid = r.astype(BF16)
    lo = (r - mid.astype(F32)).astype(BF16)
    return hi, mid, lo


def _chunk_cumsum(g, reverse):
    c = g.shape[0]
    ri = lax.broadcasted_iota(jnp.int32, (c, c), 0)
    ci = lax.broadcasted_iota(jnp.int32, (c, c), 1)
    tri = jnp.where((ci >= ri) if reverse else (ci <= ri), 1.0, 0.0).astype(BF16)
    hi, mid, lo = _split3(g)
    return _mm(tri, lo) + _mm(tri, mid) + _mm(tri, hi)


def _gla_tables(c):
    levels = int(math.log2(c))
    row = np.arange(c)
    side = np.zeros((levels, 2, 2, c, GLA_DK), np.float32)
    same = np.zeros((levels, c, c), np.float32)
    for lvl in range(levels):
        s = 1 << lvl
        upper = (row % (2 * s)) >= s
        for rev in range(2):
            q_side = ~upper if rev else upper
            side[lvl, rev, 0] = np.where(q_side, 0.0, NEG_BIG)[:, None]
            side[lvl, rev, 1] = np.where(q_side, NEG_BIG, 0.0)[:, None]
        same[lvl] = (row[:, None] // (2 * s)) == (row[None, :] // (2 * s))
    causal = np.zeros((len(GLA_BASES), 2, c, c), np.float32)
    for i, base in enumerate(GLA_BASES):
        blk = row // min(base, c)
        inside = blk[:, None] == blk[None, :]
        causal[i, 0] = inside & (row[None, :] <= row[:, None])
        causal[i, 1] = inside & (row[None, :] >= row[:, None])
    return jnp.asarray(side), jnp.asarray(same), jnp.asarray(causal)


def _base_offset(b, base, reverse):
    c, n = b.shape
    pieces = []
    for i in range(c // base):
        r = (i + 1) * base if reverse else i * base - 1
        edge = b[r:r + 1, :] if 0 <= r < c else jnp.zeros((1, n), F32)
        pieces.append(jnp.broadcast_to(edge, (base, n)))
    return b - jnp.concatenate(pieces, axis=0)


def _gla_direction(q, k, v, b, st_ref, side_ref, same_ref, causal_ref, reverse, tier, base_offset):
    c, dk = b.shape
    row = lax.broadcasted_iota(jnp.int32, (c, dk), 0)
    total = b[0:1, :] if reverse else b[c - 1:c, :]
    qf = q.astype(F32)
    kf = k.astype(F32)
    rev = 1 if reverse else 0

    st = st_ref[...]
    out = _nt((qf * jnp.exp2(b)).astype(BF16), st.astype(BF16))

    if tier is None:
        ri = lax.broadcasted_iota(jnp.int32, (c, c), 0)
        ci = lax.broadcasted_iota(jnp.int32, (c, c), 1)
        a = jnp.where(ri == ci, _nt(q, k), 0.0)
        s = 1
    else:
        qd = (qf * jnp.exp2(base_offset)).astype(BF16)
        kd = (kf * jnp.exp2(-base_offset)).astype(BF16)
        a = _nt(qd, kd) * causal_ref[tier, rev]
        s = min(GLA_BASES[tier], c)
    lvl = int(math.log2(s))
    while s < c:
        t = b - _level_ref(b, s, row, reverse)
        ql = (qf * jnp.exp2(t + side_ref[lvl, rev, 0])).astype(BF16)
        kl = (kf * jnp.exp2(side_ref[lvl, rev, 1] - t)).astype(BF16)
        al = _nt(ql, kl)
        a = a + (al if 2 * s == c else al * same_ref[lvl])
        s *= 2
        lvl += 1
    out = out + _mm(a.astype(BF16), v)

    kd = (kf * jnp.exp2(total - b)).astype(BF16)
    st_ref[...] = st * jnp.exp2(total) + _tn(v, kd)
    return out


def _gla_kernel(qf_ref, kf_ref, vf_ref, gf_ref, qb_ref, kb_ref, vb_ref, gb_ref, sf0_ref, sb0_ref,
                side_ref, same_ref, causal_ref, of_ref, ob_ref, sf1_ref, sb1_ref, sf_ref, sb_ref):
    i = pl.program_id(2)
    heads = sf_ref.shape[0]

    @pl.when(i == 0)
    def _():
        sf_ref[...] = sf0_ref[...]
        sb_ref[...] = sb0_ref[...]

    gf = gf_ref[...] * LOG2_E
    gb = gb_ref[...] * LOG2_E
    c = side_ref.shape[-2]
    subs = gf.shape[0] // c
    steepest = jnp.maximum(jnp.max(jnp.abs(gf)), jnp.max(jnp.abs(gb)))
    rows_of = lambda s: slice(s * c, (s + 1) * c)
    bf = [_chunk_cumsum(gf[rows_of(s)], False) for s in range(subs)]
    bb = [_chunk_cumsum(gb[rows_of(s)], True) for s in range(subs)]

    def run(tier):
        base = None if tier is None else min(GLA_BASES[tier], c)
        for n in range(subs):
            sf_, sb_ = n, subs - 1 - n
            rf, rb = rows_of(sf_), rows_of(sb_)
            df = None if tier is None else _base_offset(bf[sf_], base, False)
            db = None if tier is None else _base_offset(bb[sb_], base, True)
            for h in range(heads):
                ks = slice(h * GLA_DK, (h + 1) * GLA_DK)
                vs = slice(h * GLA_DV, (h + 1) * GLA_DV)
                of_ref[rf, vs] = _gla_direction(
                    qf_ref[rf, ks], kf_ref[rf, ks], vf_ref[rf, vs], bf[sf_][:, ks], sf_ref.at[h], side_ref,
                    same_ref, causal_ref, False, tier,
                    None if tier is None else df[:, ks]).astype(of_ref.dtype)
                ob_ref[rb, vs] = _gla_direction(
                    qb_ref[rb, ks], kb_ref[rb, ks], vb_ref[rb, vs], bb[sb_][:, ks], sb_ref.at[h], side_ref,
                    same_ref, causal_ref, True, tier,
                    None if tier is None else db[:, ks]).astype(ob_ref.dtype)

    taken = None
    for tier, base in enumerate(GLA_BASES):
        fits = min(base, c) * steepest <= GLA_BASE_LIMIT
        pl.when(fits if taken is None else jnp.logical_and(fits, jnp.logical_not(taken)))(
            functools.partial(run, tier))
        taken = fits if taken is None else jnp.logical_or(taken, fits)
    pl.when(jnp.logical_not(taken))(functools.partial(run, None))

    @pl.when(i == pl.num_programs(2) - 1)
    def _():
        sf1_ref[...] = sf_ref[...]
        sb1_ref[...] = sb_ref[...]


def _gla_call(gq, gk, gv, gate, sf0, sb0):
    b, t, _ = gq.shape
    c = _largest_divisor(t, (GLA_CHUNK,))
    rows = c * _largest_divisor(t // c, (GLA_STEP_CHUNKS, 1))
    n = t // rows
    hb = GLA_HEAD_BLOCK
    nhb = GLA_HEADS // hb
    fwd = lambda w, off=0: pl.BlockSpec((None, rows, hb * w), lambda bi, h, i: (bi, i, h + off))
    bwd = lambda w, off=0: pl.BlockSpec((None, rows, hb * w), lambda bi, h, i: (bi, n - 1 - i, h + off))
    state = pl.BlockSpec((None, hb, GLA_DV, GLA_DK), lambda bi, h, i: (bi, h, 0, 0))
    o_shape = jax.ShapeDtypeStruct((b, t, GLA_HEADS * GLA_DV), BF16)
    s_shape = jax.ShapeDtypeStruct((b, GLA_HEADS, GLA_DV, GLA_DK), F32)
    side, same, causal = _gla_tables(c)
    return pl.pallas_call(
        _gla_kernel,
        out_shape=[o_shape, o_shape, s_shape, s_shape],
        grid=(b, nhb, n),
        in_specs=[fwd(GLA_DK), fwd(GLA_DK), fwd(GLA_DV), fwd(GLA_DK),
                  bwd(GLA_DK), bwd(GLA_DK), bwd(GLA_DV), bwd(GLA_DK, nhb),
                  state, state, _resident(side.shape), _resident(same.shape), _resident(causal.shape)],
        out_specs=[fwd(GLA_DV), bwd(GLA_DV), state, state],
        scratch_shapes=[pltpu.VMEM((hb, GLA_DV, GLA_DK), F32), pltpu.VMEM((hb, GLA_DV, GLA_DK), F32)],
        compiler_params=_params("parallel", "parallel", "arbitrary"),
        name="gla",
    )(gq, gk, gv, gate, gq, gk, gv, gate, sf0, sb0, side, same, causal)


def _da_kernel(lq1_ref, lk1_ref, lq2_ref, lk2_ref, gain_ref, qt_ref, k_ref, vt_ref, ck_ref, cvt_ref,
               y_ref, kmax_ref, *, lam_init):
    tq = qt_ref.shape[-1]
    nk, tk, dq = k_ref.shape
    nck = ck_ref.shape[0]
    qt = qt_ref[...]
    sub = lax.broadcasted_iota(jnp.int32, qt.shape, 0)
    zero = jnp.zeros_like(qt)
    qm = jnp.concatenate([jnp.where(sub < DA_DH, qt, zero), jnp.where(sub < DA_DH, zero, qt)], axis=1)

    @pl.when(pl.program_id(2) == 0)
    def _():
        di = lax.broadcasted_iota(jnp.int32, (dq, dq), 0)
        ci = lax.broadcasted_iota(jnp.int32, (dq, dq), 1)
        same_comp = jnp.where((di < DA_DH) == (ci < DA_DH), 1.0, 0.0).astype(BF16)

        def block_max(kb, mx):
            nsq = _mm(kb * kb, same_comp)
            return jnp.maximum(mx, jnp.max(nsq.reshape(-1, SUBLANES, dq), axis=0))

        mx = jnp.zeros((SUBLANES, dq), F32)
        for kb in [k_ref[j] for j in range(nk)] + [ck_ref[j] for j in range(nck)]:
            mx = block_max(kb, mx)
        kmax_ref[...] = jnp.broadcast_to(jnp.max(mx, axis=0, keepdims=True), kmax_ref.shape)

    qf = qt.astype(F32)
    qsq = qf * qf
    kmsq = kmax_ref[...]
    bsq = jnp.concatenate([jnp.sum(qsq[:DA_DH], axis=0, keepdims=True) * kmsq[0:1, 0:1],
                           jnp.sum(qsq[DA_DH:], axis=0, keepdims=True) * kmsq[0:1, DA_DH:DA_DH + 1]],
                          axis=1)
    bound = jnp.sqrt(bsq) * DA_BOUND_SLACK
    in_range = jnp.max(bound) <= DA_BOUND_LIMIT

    def probs(kb, shift):
        s = _mm(kb, qm)
        p = jnp.exp2(s.reshape(-1, SUBLANES, s.shape[1]) - shift[None])
        return p.reshape(s.shape)

    def finish(acc, l):
        lam = (jnp.exp(jnp.sum(lq1_ref[...] * lk1_ref[...], keepdims=True))
               - jnp.exp(jnp.sum(lq2_ref[...] * lk2_ref[...], keepdims=True)) + lam_init)
        o = acc / l
        ot = o[:, :tq] - lam * o[:, tq:]
        ms = jnp.mean(ot * ot, axis=0, keepdims=True)
        yt = ot * lax.rsqrt(ms + EPS) * (gain_ref[...] * (1.0 - lam_init))
        y_ref[...] = yt.T.astype(y_ref.dtype)

    @pl.when(in_range)
    def _():
        shift = jnp.broadcast_to(bound, (SUBLANES, 2 * tq))
        acc = jnp.zeros((DA_DV, 2 * tq), F32)
        l = jnp.zeros((1, 2 * tq), F32)
        blocks = [(k_ref[j], vt_ref[j]) for j in range(nk)] + [(ck_ref[j], cvt_ref[j]) for j in range(nck)]
        for kb, vb in blocks:
            p = probs(kb, shift)
            l = l + jnp.sum(p, axis=0, keepdims=True)
            acc = acc + _mm(vb, p.astype(BF16))
        finish(acc, l)

    @pl.when(jnp.logical_not(in_range))
    def _():
        def block(kb, vb, carry):
            m_prev, l, acc = carry
            s = _mm(kb, qm)
            m_new = jnp.maximum(m_prev, jnp.max(s, axis=0, keepdims=True))
            alpha = jnp.exp2(m_prev - m_new)
            p = jnp.exp2(s - m_new)
            return (m_new, alpha * l + jnp.sum(p, axis=0, keepdims=True),
                    alpha * acc + _mm(vb, p.astype(BF16)))

        carry = (jnp.full((1, 2 * tq), NEG_BIG, F32), jnp.zeros((1, 2 * tq), F32),
                 jnp.zeros((DA_DV, 2 * tq), F32))
        carry = lax.fori_loop(0, nk, lambda j, c: block(k_ref[j], vt_ref[j], c), carry)
        for j in range(nck):
            carry = block(ck_ref[j], cvt_ref[j], carry)
        finish(carry[2], carry[1])


def _da_key_block(t):
    return _largest_divisor(t, DA_KEY_BLOCKS)


def _da_call(lams, gain, qt, k5, vt5, ck5, cvt5, lam_init):
    b, h, dq, t = qt.shape
    tq = _largest_divisor(t, DA_QUERY_BLOCKS)
    small = pl.BlockSpec((1, DA_DH), lambda bi, hi, i: (0, 0))
    per_head = lambda a: pl.BlockSpec((None, None) + a.shape[2:], lambda bi, hi, i: (bi, hi, 0, 0, 0))
    return pl.pallas_call(
        functools.partial(_da_kernel, lam_init=lam_init),
        out_shape=jax.ShapeDtypeStruct((b, t, h * DA_DV), BF16),
        grid=(b, h, t // tq),
        in_specs=[small, small, small, small,
                  pl.BlockSpec((DA_DV, 1), lambda bi, hi, i: (0, 0)),
                  pl.BlockSpec((None, None, dq, tq), lambda bi, hi, i: (bi, hi, 0, i)),
                  per_head(k5), per_head(vt5), per_head(ck5), per_head(cvt5)],
        out_specs=pl.BlockSpec((None, tq, DA_DV), lambda bi, hi, i: (bi, i, hi)),
        scratch_shapes=[pltpu.VMEM((SUBLANES, dq), F32)],
        compiler_params=_params("parallel", "parallel", "arbitrary"),
        name="da",
    )(*lams, gain, qt, k5, vt5, ck5, cvt5)


def _merge_kernel(x_ref, gt_ref, yda_ref, of_ref, ob_ref, gout_ref, mrg_ref, ggain_ref, pgain_ref,
                  wda_ref, wgla_ref, wout_ref, o_ref):
    d = x_ref.shape[-1]
    o_gla = of_ref[...].astype(F32) + ob_ref[...].astype(F32)
    ggain = ggain_ref[...]
    heads = [_rms(o_gla[:, h * GLA_DV:(h + 1) * GLA_DV], ggain) for h in range(GLA_HEADS)]
    y_gla = (jnp.concatenate(heads, axis=-1) * gout_ref[...].astype(F32)).astype(BF16)
    gates = mrg_ref[...].astype(F32)
    y = (gates[:, :d] * _mm(yda_ref[...], wda_ref[...])
         + gates[:, d:] * _mm(y_gla, wgla_ref[...]))
    z = _mm(y.astype(BF16), wout_ref[...])
    o_ref[...] = x_ref[...] + gt_ref[...] * _rms(z, pgain_ref[...])


def _merge_call(x, gt1, y_da, o_f, o_b, gout, mrg, ggain, pgain, w_da, w_gla, w_out):
    b, t, d = x.shape
    tm = _largest_divisor(t, ROW_TILES)
    row = lambda width: pl.BlockSpec((None, tm, width), lambda bi, i: (bi, i, 0))
    vec = pl.BlockSpec((None, 1, d), lambda bi, i: (bi, 0, 0))
    return pl.pallas_call(
        _merge_kernel,
        out_shape=jax.ShapeDtypeStruct((b, t, d), F32),
        grid=(b, t // tm),
        in_specs=[row(d), vec, row(d), row(d), row(d), row(d), row(2 * d),
                  _resident(ggain.shape), _resident(pgain.shape),
                  _resident(w_da.shape), _resident(w_gla.shape), _resident(w_out.shape)],
        out_specs=row(d),
        compiler_params=_params("parallel", "parallel"),
        name="merge",
    )(x, gt1, y_da, o_f, o_b, gout, mrg, ggain, pgain, w_da, w_gla, w_out)


def _mlp_kernel(x_ref, sh_ref, sc_ref, gt_ref, pre_ref, post_ref, w1_ref, w2_ref, o_ref):
    x = x_ref[...]
    h = (_rms(x, pre_ref[...]) * (1.0 + sc_ref[...]) + sh_ref[...]).astype(BF16)
    u = jnp.maximum(_mm(h, w1_ref[...]), 0.0)
    z = _mm((u * u).astype(BF16), w2_ref[...])
    o_ref[...] = x + gt_ref[...] * _rms(z, post_ref[...])


def _mlp_call(x, sh2, sc2, gt2, pre, post, w1, w2):
    b, t, d = x.shape
    tm = _largest_divisor(t, ROW_TILES)
    row = pl.BlockSpec((None, tm, d), lambda bi, i: (bi, i, 0))
    vec = pl.BlockSpec((None, 1, d), lambda bi, i: (bi, 0, 0))
    return pl.pallas_call(
        _mlp_kernel,
        out_shape=jax.ShapeDtypeStruct((b, t, d), F32),
        grid=(b, t // tm),
        in_specs=[row, vec, vec, vec, _resident(pre.shape), _resident(post.shape),
                  _resident(w1.shape), _resident(w2.shape)],
        out_specs=row,
        compiler_params=_params("parallel", "parallel"),
        name="mlp",
    )(x, sh2, sc2, gt2, pre, post, w1, w2)


def _split_w_in(w):
    low0 = _OFF_LOW
    low1 = low0 + 2 * GLA_GATE_RANK
    low = jnp.pad(w[:, low0:low1], ((0, 0), (0, LOW_PAD - 2 * GLA_GATE_RANK)))
    return w[:, :low0].astype(BF16), low.astype(BF16), w[:, low1:].astype(BF16)


def _pack_gate_up(w_gate_up, b_gate_up):
    hk = GLA_HEADS * GLA_DK
    w = jnp.zeros((LOW_PAD, 2 * hk), F32)
    for z in range(2):
        w = w.at[z * GLA_GATE_RANK:(z + 1) * GLA_GATE_RANK, z * hk:(z + 1) * hk].set(w_gate_up[z])
    return w.astype(BF16), b_gate_up.reshape(1, 2 * hk).astype(F32)


def kernel(x, c, ctx, c_ctx, w_mod, b_mod, pre_norm1, w_in, w_gate_up, b_gate_up, lambda_q1, lambda_k1,
           lambda_q2, lambda_k2, da_head_norm, gla_head_norm, w_branch_da, w_branch_gla, w_out, post_norm1,
           pre_norm2, w_ff1, w_ff2, post_norm2):
    b, t, d = x.shape
    n_ctx = ctx.shape[1]
    assert w_in.shape[0] == 1, "single-layer block only"
    assert d == DA_HEADS * DA_DV == GLA_HEADS * GLA_DV and t % GRID_W == 0
    layer = 0
    lam_init = 0.8 - 0.6 * math.exp(-0.3 * layer)
    (w_mod, b_mod, pre_norm1, w_in, w_gate_up, b_gate_up, lambda_q1, lambda_k1, lambda_q2, lambda_k2,
     da_head_norm, gla_head_norm, w_branch_da, w_branch_gla, w_out, post_norm1, pre_norm2, w_ff1, w_ff2,
     post_norm2) = (a.reshape(a.shape[1:]) for a in (
         w_mod, b_mod, pre_norm1, w_in, w_gate_up, b_gate_up, lambda_q1, lambda_k1, lambda_q2, lambda_k2,
         da_head_norm, gla_head_norm, w_branch_da, w_branch_gla, w_out, post_norm1, pre_norm2, w_ff1,
         w_ff2, post_norm2))

    rows = -(-(b + 1) // SUBLANES) * SUBLANES
    cc = jnp.zeros((rows, d), F32).at[:b].set(c).at[b].set(c_ctx)
    mod = _mod_call(cc, w_mod, b_mod)
    lat = [mod[:b, i * d:(i + 1) * d].reshape(b, 1, d) for i in range(N_MOD)]
    cmod = [jnp.broadcast_to(mod[b, i * d:(i + 1) * d].reshape(1, 1, d), (b, 1, d)) for i in range(2)]
    sh1, sc1, gt1, sh2, sc2, gt2 = lat

    w_pieces = _split_w_in(w_in)
    w_up, b_up = _pack_gate_up(w_gate_up, b_gate_up)
    gain1 = pre_norm1.reshape(1, d)

    qt, k, vt5, gq, gk, gv, gate, gout, mrg = _proj_call(
        x, sh1, sc1, gain1, w_pieces, w_up, b_up, *_rope_tables(t))
    _, ck, cvt5, _, cgk, cgv, cgate, _, _ = _proj_call(
        ctx, cmod[0], cmod[1], gain1, w_pieces, w_up, b_up, *_identity_tables(n_ctx))

    zero_state = jnp.zeros((b, GLA_HEADS, GLA_DV, GLA_DK), F32)
    _, _, sf0, sb0 = _gla_call(jnp.zeros_like(cgk), cgk, cgv, cgate, zero_state, zero_state)
    o_f, o_b, _, _ = _gla_call(gq, gk, gv, gate, sf0, sb0)

    k5 = k.reshape(b, DA_HEADS, vt5.shape[2], vt5.shape[4], 2 * DA_DH)
    ck5 = ck.reshape(b, DA_HEADS, cvt5.shape[2], cvt5.shape[4], 2 * DA_DH)
    lams = [a.reshape(1, DA_DH) for a in (lambda_q1, lambda_k1, lambda_q2, lambda_k2)]
    y_da = _da_call(lams, da_head_norm.reshape(DA_DV, 1), qt, k5, vt5, ck5, cvt5, lam_init)

    x1 = _merge_call(x, gt1, y_da, o_f, o_b, gout, mrg,
                     gla_head_norm.reshape(1, GLA_DV), post_norm1.reshape(1, d),
                     w_branch_da.astype(BF16), w_branch_gla.astype(BF16),
                     w_out.astype(BF16))
    return _mlp_call(x1, sh2, sc2, gt2, pre_norm2.reshape(1, d), post_norm2.reshape(1, d),
                     w_ff1.astype(BF16), w_ff2.astype(BF16))
```
